```python
import jax, jax.numpy as jnp
from jax import lax
import numpy as np

D_MODEL = 1024
BATCH = 8
SEQ = 2048
DEPTH = 1

D_MIX = D_MODEL
N_DN_HEADS = 4
DN_HEAD_DIM = 128
D_DN = N_DN_HEADS * DN_HEAD_DIM
D_SC = D_MIX - D_DN
N_SC_GROUPS = 4
DN_CONV = 4
SC_CONV = 3
CHUNK = 64
D_FF = 4 * D_MODEL
EPS = 1e-6
D_IN = 4 * D_DN + 2 * N_DN_HEADS + 3 * D_SC
SPLIT_OFFSETS = (3 * D_DN, 4 * D_DN, 4 * D_DN + N_DN_HEADS, 4 * D_DN + 2 * N_DN_HEADS,
                 4 * D_DN + 2 * N_DN_HEADS + D_SC, 4 * D_DN + 2 * N_DN_HEADS + 2 * D_SC)

kernel_name = "hymba_deltanet_shortconv_adaln_sandwich"


def rmsnorm(x, w):
    xf = x.astype(jnp.float32)
    y = xf * lax.rsqrt(jnp.mean(xf * xf, axis=-1, keepdims=True) + EPS)
    return (y * w.astype(jnp.float32)).astype(x.dtype)


def l2norm(t):
    return t * lax.rsqrt(jnp.sum(t * t, axis=-1, keepdims=True) + EPS)


def causal_depthwise_conv(x, w):
    width = w.shape[0]
    return lax.conv_general_dilated(
        x, w[:, None, :].astype(x.dtype), window_strides=(1,), padding=[(width - 1, 0)],
        dimension_numbers=('NWC', 'WIO', 'NWC'), feature_group_count=x.shape[-1])


def gated_delta_rule_chunked(q, k, v, g, beta):
    bsz, seq, nh, dh = q.shape
    n = seq // CHUNK

    def chunks(t):
        t = t.reshape((bsz, n, CHUNK, nh) + t.shape[3:])
        return jnp.moveaxis(jnp.moveaxis(t, 1, 0), 3, 2)

    q = chunks(q * (dh ** -0.5))
    k = chunks(k)
    v = chunks(v)
    beta = chunks(beta)
    g_cum = jnp.cumsum(chunks(g), axis=-1)

    tri_incl = jnp.tril(jnp.ones((CHUNK, CHUNK), dtype=bool))
    tri_strict = jnp.tril(jnp.ones((CHUNK, CHUNK), dtype=bool), -1)
    diff = g_cum[..., :, None] - g_cum[..., None, :]
    decay = jnp.exp(jnp.where(tri_incl, diff, -jnp.inf))

    k_beta = k * beta[..., None]
    v_beta = v * beta[..., None]
    m = jnp.where(tri_strict, jnp.einsum('nbhid,nbhjd->nbhij', k_beta, k) * decay, 0.0)
    eye = jnp.eye(CHUNK, dtype=m.dtype)
    t_inv = lax.linalg.triangular_solve(m + eye, jnp.broadcast_to(eye, m.shape),
                                        left_side=True, lower=True, unit_diagonal=True)
    u = jnp.einsum('nbhij,nbhjd->nbhid', t_inv, v_beta)
    w = jnp.einsum('nbhij,nbhjd->nbhid', t_inv, k_beta * jnp.exp(g_cum)[..., None])
    attn = jnp.einsum('nbhid,nbhjd->nbhij', q, k) * decay

    def step(state, inp):
        q_c, k_c, u_c, w_c, a_c, g_c = inp
        v_new = u_c - jnp.einsum('bhcd,bhde->bhce', w_c, state)
        o_c = (jnp.einsum('bhcd,bhde->bhce', q_c * jnp.exp(g_c)[..., None], state)
               + jnp.einsum('bhij,bhje->bhie', a_c, v_new))
        g_last = g_c[..., -1]
        k_dec = k_c * jnp.exp(g_last[..., None] - g_c)[..., None]
        state = (state * jnp.exp(g_last)[..., None, None]
                 + jnp.einsum('bhcd,bhce->bhde', k_dec, v_new))
        return state, o_c

    s0 = jnp.zeros((bsz, nh, dh, dh), dtype=q.dtype)
    _, o = lax.scan(step, s0, (q, k, u, w, attn, g_cum))
    o = jnp.moveaxis(jnp.moveaxis(o, 2, 3), 0, 1)
    return o.reshape(bsz, seq, nh, dh)


def token_mixer(h, w_in, dn_conv_w, dn_a_log, dn_dt_bias, dn_norm_w, sc_conv_w, w_out):
    bsz, seq, _ = h.shape
    proj = h @ w_in
    qkv, z, a, b, s_b, s_c, s_h = jnp.split(proj, SPLIT_OFFSETS, axis=-1)

    qkv = jax.nn.silu(causal_depthwise_conv(qkv, dn_conv_w)).astype(jnp.float32)
    q, k, v = [t.reshape(bsz, seq, N_DN_HEADS, DN_HEAD_DIM) for t in jnp.split(qkv, 3, axis=-1)]
    q, k = l2norm(q), l2norm(k)
    g = -jnp.exp(dn_a_log.astype(jnp.float32)) * jax.nn.softplus(
        a.astype(jnp.float32) + dn_dt_bias.astype(jnp.float32))
    beta = jax.nn.sigmoid(b.astype(jnp.float32))
    o = gated_delta_rule_chunked(q, k, v, g, beta)
    zg = jax.nn.silu(z.astype(jnp.float32).reshape(bsz, seq, N_DN_HEADS, DN_HEAD_DIM))
    o_dn = (rmsnorm(o, dn_norm_w) * zg).astype(h.dtype).reshape(bsz, seq, D_DN)

    y_sc = s_b * causal_depthwise_conv(s_c * s_h, sc_conv_w)

    return jnp.concatenate([o_dn, y_sc], axis=-1) @ w_out


def setup_inputs(seed: int = 0) -> dict:
    key = jax.random.key(seed)
    ks = jax.random.split(key, 20)
    L, D = DEPTH, D_MODEL

    def nrm(k, shape, scale):
        return jax.random.normal(k, shape, jnp.float32) * scale

    def gain(k, shape):
        return 1.0 + 0.02 * jax.random.normal(k, shape, jnp.float32)

    dt = jnp.exp(jax.random.uniform(ks[9], (L, N_DN_HEADS), jnp.float32,
                                    float(np.log(1e-3)), float(np.log(1e-1))))
    return {
        "x": jax.random.normal(ks[0], (BATCH, SEQ, D), jnp.float32),
        "c": jax.random.normal(ks[1], (BATCH, D), jnp.float32),
        "w_ada": nrm(ks[2], (L, D, 6 * D), 0.5 * D ** -0.5),
        "b_ada": nrm(ks[3], (L, 6 * D), 0.01),
        "pre_mix_norm_w": gain(ks[4], (L, D)),
        "post_mix_norm_w": gain(ks[5], (L, D)),
        "w_in": nrm(ks[6], (L, D, D_IN), D ** -0.5),
        "dn_conv_w": nrm(ks[7], (L, DN_CONV, 3 * D_DN), DN_CONV ** -0.5),
        "dn_a_log": jnp.log(jax.random.uniform(ks[8], (L, N_DN_HEADS), jnp.float32, 1.0, 16.0)),
        "dn_dt_bias": dt + jnp.log(-jnp.expm1(-dt)),
        "dn_norm_w": gain(ks[10], (L, DN_HEAD_DIM)),
        "sc_conv_w": nrm(ks[11], (L, SC_CONV, D_SC), SC_CONV ** -0.5),
        "w_out": nrm(ks[12], (L, D_MIX, D), D_MIX ** -0.5),
        "pre_ffn_norm_w": gain(ks[13], (L, D)),
        "post_ffn_norm_w": gain(ks[14], (L, D)),
        "w_ff1": nrm(ks[15], (L, D, D_FF), D ** -0.5),
        "w_ff2": nrm(ks[16], (L, D_FF, D), D_FF ** -0.5),
    }


def reference(x, c, w_ada, b_ada, pre_mix_norm_w, post_mix_norm_w, w_in, dn_conv_w,
              dn_a_log, dn_dt_bias, dn_norm_w, sc_conv_w, w_out, pre_ffn_norm_w,
              post_ffn_norm_w, w_ff1, w_ff2):
    c_act = jax.nn.silu(c)
    for l in range(DEPTH):
        mod = (c_act @ w_ada[l] + b_ada[l])[:, None, :]
        sh_m, sc_m, g_m, sh_f, sc_f, g_f = jnp.split(mod, 6, axis=-1)

        h = rmsnorm(x, pre_mix_norm_w[l]) * (1.0 + sc_m) + sh_m
        y = token_mixer(h, w_in[l], dn_conv_w[l], dn_a_log[l], dn_dt_bias[l],
                        dn_norm_w[l], sc_conv_w[l], w_out[l])
        x = x + g_m * rmsnorm(y, post_mix_norm_w[l])

        h = rmsnorm(x, pre_ffn_norm_w[l]) * (1.0 + sc_f) + sh_f
        y = jnp.square(jax.nn.relu(h @ w_ff1[l])) @ w_ff2[l]
        x = x + g_f * rmsnorm(y, post_ffn_norm_w[l])
    return x
```

```python
import functools

import jax
import jax.numpy as jnp
from jax import lax
from jax.experimental import pallas as pl
from jax.experimental.pallas import tpu as pltpu

F32 = jnp.float32
BF16 = jnp.bfloat16

D_MODEL = 1024
N_HEADS = 4
HEAD_DIM = 128
D_DN = N_HEADS * HEAD_DIM
D_SC = D_MODEL - D_DN
DN_CONV = 4
SC_CONV = 3
CHUNK = 64
D_FF = 4 * D_MODEL
EPS = 1e-6
N_QKV = 3 * D_DN
N_MAIN = 4 * D_DN + 3 * D_SC
COL_Z = N_QKV
COL_SB = 4 * D_DN
COL_SC = COL_SB + D_SC
COL_SH = COL_SC + D_SC
LANES = 128
HALO = 8

TM_IN = 512
TQ_MIX = 256
TM_FFN = 512
FF_CHUNK = 1024
VMEM_LIMIT = 56 * 1024 * 1024


def _sigmoid(x):
    return 1.0 / (1.0 + jnp.exp(-x))


def _mm(a, b):
    return jnp.dot(a.astype(BF16), b.astype(BF16), preferred_element_type=F32)


def _mm_nt(a, b):
    return lax.dot_general(a.astype(BF16), b.astype(BF16), (((1,), (1,)), ((), ())),
                           preferred_element_type=F32)


def _mm_tn(a, b):
    return lax.dot_general(a.astype(BF16), b.astype(BF16), (((0,), (0,)), ((), ())),
                           preferred_element_type=F32)


def _rms_scale(x):
    return lax.rsqrt(jnp.mean(x * x, axis=-1, keepdims=True) + EPS)


def _ada_kernel(c_ref, w_ref, b_ref, o_ref):
    c = c_ref[...]
    ca = (c * _sigmoid(c)).astype(BF16)
    o_ref[...] = jnp.dot(ca, w_ref[...].astype(BF16), preferred_element_type=F32) + b_ref[...]


def _ada(c, w_ada, b_ada):
    bsz = c.shape[0]
    n = w_ada.shape[1]
    bn = 1024
    return pl.pallas_call(
        _ada_kernel,
        grid=(n // bn,),
        in_specs=[
            pl.BlockSpec((bsz, D_MODEL), lambda j: (0, 0)),
            pl.BlockSpec((D_MODEL, bn), lambda j: (0, j)),
            pl.BlockSpec((1, bn), lambda j: (0, j)),
        ],
        out_specs=pl.BlockSpec((bsz, bn), lambda j: (0, j)),
        out_shape=jax.ShapeDtypeStruct((bsz, n), F32),
        compiler_params=pltpu.CompilerParams(dimension_semantics=("arbitrary",),
                                             vmem_limit_bytes=VMEM_LIMIT),
        name="ada_mod",
    )(c, w_ada, b_ada)


def _in_kernel(x_ref, mod_ref, nw_ref, wm_ref, wab_ref, proj_ref, ab_ref):
    x = x_ref[...]
    h = x * _rms_scale(x) * nw_ref[...]
    h = h * (1.0 + mod_ref[1:2, :]) + mod_ref[0:1, :]
    hb = h.astype(BF16)
    proj_ref[...] = jnp.dot(hb, wm_ref[...], preferred_element_type=F32).astype(BF16)
    ab_ref[...] = jnp.dot(hb, wab_ref[...], preferred_element_type=F32)


def _in_proj(x2, mod3, nw, w_main, w_ab, seq):
    t = x2.shape[0]
    tm = TM_IN
    per_b = seq // tm
    return pl.pallas_call(
        _in_kernel,
        grid=(t // tm,),
        in_specs=[
            pl.BlockSpec((tm, D_MODEL), lambda i: (i, 0)),
            pl.BlockSpec((None, 6, D_MODEL), lambda i: (i // per_b, 0, 0)),
            pl.BlockSpec((1, D_MODEL), lambda i: (0, 0)),
            pl.BlockSpec((D_MODEL, N_MAIN), lambda i: (0, 0)),
            pl.BlockSpec((D_MODEL, LANES), lambda i: (0, 0)),
        ],
        out_specs=[
            pl.BlockSpec((tm, N_MAIN), lambda i: (i, 0)),
            pl.BlockSpec((tm, LANES), lambda i: (i, 0)),
        ],
        out_shape=[
            jax.ShapeDtypeStruct((t, N_MAIN), BF16),
            jax.ShapeDtypeStruct((t, LANES), F32),
        ],
        compiler_params=pltpu.CompilerParams(dimension_semantics=("arbitrary",),
                                             vmem_limit_bytes=VMEM_LIMIT),
        name="in_proj",
    )(x2, mod3, nw, w_main, w_ab)


def _mix_kernel(proj_ref, halo_ref, ab_ref, x_ref, mod_ref, cw_ref, scw_ref, gp_ref, dnw_ref,
                wout_ref, pw_ref, out_ref,
                ext_s, q_s, k_s, v_s, o_s, g_s, b_s, state_s):
    tq = x_ref.shape[0]
    t = pl.program_id(1)

    @pl.when(t == 0)
    def _():
        state_s[...] = jnp.zeros_like(state_s)

    halo_on = (t > 0).astype(F32)

    for part, dst in enumerate((q_s, k_s, v_s)):
        c0 = part * D_DN
        ext_s[0:HALO, :] = halo_ref[:, c0:c0 + D_DN].astype(F32) * halo_on
        ext_s[HALO:HALO + tq, :] = proj_ref[:, c0:c0 + D_DN].astype(F32)
        acc = cw_ref[DN_CONV - 1:DN_CONV, c0:c0 + D_DN] * ext_s[HALO:HALO + tq, :]
        for j in range(DN_CONV - 1):
            off = HALO - (DN_CONV - 1) + j
            acc = acc + cw_ref[j:j + 1, c0:c0 + D_DN] * ext_s[off:off + tq, :]
        y = acc * _sigmoid(acc)
        if part < 2:
            for h in range(N_HEADS):
                yh = y[:, h * HEAD_DIM:(h + 1) * HEAD_DIM]
                nrm = lax.rsqrt(jnp.sum(yh * yh, axis=-1, keepdims=True) + EPS)
                if part == 0:
                    nrm = nrm * (HEAD_DIM ** -0.5)
                dst[:, h * HEAD_DIM:(h + 1) * HEAD_DIM] = yh * nrm
        else:
            dst[...] = y

    ab = ab_ref[...]
    xg = ab + gp_ref[1:2, :]
    softplus = jnp.maximum(xg, 0.0) + jnp.log1p(jnp.exp(-jnp.abs(xg)))
    g_s[...] = -jnp.exp(gp_ref[0:1, :]) * softplus
    b_s[...] = _sigmoid(ab)

    row = lax.broadcasted_iota(jnp.int32, (CHUNK, CHUNK), 0)
    col = lax.broadcasted_iota(jnp.int32, (CHUNK, CHUNK), 1)
    tri_incl = row >= col
    tri_strict = row > col
    eye = row == col
    eye_f = eye.astype(F32)
    row_l = lax.broadcasted_iota(jnp.int32, (CHUNK, LANES), 0)

    def level_mask(l):
        return (((row >> l) & 1) == 1) & (((col >> l) & 1) == 0) & ((row >> (l + 1)) == (col >> (l + 1)))

    def chunk_body(c, carry):
        r0 = pl.multiple_of(c * CHUNK, CHUNK)
        gcum = g_s[pl.ds(r0, CHUNK), :]
        for s in (1, 2, 4, 8, 16, 32):
            gcum = gcum + jnp.where(row_l >= s, pltpu.roll(gcum, s, axis=0), 0.0)
        bt = b_s[pl.ds(r0, CHUNK), :]
        for h in range(N_HEADS):
            hs = slice(h * HEAD_DIM, (h + 1) * HEAD_DIM)
            gc = gcum[:, h:h + 1]
            beta = bt[:, N_HEADS + h:N_HEADS + h + 1]
            g_i = jnp.broadcast_to(gc, (CHUNK, CHUNK))
            g_j = jnp.sum(jnp.where(eye, g_i, 0.0), axis=0, keepdims=True)
            decay = jnp.exp(jnp.where(tri_incl, g_i - g_j, -1e30))
            g_last = g_j[:, CHUNK - 1:CHUNK]
            e_gc = jnp.exp(gc)
            q = q_s[pl.ds(r0, CHUNK), hs]
            k = k_s[pl.ds(r0, CHUNK), hs]
            v = v_s[pl.ds(r0, CHUNK), hs]
            kb = k * beta
            vb = v * beta
            m_st = jnp.where(tri_strict, _mm_nt(kb, k) * decay, 0.0)
            x_inv = eye_f - jnp.where(level_mask(0), m_st, 0.0)
            for l in range(1, 6):
                c_l = jnp.where(level_mask(l), m_st, 0.0)
                x_inv = x_inv - _mm(_mm(x_inv, c_l), x_inv)
            uw = _mm(x_inv, jnp.concatenate([vb, kb * e_gc], axis=1))
            u = uw[:, :HEAD_DIM]
            w = uw[:, HEAD_DIM:]
            attn = _mm_nt(q, k) * decay
            state = state_s[h]
            ws = _mm(jnp.concatenate([w, q * e_gc], axis=0), state)
            v_new = u - ws[:CHUNK]
            o_s[pl.ds(r0, CHUNK), hs] = ws[CHUNK:] + _mm(attn, v_new)
            k_dec = k * jnp.exp(g_last - gc)
            state_s[h] = state * jnp.exp(g_last) + _mm_tn(k_dec, v_new)
        return carry

    lax.fori_loop(0, tq // CHUNK, chunk_body, 0)

    z = proj_ref[:, COL_Z:COL_Z + D_DN].astype(F32)
    zg = z * _sigmoid(z)
    o_parts = []
    for h in range(N_HEADS):
        hs = slice(h * HEAD_DIM, (h + 1) * HEAD_DIM)
        oh = o_s[:, hs]
        o_parts.append(oh * _rms_scale(oh) * dnw_ref[...] * zg[:, hs])
    o_dn = jnp.concatenate(o_parts, axis=1)

    ext_s[0:HALO, :] = (halo_ref[:, COL_SC:COL_SC + D_SC].astype(F32)
                        * halo_ref[:, COL_SH:COL_SH + D_SC].astype(F32) * halo_on)
    ext_s[HALO:HALO + tq, :] = (proj_ref[:, COL_SC:COL_SC + D_SC].astype(F32)
                                * proj_ref[:, COL_SH:COL_SH + D_SC].astype(F32))
    conv = scw_ref[SC_CONV - 1:SC_CONV, :] * ext_s[HALO:HALO + tq, :]
    for j in range(SC_CONV - 1):
        off = HALO - (SC_CONV - 1) + j
        conv = conv + scw_ref[j:j + 1, :] * ext_s[off:off + tq, :]
    y_sc = proj_ref[:, COL_SB:COL_SB + D_SC].astype(F32) * conv

    y = (jnp.dot(o_dn.astype(BF16), wout_ref[0:D_DN, :], preferred_element_type=F32)
         + jnp.dot(y_sc.astype(BF16), wout_ref[D_DN:D_MODEL, :], preferred_element_type=F32))
    out_ref[...] = x_ref[...] + mod_ref[2:3, :] * (y * _rms_scale(y) * pw_ref[...])


def _mixer(proj, ab, x2, mod3, cw, scw, gp, dnw, w_out, pw, bsz, seq):
    tq = TQ_MIX
    per_b = seq // tq
    t = x2.shape[0]

    def tile(b, s):
        return (b * per_b + s, 0)

    def halo(b, s):
        return (jnp.maximum((b * per_b + s) * (tq // HALO) - 1, 0), 0)

    const = lambda b, s: (0, 0)
    return pl.pallas_call(
        _mix_kernel,
        grid=(bsz, per_b),
        in_specs=[
            pl.BlockSpec((tq, N_MAIN), tile),
            pl.BlockSpec((HALO, N_MAIN), halo),
            pl.BlockSpec((tq, LANES), tile),
            pl.BlockSpec((tq, D_MODEL), tile),
            pl.BlockSpec((None, 6, D_MODEL), lambda b, s: (b, 0, 0)),
            pl.BlockSpec((DN_CONV, N_QKV), const),
            pl.BlockSpec((SC_CONV, D_SC), const),
            pl.BlockSpec((8, LANES), const),
            pl.BlockSpec((1, HEAD_DIM), const),
            pl.BlockSpec((D_MODEL, D_MODEL), const),
            pl.BlockSpec((1, D_MODEL), const),
        ],
        out_specs=pl.BlockSpec((tq, D_MODEL), tile),
        out_shape=jax.ShapeDtypeStruct((t, D_MODEL), F32),
        scratch_shapes=[
            pltpu.VMEM((HALO + tq, D_DN), F32),
            pltpu.VMEM((tq, D_DN), F32),
            pltpu.VMEM((tq, D_DN), F32),
            pltpu.VMEM((tq, D_DN), F32),
            pltpu.VMEM((tq, D_DN), F32),
            pltpu.VMEM((tq, LANES), F32),
            pltpu.VMEM((tq, LANES), F32),
            pltpu.VMEM((N_HEADS, HEAD_DIM, HEAD_DIM), F32),
        ],
        compiler_params=pltpu.CompilerParams(dimension_semantics=("arbitrary", "arbitrary"),
                                             vmem_limit_bytes=VMEM_LIMIT),
        name="token_mixer",
    )(proj, proj, ab, x2, mod3, cw, scw, gp, dnw, w_out, pw)


def _ffn_kernel(x_ref, mod_ref, nw_ref, w1_ref, w2_ref, pw_ref, out_ref):
    x = x_ref[...]
    h = x * _rms_scale(x) * nw_ref[...]
    h = h * (1.0 + mod_ref[4:5, :]) + mod_ref[3:4, :]
    hb = h.astype(BF16)
    y = jnp.zeros(x.shape, F32)
    for j in range(D_FF // FF_CHUNK):
        a = jnp.dot(hb, w1_ref[:, j * FF_CHUNK:(j + 1) * FF_CHUNK], preferred_element_type=F32)
        a = jnp.square(jnp.maximum(a, 0.0)).astype(BF16)
        y = y + jnp.dot(a, w2_ref[j * FF_CHUNK:(j + 1) * FF_CHUNK, :], preferred_element_type=F32)
    out_ref[...] = x + mod_ref[5:6, :] * (y * _rms_scale(y) * pw_ref[...])


def _ffn(x2, mod3, nw, w1, w2, pw, seq):
    t = x2.shape[0]
    tm = TM_FFN
    per_b = seq // tm
    return pl.pallas_call(
        _ffn_kernel,
        grid=(t // tm,),
        in_specs=[
            pl.BlockSpec((tm, D_MODEL), lambda i: (i, 0)),
            pl.BlockSpec((None, 6, D_MODEL), lambda i: (i // per_b, 0, 0)),
            pl.BlockSpec((1, D_MODEL), lambda i: (0, 0)),
            pl.BlockSpec((D_MODEL, D_FF), lambda i: (0, 0)),
            pl.BlockSpec((D_FF, D_MODEL), lambda i: (0, 0)),
            pl.BlockSpec((1, D_MODEL), lambda i: (0, 0)),
        ],
        out_specs=pl.BlockSpec((tm, D_MODEL), lambda i: (i, 0)),
        out_shape=jax.ShapeDtypeStruct((t, D_MODEL), F32),
        compiler_params=pltpu.CompilerParams(dimension_semantics=("arbitrary",),
                                             vmem_limit_bytes=VMEM_LIMIT),
        name="ffn",
    )(x2, mod3, nw, w1, w2, pw)


def _layer(x2, c, w_ada, b_ada, pre_mix_w, post_mix_w, w_in, dn_conv_w, dn_a_log, dn_dt_bias,
           dn_norm_w, sc_conv_w, w_out, pre_ffn_w, post_ffn_w, w_ff1, w_ff2, bsz, seq):
    mod3 = _ada(c, w_ada, b_ada[None, :]).reshape(bsz, 6, D_MODEL)

    n_gate = 2 * N_HEADS
    w_main = jnp.concatenate([w_in[:, :COL_SB], w_in[:, COL_SB + n_gate:]], axis=1).astype(BF16)
    w_ab = jnp.pad(w_in[:, COL_SB:COL_SB + n_gate], ((0, 0), (0, LANES - n_gate))).astype(BF16)
    proj, ab = _in_proj(x2, mod3, pre_mix_w[None, :], w_main, w_ab, seq)

    gp = jnp.zeros((8, LANES), F32)
    gp = gp.at[0, :N_HEADS].set(dn_a_log).at[1, :N_HEADS].set(dn_dt_bias)
    x2 = _mixer(proj, ab, x2, mod3, dn_conv_w, sc_conv_w, gp, dn_norm_w[None, :],
                w_out.astype(BF16), post_mix_w[None, :], bsz, seq)

    return _ffn(x2, mod3, pre_ffn_w[None, :], w_ff1.astype(BF16), w_ff2.astype(BF16),
                post_ffn_w[None, :], seq)


def kernel(x, c, w_ada, b_ada, pre_mix_norm_w, post_mix_norm_w, w_in, dn_conv_w, dn_a_log, dn_dt_bias, dn_norm_w, sc_conv_w, w_out, pre_ffn_norm_w, post_ffn_norm_w, w_ff1, w_ff2):
    bsz, seq, d = x.shape
    x2 = x.reshape(bsz * seq, d)
    for l in range(w_ada.shape[0]):
        x2 = _layer(x2, c, w_ada[l], b_ada[l], pre_mix_norm_w[l], post_mix_norm_w[l], w_in[l],
                    dn_conv_w[l], dn_a_log[l], dn_dt_bias[l], dn_norm_w[l], sc_conv_w[l], w_out[l],
                    pre_ffn_norm_w[l], post_ffn_norm_w[l], w_ff1[l], w_ff2[l], bsz, seq)
    return x2.reshape(bsz, seq, d)
```

```python
import functools

import jax
import jax.numpy as jnp
from jax import lax
from jax.experimental import pallas as pl
from jax.experimental.pallas import tpu as pltpu

F32 = jnp.float32
BF16 = jnp.bfloat16

D_MODEL = 1024
N_HEADS = 4
HEAD_DIM = 128
D_DN = N_HEADS * HEAD_DIM
D_SC = D_MODEL - D_DN
DN_CONV = 4
SC_CONV = 3
CHUNK = 64
D_FF = 4 * D_MODEL
EPS = 1e-6
N_QKV = 3 * D_DN
N_MAIN = 4 * D_DN + 3 * D_SC
COL_Z = N_QKV
COL_SB = 4 * D_DN
COL_SC = COL_SB + D_SC
COL_SH = COL_SC + D_SC
LANES = 128
HALO = 8

TM_IN = 512
TQ_MIX = 256
TM_FFN = 512
FF_CHUNK = 1024
VMEM_LIMIT = 56 * 1024 * 1024


def _sigmoid(x):
    return 1.0 / (1.0 + jnp.exp(-x))


def _mm(a, b):
    return jnp.dot(a.astype(BF16), b.astype(BF16), preferred_element_type=F32)


def _mm_nt(a, b):
    return lax.dot_general(a.astype(BF16), b.astype(BF16), (((1,), (1,)), ((), ())),
                           preferred_element_type=F32)


def _mm_tn(a, b):
    return lax.dot_general(a.astype(BF16), b.astype(BF16), (((0,), (0,)), ((), ())),
                           preferred_element_type=F32)


def _rms_scale(x):
    return lax.rsqrt(jnp.mean(x * x, axis=-1, keepdims=True) + EPS)


def _ada_kernel(c_ref, w_ref, b_ref, o_ref):
    c = c_ref[...]
    ca = (c * _sigmoid(c)).astype(BF16)
    o_ref[...] = jnp.dot(ca, w_ref[...].astype(BF16), preferred_element_type=F32) + b_ref[...]


def _ada(c, w_ada, b_ada):
    bsz = c.shape[0]
    n = w_ada.shape[1]
    bn = 1024
    return pl.pallas_call(
        _ada_kernel,
        grid=(n // bn,),
        in_specs=[
            pl.BlockSpec((bsz, D_MODEL), lambda j: (0, 0)),
            pl.BlockSpec((D_MODEL, bn), lambda j: (0, j)),
            pl.BlockSpec((1, bn), lambda j: (0, j)),
        ],
        out_specs=pl.BlockSpec((bsz, bn), lambda j: (0, j)),
        out_shape=jax.ShapeDtypeStruct((bsz, n), F32),
        compiler_params=pltpu.CompilerParams(dimension_semantics=("arbitrary",),
                                             vmem_limit_bytes=VMEM_LIMIT),
        name="ada_mod",
    )(c, w_ada, b_ada)


def _in_kernel(x_ref, mod_ref, nw_ref, wm_ref, wab_ref, proj_ref, ab_ref):
    x = x_ref[...]
    h = x * _rms_scale(x) * nw_ref[...]
    h = h * (1.0 + mod_ref[1:2, :]) + mod_ref[0:1, :]
    hb = h.astype(BF16)
    proj_ref[...] = jnp.dot(hb, wm_ref[...], preferred_element_type=F32).astype(BF16)
    ab_ref[...] = jnp.dot(hb, wab_ref[...], preferred_element_type=F32)


def _in_proj(x2, mod3, nw, w_main, w_ab, seq):
    t = x2.shape[0]
    tm = TM_IN
    per_b = seq // tm
    return pl.pallas_call(
        _in_kernel,
        grid=(t // tm,),
        in_specs=[
            pl.BlockSpec((tm, D_MODEL), lambda i: (i, 0)),
            pl.BlockSpec((None, 6, D_MODEL), lambda i: (i // per_b, 0, 0)),
            pl.BlockSpec((1, D_MODEL), lambda i: (0, 0)),
            pl.BlockSpec((D_MODEL, N_MAIN), lambda i: (0, 0)),
            pl.BlockSpec((D_MODEL, LANES), lambda i: (0, 0)),
        ],
        out_specs=[
            pl.BlockSpec((tm, N_MAIN), lambda i: (i, 0)),
            pl.BlockSpec((tm, LANES), lambda i: (i, 0)),
        ],
        out_shape=[
            jax.ShapeDtypeStruct((t, N_MAIN), BF16),
            jax.ShapeDtypeStruct((t, LANES), F32),
        ],
        compiler_params=pltpu.CompilerParams(dimension_semantics=("arbitrary",),
                                             vmem_limit_bytes=VMEM_LIMIT),
        name="in_proj",
    )(x2, mod3, nw, w_main, w_ab)


def _mix_kernel(proj_ref, halo_ref, ab_ref, x_ref, mod_ref, cw_ref, scw_ref, gp_ref, dnw_ref,
                wout_ref, pw_ref, out_ref,
                ext_s, q_s, k_s, v_s, o_s, g_s, b_s, state_s, aq_s, bb_s, op_s, gam_s, bdm_s, lvl_s):
    tq = x_ref.shape[0]
    t = pl.program_id(1)

    @pl.when(t == 0)
    def _():
        state_s[...] = jnp.zeros_like(state_s)

    halo_on = (t > 0).astype(F32)

    for part, dst in enumerate((q_s, k_s, v_s)):
        c0 = part * D_DN
        ext_s[0:HALO, :] = halo_ref[:, c0:c0 + D_DN].astype(F32) * halo_on
        ext_s[HALO:HALO + tq, :] = proj_ref[:, c0:c0 + D_DN].astype(F32)
        acc = cw_ref[DN_CONV - 1:DN_CONV, c0:c0 + D_DN] * ext_s[HALO:HALO + tq, :]
        for j in range(DN_CONV - 1):
            off = HALO - (DN_CONV - 1) + j
            acc = acc + cw_ref[j:j + 1, c0:c0 + D_DN] * ext_s[off:off + tq, :]
        y = acc * _sigmoid(acc)
        if part < 2:
            for h in range(N_HEADS):
                yh = y[:, h * HEAD_DIM:(h + 1) * HEAD_DIM]
                nrm = lax.rsqrt(jnp.sum(yh * yh, axis=-1, keepdims=True) + EPS)
                if part == 0:
                    nrm = nrm * (HEAD_DIM ** -0.5)
                dst[:, h * HEAD_DIM:(h + 1) * HEAD_DIM] = yh * nrm
        else:
            dst[...] = y

    ab = ab_ref[...]
    xg = ab + gp_ref[1:2, :]
    softplus = jnp.maximum(xg, 0.0) + jnp.log1p(jnp.exp(-jnp.abs(xg)))
    g_s[...] = -jnp.exp(gp_ref[0:1, :]) * softplus
    b_s[...] = _sigmoid(ab)

    n_chunks = tq // CHUNK
    quad = N_HEADS * CHUNK
    row_q = lax.broadcasted_iota(jnp.int32, (CHUNK, quad), 0)
    lane_q = lax.broadcasted_iota(jnp.int32, (CHUNK, quad), 1)
    col_q = lane_q & (CHUNK - 1)
    head_q = lane_q >> 6
    tri_incl = row_q >= col_q
    tri_strict = row_q > col_q
    eye_q = row_q == col_q
    eye_f = eye_q.astype(F32)
    row_l = lax.broadcasted_iota(jnp.int32, (CHUNK, LANES), 0)

    def level_mask(r, c, l):
        return (((r >> l) & 1) == 1) & (((c >> l) & 1) == 0) & ((r >> (l + 1)) == (c >> (l + 1)))

    @pl.when((pl.program_id(0) == 0) & (t == 0))
    def _():
        r2 = lax.broadcasted_iota(jnp.int32, (quad, quad), 0)
        c2 = lax.broadcasted_iota(jnp.int32, (quad, quad), 1)
        same_head = (r2 >> 6) == (c2 >> 6)
        bdm_s[...] = same_head.astype(F32).astype(BF16)
        for l in range(1, 6):
            lm = same_head & level_mask(r2 & (CHUNK - 1), c2 & (CHUNK - 1), l)
            lvl_s[l - 1] = lm.astype(F32).astype(BF16)

    lm0_f = level_mask(row_q, col_q, 0).astype(F32)
    zero_blk = jnp.zeros((CHUNK, HEAD_DIM), BF16)

    def head_bcast(cols, width):
        if width == HEAD_DIM:
            return jnp.concatenate([jnp.broadcast_to(c_, (CHUNK, HEAD_DIM)) for c_ in cols], axis=1)
        full = [jnp.broadcast_to(c_, (CHUNK, quad)) for c_ in cols]
        return jnp.where(head_q == 0, full[0],
                         jnp.where(head_q == 1, full[1], jnp.where(head_q == 2, full[2], full[3])))

    def bd_wide(a):
        rows = []
        for h in range(N_HEADS):
            rows.append(jnp.concatenate(
                [a[:, j * HEAD_DIM:(j + 1) * HEAD_DIM] if j == h else zero_blk for j in range(N_HEADS)],
                axis=1))
        return jnp.concatenate(rows, axis=0)

    def tile4(a):
        return jnp.concatenate([a] * N_HEADS, axis=0)

    for c in range(n_chunks):
        rs = slice(c * CHUNK, (c + 1) * CHUNK)
        gcum = g_s[rs, :]
        for s in (1, 2, 4, 8, 16, 32):
            gcum = gcum + jnp.where(row_l >= s, pltpu.roll(gcum, s, axis=0), 0.0)
        bt = b_s[rs, :]
        g_cols = [gcum[:, h:h + 1] for h in range(N_HEADS)]
        b_cols = [bt[:, N_HEADS + h:N_HEADS + h + 1] for h in range(N_HEADS)]
        g_i = head_bcast(g_cols, CHUNK)
        b_i = head_bcast(b_cols, CHUNK)
        g_w = head_bcast(g_cols, HEAD_DIM)
        g_j = jnp.sum(jnp.where(eye_q, g_i, 0.0), axis=0, keepdims=True)
        b_j = jnp.sum(jnp.where(eye_q, b_i, 0.0), axis=0, keepdims=True)
        g_last = g_w[CHUNK - 1:CHUNK, :]
        decay = jnp.exp(jnp.where(tri_incl, g_i - g_j, -1e30))

        q = q_s[rs, :]
        k = k_s[rs, :]
        k16 = k.astype(BF16)
        kq = jnp.concatenate([k16, q.astype(BF16)], axis=0)
        kq_k = lax.dot_general(kq, bd_wide(k16), (((1,), (1,)), ((), ())),
                               preferred_element_type=F32)
        m_st = jnp.where(tri_strict, kq_k[:CHUNK] * b_i * decay, 0.0)
        attn = kq_k[CHUNK:] * decay

        x_inv = eye_f - m_st * lm0_f
        m_t = tile4(m_st.astype(BF16))
        for l in range(1, 6):
            x16 = x_inv.astype(BF16)
            y = jnp.dot(x16, m_t * lvl_s[l - 1], preferred_element_type=F32)
            z = jnp.dot(y.astype(BF16), tile4(x16) * bdm_s[...], preferred_element_type=F32)
            x_inv = x_inv - z

        xb = x_inv * b_j
        u = jnp.dot(xb.astype(BF16), bd_wide(v_s[rs, :].astype(BF16)), preferred_element_type=F32)
        w = jnp.dot((xb * jnp.exp(g_j)).astype(BF16), bd_wide(k16), preferred_element_type=F32)
        u16 = u.astype(BF16)
        w16 = w.astype(BF16)
        a_wu = jnp.dot(attn.astype(BF16), jnp.concatenate([bd_wide(w16), bd_wide(u16)], axis=1),
                       preferred_element_type=F32)
        aq_s[c, 2 * CHUNK:3 * CHUNK, :] = (q * jnp.exp(g_w) - a_wu[:, :D_DN]).astype(BF16)
        op_s[c] = a_wu[:, D_DN:]
        gam_s[c] = jnp.broadcast_to(jnp.exp(g_last), (8, D_DN))
        k_dec = (k * jnp.exp(g_last - g_w)).astype(BF16)
        for h in range(N_HEADS):
            hs = slice(h * HEAD_DIM, (h + 1) * HEAD_DIM)
            ab_h = _mm_tn(k_dec[:, hs], jnp.concatenate([w16[:, hs], u16[:, hs]], axis=1))
            aq_s[c, 0:2 * CHUNK, hs] = ab_h[:, :HEAD_DIM].astype(BF16)
            bb_s[c, :, hs] = ab_h[:, HEAD_DIM:]

    state = state_s[...]
    zero_sq = jnp.zeros((HEAD_DIM, HEAD_DIM), BF16)
    for c in range(n_chunks):
        s16 = state.astype(BF16)
        new_parts = []
        o_parts_c = []
        for p in range(N_HEADS // 2):
            ps = slice(p * 2 * HEAD_DIM, (p + 1) * 2 * HEAD_DIM)
            s_a = s16[:, (2 * p) * HEAD_DIM:(2 * p + 1) * HEAD_DIM]
            s_b = s16[:, (2 * p + 1) * HEAD_DIM:(2 * p + 2) * HEAD_DIM]
            bd = jnp.concatenate([jnp.concatenate([s_a, zero_sq], axis=1),
                                  jnp.concatenate([zero_sq, s_b], axis=1)], axis=0)
            r = jnp.dot(aq_s[c, :, ps], bd, preferred_element_type=F32)
            new_parts.append(gam_s[c, 0:1, ps] * state[:, ps] + bb_s[c, :, ps] - r[:2 * CHUNK])
            o_parts_c.append(op_s[c, :, ps] + r[2 * CHUNK:])
        state = jnp.concatenate(new_parts, axis=1)
        o_s[c * CHUNK:(c + 1) * CHUNK, :] = jnp.concatenate(o_parts_c, axis=1)
    state_s[...] = state

    z = proj_ref[:, COL_Z:COL_Z + D_DN].astype(F32)
    zg = z * _sigmoid(z)
    o_parts = []
    for h in range(N_HEADS):
        hs = slice(h * HEAD_DIM, (h + 1) * HEAD_DIM)
        oh = o_s[:, hs]
        o_parts.append(oh * _rms_scale(oh) * dnw_ref[...] * zg[:, hs])
    o_dn = jnp.concatenate(o_parts, axis=1)

    ext_s[0:HALO, :] = (halo_ref[:, COL_SC:COL_SC + D_SC].astype(F32)
                        * halo_ref[:, COL_SH:COL_SH + D_SC].astype(F32) * halo_on)
    ext_s[HALO:HALO + tq, :] = (proj_ref[:, COL_SC:COL_SC + D_SC].astype(F32)
                                * proj_ref[:, COL_SH:COL_SH + D_SC].astype(F32))
    conv = scw_ref[SC_CONV - 1:SC_CONV, :] * ext_s[HALO:HALO + tq, :]
    for j in range(SC_CONV - 1):
        off = HALO - (SC_CONV - 1) + j
        conv = conv + scw_ref[j:j + 1, :] * ext_s[off:off + tq, :]
    y_sc = proj_ref[:, COL_SB:COL_SB + D_SC].astype(F32) * conv

    y = (jnp.dot(o_dn.astype(BF16), wout_ref[0:D_DN, :], preferred_element_type=F32)
         + jnp.dot(y_sc.astype(BF16), wout_ref[D_DN:D_MODEL, :], preferred_element_type=F32))
    out_ref[...] = x_ref[...] + mod_ref[2:3, :] * (y * _rms_scale(y) * pw_ref[...])


def _mixer(proj, ab, x2, mod3, cw, scw, gp, dnw, w_out, pw, bsz, seq):
    tq = TQ_MIX
    per_b = seq // tq
    t = x2.shape[0]

    def tile(b, s):
        return (b * per_b + s, 0)

    def halo(b, s):
        return (jnp.maximum((b * per_b + s) * (tq // HALO) - 1, 0), 0)

    const = lambda b, s: (0, 0)
    return pl.pallas_call(
        _mix_kernel,
        grid=(bsz, per_b),
        in_specs=[
            pl.BlockSpec((tq, N_MAIN), tile),
            pl.BlockSpec((HALO, N_MAIN), halo),
            pl.BlockSpec((tq, LANES), tile),
            pl.BlockSpec((tq, D_MODEL), tile),
            pl.BlockSpec((None, 6, D_MODEL), lambda b, s: (b, 0, 0)),
            pl.BlockSpec((DN_CONV, N_QKV), const),
            pl.BlockSpec((SC_CONV, D_SC), const),
            pl.BlockSpec((8, LANES), const),
            pl.BlockSpec((1, HEAD_DIM), const),
            pl.BlockSpec((D_MODEL, D_MODEL), const),
            pl.BlockSpec((1, D_MODEL), const),
        ],
        out_specs=pl.BlockSpec((tq, D_MODEL), tile),
        out_shape=jax.ShapeDtypeStruct((t, D_MODEL), F32),
        scratch_shapes=[
            pltpu.VMEM((HALO + tq, D_DN), F32),
            pltpu.VMEM((tq, D_DN), F32),
            pltpu.VMEM((tq, D_DN), F32),
            pltpu.VMEM((tq, D_DN), F32),
            pltpu.VMEM((tq, D_DN), F32),
            pltpu.VMEM((tq, LANES), F32),
            pltpu.VMEM((tq, LANES), F32),
            pltpu.VMEM((HEAD_DIM, D_DN), F32),
            pltpu.VMEM((tq // CHUNK, 3 * CHUNK, D_DN), BF16),
            pltpu.VMEM((tq // CHUNK, HEAD_DIM, D_DN), F32),
            pltpu.VMEM((tq // CHUNK, CHUNK, D_DN), F32),
            pltpu.VMEM((tq // CHUNK, 8, D_DN), F32),
            pltpu.VMEM((N_HEADS * CHUNK, N_HEADS * CHUNK), BF16),
            pltpu.VMEM((5, N_HEADS * CHUNK, N_HEADS * CHUNK), BF16),
        ],
        compiler_params=pltpu.CompilerParams(dimension_semantics=("arbitrary", "arbitrary"),
                                             vmem_limit_bytes=VMEM_LIMIT),
        name="token_mixer",
    )(proj, proj, ab, x2, mod3, cw, scw, gp, dnw, w_out, pw)


def _ffn_kernel(x_ref, mod_ref, nw_ref, w1_ref, w2_ref, pw_ref, out_ref):
    x = x_ref[...]
    h = x * _rms_scale(x) * nw_ref[...]
    h = h * (1.0 + mod_ref[4:5, :]) + mod_ref[3:4, :]
    hb = h.astype(BF16)
    y = jnp.zeros(x.shape, F32)
    for j in range(D_FF // FF_CHUNK):
        a = jnp.dot(hb, w1_ref[:, j * FF_CHUNK:(j + 1) * FF_CHUNK], preferred_element_type=F32)
        a = jnp.square(jnp.maximum(a, 0.0)).astype(BF16)
        y = y + jnp.dot(a, w2_ref[j * FF_CHUNK:(j + 1) * FF_CHUNK, :], preferred_element_type=F32)
    out_ref[...] = x + mod_ref[5:6, :] * (y * _rms_scale(y) * pw_ref[...])


def _ffn(x2, mod3, nw, w1, w2, pw, seq):
    t = x2.shape[0]
    tm = TM_FFN
    per_b = seq // tm
    return pl.pallas_call(
        _ffn_kernel,
        grid=(t // tm,),
        in_specs=[
            pl.BlockSpec((tm, D_MODEL), lambda i: (i, 0)),
            pl.BlockSpec((None, 6, D_MODEL), lambda i: (i // per_b, 0, 0)),
            pl.BlockSpec((1, D_MODEL), lambda i: (0, 0)),
            pl.BlockSpec((D_MODEL, D_FF), lambda i: (0, 0)),
            pl.BlockSpec((D_FF, D_MODEL), lambda i: (0, 0)),
            pl.BlockSpec((1, D_MODEL), lambda i: (0, 0)),
        ],
        out_specs=pl.BlockSpec((tm, D_MODEL), lambda i: (i, 0)),
        out_shape=jax.ShapeDtypeStruct((t, D_MODEL), F32),
        compiler_params=pltpu.CompilerParams(dimension_semantics=("arbitrary",),
                                             vmem_limit_bytes=VMEM_LIMIT),
        name="ffn",
    )(x2, mod3, nw, w1, w2, pw)


def _layer(x2, c, w_ada, b_ada, pre_mix_w, post_mix_w, w_in, dn_conv_w, dn_a_log, dn_dt_bias,
           dn_norm_w, sc_conv_w, w_out, pre_ffn_w, post_ffn_w, w_ff1, w_ff2, bsz, seq):
    mod3 = _ada(c, w_ada, b_ada[None, :]).reshape(bsz, 6, D_MODEL)

    n_gate = 2 * N_HEADS
    w_main = jnp.concatenate([w_in[:, :COL_SB], w_in[:, COL_SB + n_gate:]], axis=1).astype(BF16)
    w_ab = jnp.pad(w_in[:, COL_SB:COL_SB + n_gate], ((0, 0), (0, LANES - n_gate))).astype(BF16)
    proj, ab = _in_proj(x2, mod3, pre_mix_w[None, :], w_main, w_ab, seq)

    gp = jnp.zeros((8, LANES), F32)
    gp = gp.at[0, :N_HEADS].set(dn_a_log).at[1, :N_HEADS].set(dn_dt_bias)
    x2 = _mixer(proj, ab, x2, mod3, dn_conv_w, sc_conv_w, gp, dn_norm_w[None, :],
                w_out.astype(BF16), post_mix_w[None, :], bsz, seq)

    return _ffn(x2, mod3, pre_ffn_w[None, :], w_ff1.astype(BF16), w_ff2.astype(BF16),
                post_ffn_w[None, :], seq)


def kernel(x, c, w_ada, b_ada, pre_mix_norm_w, post_mix_norm_w, w_in, dn_conv_w, dn_a_log, dn_dt_bias, dn_norm_w, sc_conv_w, w_out, pre_ffn_norm_w, post_ffn_norm_w, w_ff1, w_ff2):
    bsz, seq, d = x.shape
    x2 = x.reshape(bsz * seq, d)
    for l in range(w_ada.shape[0]):
        x2 = _layer(x2, c, w_ada[l], b_ada[l], pre_mix_norm_w[l], post_mix_norm_w[l], w_in[l],
                    dn_conv_w[l], dn_a_log[l], dn_dt_bias[l], dn_norm_w[l], sc_conv_w[l], w_out[l],
                    pre_ffn_norm_w[l], post_ffn_norm_w[l], w_ff1[l], w_ff2[l], bsz, seq)
    return x2.reshape(bsz, seq, d)
```

```python
import functools

import jax
import jax.numpy as jnp
from jax import lax
from jax.experimental import pallas as pl
from jax.experimental.pallas import tpu as pltpu

F32 = jnp.float32
BF16 = jnp.bfloat16

D_MODEL = 1024
N_HEADS = 4
HEAD_DIM = 128
D_DN = N_HEADS * HEAD_DIM
D_SC = D_MODEL - D_DN
DN_CONV = 4
SC_CONV = 3
CHUNK = 64
D_FF = 4 * D_MODEL
EPS = 1e-6
N_QKV = 3 * D_DN
N_MAIN = 4 * D_DN + 3 * D_SC
COL_Z = N_QKV
COL_SB = 4 * D_DN
COL_SC = COL_SB + D_SC
COL_SH = COL_SC + D_SC
LANES = 128
HALO = 8

TM_IN = 512
TQ_MIX = 512
TM_FFN = 512
FF_CHUNK = 1024
VMEM_LIMIT = 56 * 1024 * 1024


def _sigmoid(x):
    return 1.0 / (1.0 + jnp.exp(-x))


def _mm(a, b):
    return jnp.dot(a.astype(BF16), b.astype(BF16), preferred_element_type=F32)


def _mm_nt(a, b):
    return lax.dot_general(a.astype(BF16), b.astype(BF16), (((1,), (1,)), ((), ())),
                           preferred_element_type=F32)


def _mm_tn(a, b):
    return lax.dot_general(a.astype(BF16), b.astype(BF16), (((0,), (0,)), ((), ())),
                           preferred_element_type=F32)


def _rms_scale(x):
    return lax.rsqrt(jnp.mean(x * x, axis=-1, keepdims=True) + EPS)


def _ada_kernel(c_ref, w_ref, b_ref, o_ref):
    c = c_ref[...]
    ca = (c * _sigmoid(c)).astype(BF16)
    o_ref[...] = jnp.dot(ca, w_ref[...].astype(BF16), preferred_element_type=F32) + b_ref[...]


def _ada(c, w_ada, b_ada):
    bsz = c.shape[0]
    n = w_ada.shape[1]
    bn = 1024
    return pl.pallas_call(
        _ada_kernel,
        grid=(n // bn,),
        in_specs=[
            pl.BlockSpec((bsz, D_MODEL), lambda j: (0, 0)),
            pl.BlockSpec((D_MODEL, bn), lambda j: (0, j)),
            pl.BlockSpec((1, bn), lambda j: (0, j)),
        ],
        out_specs=pl.BlockSpec((bsz, bn), lambda j: (0, j)),
        out_shape=jax.ShapeDtypeStruct((bsz, n), F32),
        compiler_params=pltpu.CompilerParams(dimension_semantics=("arbitrary",),
                                             vmem_limit_bytes=VMEM_LIMIT),
        name="ada_mod",
    )(c, w_ada, b_ada)


def _in_kernel(x_ref, mod_ref, nw_ref, wm_ref, wab_ref, proj_ref, ab_ref):
    x = x_ref[...]
    h = x * _rms_scale(x) * nw_ref[...]
    h = h * (1.0 + mod_ref[1:2, :]) + mod_ref[0:1, :]
    hb = h.astype(BF16)
    proj_ref[...] = jnp.dot(hb, wm_ref[...], preferred_element_type=F32).astype(BF16)
    ab_ref[...] = jnp.dot(hb, wab_ref[...], preferred_element_type=F32)


def _in_proj(x2, mod3, nw, w_main, w_ab, seq):
    t = x2.shape[0]
    tm = TM_IN
    per_b = seq // tm
    return pl.pallas_call(
        _in_kernel,
        grid=(t // tm,),
        in_specs=[
            pl.BlockSpec((tm, D_MODEL), lambda i: (i, 0)),
            pl.BlockSpec((None, 6, D_MODEL), lambda i: (i // per_b, 0, 0)),
            pl.BlockSpec((1, D_MODEL), lambda i: (0, 0)),
            pl.BlockSpec((D_MODEL, N_MAIN), lambda i: (0, 0)),
            pl.BlockSpec((D_MODEL, LANES), lambda i: (0, 0)),
        ],
        out_specs=[
            pl.BlockSpec((tm, N_MAIN), lambda i: (i, 0)),
            pl.BlockSpec((tm, LANES), lambda i: (i, 0)),
        ],
        out_shape=[
            jax.ShapeDtypeStruct((t, N_MAIN), BF16),
            jax.ShapeDtypeStruct((t, LANES), F32),
        ],
        compiler_params=pltpu.CompilerParams(dimension_semantics=("arbitrary",),
                                             vmem_limit_bytes=VMEM_LIMIT),
        name="in_proj",
    )(x2, mod3, nw, w_main, w_ab)


def _mix_kernel(proj_ref, halo_ref, ab_ref, x_ref, mod_ref, cw_ref, scw_ref, gp_ref, dnw_ref,
                wout_ref, pw_ref, out_ref,
                ext_s, q_s, k_s, v_s, o_s, g_s, b_s, state_s, aq_s, bb_s, op_s, gam_s, bdm_s, lvl_s):
    tq = x_ref.shape[0]
    t = pl.program_id(1)

    @pl.when(t == 0)
    def _():
        state_s[...] = jnp.zeros_like(state_s)

    halo_on = (t > 0).astype(F32)

    for part, dst in enumerate((q_s, k_s, v_s)):
        c0 = part * D_DN
        ext_s[0:HALO, :] = halo_ref[:, c0:c0 + D_DN].astype(F32) * halo_on
        ext_s[HALO:HALO + tq, :] = proj_ref[:, c0:c0 + D_DN].astype(F32)
        acc = cw_ref[DN_CONV - 1:DN_CONV, c0:c0 + D_DN] * ext_s[HALO:HALO + tq, :]
        for j in range(DN_CONV - 1):
            off = HALO - (DN_CONV - 1) + j
            acc = acc + cw_ref[j:j + 1, c0:c0 + D_DN] * ext_s[off:off + tq, :]
        y = acc * _sigmoid(acc)
        if part < 2:
            for h in range(N_HEADS):
                yh = y[:, h * HEAD_DIM:(h + 1) * HEAD_DIM]
                nrm = lax.rsqrt(jnp.sum(yh * yh, axis=-1, keepdims=True) + EPS)
                if part == 0:
                    nrm = nrm * (HEAD_DIM ** -0.5)
                dst[:, h * HEAD_DIM:(h + 1) * HEAD_DIM] = yh * nrm
        else:
            dst[...] = y

    ab = ab_ref[...]
    xg = ab + gp_ref[1:2, :]
    softplus = jnp.maximum(xg, 0.0) + jnp.log1p(jnp.exp(-jnp.abs(xg)))
    g_s[...] = -jnp.exp(gp_ref[0:1, :]) * softplus
    b_s[...] = _sigmoid(ab)

    n_chunks = tq // CHUNK
    quad = N_HEADS * CHUNK
    row_q = lax.broadcasted_iota(jnp.int32, (CHUNK, quad), 0)
    lane_q = lax.broadcasted_iota(jnp.int32, (CHUNK, quad), 1)
    col_q = lane_q & (CHUNK - 1)
    head_q = lane_q >> 6
    tri_incl = row_q >= col_q
    tri_strict = row_q > col_q
    eye_q = row_q == col_q
    eye_f = eye_q.astype(F32)
    row_l = lax.broadcasted_iota(jnp.int32, (CHUNK, LANES), 0)

    def level_mask(r, c, l):
        return (((r >> l) & 1) == 1) & (((c >> l) & 1) == 0) & ((r >> (l + 1)) == (c >> (l + 1)))

    @pl.when((pl.program_id(0) == 0) & (t == 0))
    def _():
        r2 = lax.broadcasted_iota(jnp.int32, (quad, quad), 0)
        c2 = lax.broadcasted_iota(jnp.int32, (quad, quad), 1)
        same_head = (r2 >> 6) == (c2 >> 6)
        bdm_s[...] = same_head.astype(F32).astype(BF16)
        for l in range(1, 6):
            lm = same_head & level_mask(r2 & (CHUNK - 1), c2 & (CHUNK - 1), l)
            lvl_s[l - 1] = lm.astype(F32).astype(BF16)

    lm0_f = level_mask(row_q, col_q, 0).astype(F32)
    zero_blk = jnp.zeros((CHUNK, HEAD_DIM), BF16)

    def head_bcast(cols, width):
        if width == HEAD_DIM:
            return jnp.concatenate([jnp.broadcast_to(c_, (CHUNK, HEAD_DIM)) for c_ in cols], axis=1)
        full = [jnp.broadcast_to(c_, (CHUNK, quad)) for c_ in cols]
        return jnp.where(head_q == 0, full[0],
                         jnp.where(head_q == 1, full[1], jnp.where(head_q == 2, full[2], full[3])))

    def bd_wide(a):
        rows = []
        for h in range(N_HEADS):
            rows.append(jnp.concatenate(
                [a[:, j * HEAD_DIM:(j + 1) * HEAD_DIM] if j == h else zero_blk for j in range(N_HEADS)],
                axis=1))
        return jnp.concatenate(rows, axis=0)

    def tile4(a):
        return jnp.concatenate([a] * N_HEADS, axis=0)

    chunks = range(n_chunks)
    rows = [slice(c * CHUNK, (c + 1) * CHUNK) for c in chunks]
    x_inv, m_t, attn16, b_j, eg_j, g_w = [], [], [], [], [], []
    for c in chunks:
        gcum = g_s[rows[c], :]
        for s in (1, 2, 4, 8, 16, 32):
            gcum = gcum + jnp.where(row_l >= s, pltpu.roll(gcum, s, axis=0), 0.0)
        bt = b_s[rows[c], :]
        g_cols = [gcum[:, h:h + 1] for h in range(N_HEADS)]
        b_cols = [bt[:, N_HEADS + h:N_HEADS + h + 1] for h in range(N_HEADS)]
        g_i = head_bcast(g_cols, CHUNK)
        b_i = head_bcast(b_cols, CHUNK)
        g_w.append(head_bcast(g_cols, HEAD_DIM))
        g_j = jnp.sum(jnp.where(eye_q, g_i, 0.0), axis=0, keepdims=True)
        b_j.append(jnp.sum(jnp.where(eye_q, b_i, 0.0), axis=0, keepdims=True))
        eg_j.append(jnp.exp(g_j))
        decay = jnp.exp(jnp.where(tri_incl, g_i - g_j, -1e30))

        k16 = k_s[rows[c], :].astype(BF16)
        kq = jnp.concatenate([k16, q_s[rows[c], :].astype(BF16)], axis=0)
        kq_k = lax.dot_general(kq, bd_wide(k16), (((1,), (1,)), ((), ())),
                               preferred_element_type=F32)
        m_st = jnp.where(tri_strict, kq_k[:CHUNK] * b_i * decay, 0.0)
        attn16.append((kq_k[CHUNK:] * decay).astype(BF16))
        x_inv.append(eye_f - m_st * lm0_f)
        m_t.append(tile4(m_st.astype(BF16)))

    for l in range(1, 6):
        x16 = [x_inv[c].astype(BF16) for c in chunks]
        y = [jnp.dot(x16[c], m_t[c] * lvl_s[l - 1], preferred_element_type=F32) for c in chunks]
        z = [jnp.dot(y[c].astype(BF16), tile4(x16[c]) * bdm_s[...], preferred_element_type=F32)
             for c in chunks]
        x_inv = [x_inv[c] - z[c] for c in chunks]

    for c in chunks:
        q = q_s[rows[c], :]
        k = k_s[rows[c], :]
        k16 = k.astype(BF16)
        xb = x_inv[c] * b_j[c]
        u = jnp.dot(xb.astype(BF16), bd_wide(v_s[rows[c], :].astype(BF16)), preferred_element_type=F32)
        w = jnp.dot((xb * eg_j[c]).astype(BF16), bd_wide(k16), preferred_element_type=F32)
        u16 = u.astype(BF16)
        w16 = w.astype(BF16)
        a_wu = jnp.dot(attn16[c], jnp.concatenate([bd_wide(w16), bd_wide(u16)], axis=1),
                       preferred_element_type=F32)
        g_last = g_w[c][CHUNK - 1:CHUNK, :]
        aq_s[c, 2 * CHUNK:3 * CHUNK, :] = (q * jnp.exp(g_w[c]) - a_wu[:, :D_DN]).astype(BF16)
        op_s[c] = a_wu[:, D_DN:]
        gam_s[c] = jnp.broadcast_to(jnp.exp(g_last), (8, D_DN))
        k_dec = (k * jnp.exp(g_last - g_w[c])).astype(BF16)
        for h in range(N_HEADS):
            hs = slice(h * HEAD_DIM, (h + 1) * HEAD_DIM)
            ab_h = _mm_tn(k_dec[:, hs], jnp.concatenate([w16[:, hs], u16[:, hs]], axis=1))
            aq_s[c, 0:2 * CHUNK, hs] = ab_h[:, :HEAD_DIM].astype(BF16)
            bb_s[c, :, hs] = ab_h[:, HEAD_DIM:]

    state = state_s[...]
    zero_sq = jnp.zeros((HEAD_DIM, HEAD_DIM), BF16)
    for c in range(n_chunks):
        s16 = state.astype(BF16)
        new_parts = []
        o_parts_c = []
        for p in range(N_HEADS // 2):
            ps = slice(p * 2 * HEAD_DIM, (p + 1) * 2 * HEAD_DIM)
            s_a = s16[:, (2 * p) * HEAD_DIM:(2 * p + 1) * HEAD_DIM]
            s_b = s16[:, (2 * p + 1) * HEAD_DIM:(2 * p + 2) * HEAD_DIM]
            bd = jnp.concatenate([jnp.concatenate([s_a, zero_sq], axis=1),
                                  jnp.concatenate([zero_sq, s_b], axis=1)], axis=0)
            r = jnp.dot(aq_s[c, :, ps], bd, preferred_element_type=F32)
            new_parts.append(gam_s[c, 0:1, ps] * state[:, ps] + bb_s[c, :, ps] - r[:2 * CHUNK])
            o_parts_c.append(op_s[c, :, ps] + r[2 * CHUNK:])
        state = jnp.concatenate(new_parts, axis=1)
        o_s[c * CHUNK:(c + 1) * CHUNK, :] = jnp.concatenate(o_parts_c, axis=1)
    state_s[...] = state

    z = proj_ref[:, COL_Z:COL_Z + D_DN].astype(F32)
    zg = z * _sigmoid(z)
    o_parts = []
    for h in range(N_HEADS):
        hs = slice(h * HEAD_DIM, (h + 1) * HEAD_DIM)
        oh = o_s[:, hs]
        o_parts.append(oh * _rms_scale(oh) * dnw_ref[...] * zg[:, hs])
    o_dn = jnp.concatenate(o_parts, axis=1)

    ext_s[0:HALO, :] = (halo_ref[:, COL_SC:COL_SC + D_SC].astype(F32)
                        * halo_ref[:, COL_SH:COL_SH + D_SC].astype(F32) * halo_on)
    ext_s[HALO:HALO + tq, :] = (proj_ref[:, COL_SC:COL_SC + D_SC].astype(F32)
                                * proj_ref[:, COL_SH:COL_SH + D_SC].astype(F32))
    conv = scw_ref[SC_CONV - 1:SC_CONV, :] * ext_s[HALO:HALO + tq, :]
    for j in range(SC_CONV - 1):
        off = HALO - (SC_CONV - 1) + j
        conv = conv + scw_ref[j:j + 1, :] * ext_s[off:off + tq, :]
    y_sc = proj_ref[:, COL_SB:COL_SB + D_SC].astype(F32) * conv

    y = (jnp.dot(o_dn.astype(BF16), wout_ref[0:D_DN, :], preferred_element_type=F32)
         + jnp.dot(y_sc.astype(BF16), wout_ref[D_DN:D_MODEL, :], preferred_element_type=F32))
    out_ref[...] = x_ref[...] + mod_ref[2:3, :] * (y * _rms_scale(y) * pw_ref[...])


def _mixer(proj, ab, x2, mod3, cw, scw, gp, dnw, w_out, pw, bsz, seq):
    tq = TQ_MIX
    per_b = seq // tq
    t = x2.shape[0]

    def tile(b, s):
        return (b * per_b + s, 0)

    def halo(b, s):
        return (jnp.maximum((b * per_b + s) * (tq // HALO) - 1, 0), 0)

    const = lambda b, s: (0, 0)
    return pl.pallas_call(
        _mix_kernel,
        grid=(bsz, per_b),
        in_specs=[
            pl.BlockSpec((tq, N_MAIN), tile),
            pl.BlockSpec((HALO, N_MAIN), halo),
            pl.BlockSpec((tq, LANES), tile),
            pl.BlockSpec((tq, D_MODEL), tile),
            pl.BlockSpec((None, 6, D_MODEL), lambda b, s: (b, 0, 0)),
            pl.BlockSpec((DN_CONV, N_QKV), const),
            pl.BlockSpec((SC_CONV, D_SC), const),
            pl.BlockSpec((8, LANES), const),
            pl.BlockSpec((1, HEAD_DIM), const),
            pl.BlockSpec((D_MODEL, D_MODEL), const),
            pl.BlockSpec((1, D_MODEL), const),
        ],
        out_specs=pl.BlockSpec((tq, D_MODEL), tile),
        out_shape=jax.ShapeDtypeStruct((t, D_MODEL), F32),
        scratch_shapes=[
            pltpu.VMEM((HALO + tq, D_DN), F32),
            pltpu.VMEM((tq, D_DN), F32),
            pltpu.VMEM((tq, D_DN), F32),
            pltpu.VMEM((tq, D_DN), F32),
            pltpu.VMEM((tq, D_DN), F32),
            pltpu.VMEM((tq, LANES), F32),
            pltpu.VMEM((tq, LANES), F32),
            pltpu.VMEM((HEAD_DIM, D_DN), F32),
            pltpu.VMEM((tq // CHUNK, 3 * CHUNK, D_DN), BF16),
            pltpu.VMEM((tq // CHUNK, HEAD_DIM, D_DN), F32),
            pltpu.VMEM((tq // CHUNK, CHUNK, D_DN), F32),
            pltpu.VMEM((tq // CHUNK, 8, D_DN), F32),
            pltpu.VMEM((N_HEADS * CHUNK, N_HEADS * CHUNK), BF16),
            pltpu.VMEM((5, N_HEADS * CHUNK, N_HEADS * CHUNK), BF16),
        ],
        compiler_params=pltpu.CompilerParams(dimension_semantics=("arbitrary", "arbitrary"),
                                             vmem_limit_bytes=VMEM_LIMIT),
        name="token_mixer",
    )(proj, proj, ab, x2, mod3, cw, scw, gp, dnw, w_out, pw)


def _ffn_kernel(x_ref, mod_ref, nw_ref, w1_ref, w2_ref, pw_ref, out_ref):
    x = x_ref[...]
    h = x * _rms_scale(x) * nw_ref[...]
    h = h * (1.0 + mod_ref[4:5, :]) + mod_ref[3:4, :]
    hb = h.astype(BF16)
    y = jnp.zeros(x.shape, F32)
    for j in range(D_FF // FF_CHUNK):
        a = jnp.dot(hb, w1_ref[:, j * FF_CHUNK:(j + 1) * FF_CHUNK], preferred_element_type=F32)
        a = jnp.square(jnp.maximum(a, 0.0)).astype(BF16)
        y = y + jnp.dot(a, w2_ref[j * FF_CHUNK:(j + 1) * FF_CHUNK, :], preferred_element_type=F32)
    out_ref[...] = x + mod_ref[5:6, :] * (y * _rms_scale(y) * pw_ref[...])


def _ffn(x2, mod3, nw, w1, w2, pw, seq):
    t = x2.shape[0]
    tm = TM_FFN
    per_b = seq // tm
    return pl.pallas_call(
        _ffn_kernel,
        grid=(t // tm,),
        in_specs=[
            pl.BlockSpec((tm, D_MODEL), lambda i: (i, 0)),
            pl.BlockSpec((None, 6, D_MODEL), lambda i: (i // per_b, 0, 0)),
            pl.BlockSpec((1, D_MODEL), lambda i: (0, 0)),
            pl.BlockSpec((D_MODEL, D_FF), lambda i: (0, 0)),
            pl.BlockSpec((D_FF, D_MODEL), lambda i: (0, 0)),
            pl.BlockSpec((1, D_MODEL), lambda i: (0, 0)),
        ],
        out_specs=pl.BlockSpec((tm, D_MODEL), lambda i: (i, 0)),
        out_shape=jax.ShapeDtypeStruct((t, D_MODEL), F32),
        compiler_params=pltpu.CompilerParams(dimension_semantics=("arbitrary",),
                                             vmem_limit_bytes=VMEM_LIMIT),
        name="ffn",
    )(x2, mod3, nw, w1, w2, pw)


def _layer(x2, c, w_ada, b_ada, pre_mix_w, post_mix_w, w_in, dn_conv_w, dn_a_log, dn_dt_bias,
           dn_norm_w, sc_conv_w, w_out, pre_ffn_w, post_ffn_w, w_ff1, w_ff2, bsz, seq):
    mod3 = _ada(c, w_ada, b_ada[None, :]).reshape(bsz, 6, D_MODEL)

    n_gate = 2 * N_HEADS
    w_main = jnp.concatenate([w_in[:, :COL_SB], w_in[:, COL_SB + n_gate:]], axis=1).astype(BF16)
    w_ab = jnp.pad(w_in[:, COL_SB:COL_SB + n_gate], ((0, 0), (0, LANES - n_gate))).astype(BF16)
    proj, ab = _in_proj(x2, mod3, pre_mix_w[None, :], w_main, w_ab, seq)

    gp = jnp.zeros((8, LANES), F32)
    gp = gp.at[0, :N_HEADS].set(dn_a_log).at[1, :N_HEADS].set(dn_dt_bias)
    x2 = _mixer(proj, ab, x2, mod3, dn_conv_w, sc_conv_w, gp, dn_norm_w[None, :],
                w_out.astype(BF16), post_mix_w[None, :], bsz, seq)

    return _ffn(x2, mod3, pre_ffn_w[None, :], w_ff1.astype(BF16), w_ff2.astype(BF16),
                post_ffn_w[None, :], seq)


def kernel(x, c, w_ada, b_ada, pre_mix_norm_w, post_mix_norm_w, w_in, dn_conv_w, dn_a_log, dn_dt_bias, dn_norm_w, sc_conv_w, w_out, pre_ffn_norm_w, post_ffn_norm_w, w_ff1, w_ff2):
    bsz, seq, d = x.shape
    x2 = x.reshape(bsz * seq, d)
    for l in range(w_ada.shape[0]):
        x2 = _layer(x2, c, w_ada[l], b_ada[l], pre_mix_norm_w[l], post_mix_norm_w[l], w_in[l],
                    dn_conv_w[l], dn_a_log[l], dn_dt_bias[l], dn_norm_w[l], sc_conv_w[l], w_out[l],
                    pre_ffn_norm_w[l], post_ffn_norm_w[l], w_ff1[l], w_ff2[l], bsz, seq)
    return x2.reshape(bsz, seq, d)
```

```python
import functools

import jax
import jax.numpy as jnp
from jax import lax
from jax.experimental import pallas as pl
from jax.experimental.pallas import tpu as pltpu

F32 = jnp.float32
BF16 = jnp.bfloat16

D_MODEL = 1024
N_HEADS = 4
HEAD_DIM = 128
D_DN = N_HEADS * HEAD_DIM
D_SC = D_MODEL - D_DN
DN_CONV = 4
SC_CONV = 3
CHUNK = 64
D_FF = 4 * D_MODEL
EPS = 1e-6
N_QKV = 3 * D_DN
N_MAIN = 4 * D_DN + 3 * D_SC
COL_Z = N_QKV
COL_SB = 4 * D_DN
COL_SC = COL_SB + D_SC
COL_SH = COL_SC + D_SC
N_P2 = N_QKV + D_DN + D_SC
COL_ZG = N_QKV
COL_YSC = N_QKV + D_DN
LANES = 128
HALO = 8

TM_IN = 512
TQ_MIX = 512
TM_FFN = 512
FF_CHUNK = 1024
VMEM_LIMIT = 56 * 1024 * 1024


def _sigmoid(x):
    return 0.5 + 0.5 * jnp.tanh(0.5 * x)


def _mm(a, b):
    return jnp.dot(a.astype(BF16), b.astype(BF16), preferred_element_type=F32)


def _mm_nt(a, b):
    return lax.dot_general(a.astype(BF16), b.astype(BF16), (((1,), (1,)), ((), ())),
                           preferred_element_type=F32)


def _mm_tn(a, b):
    return lax.dot_general(a.astype(BF16), b.astype(BF16), (((0,), (0,)), ((), ())),
                           preferred_element_type=F32)


def _rms_scale(x):
    return lax.rsqrt(jnp.mean(x * x, axis=-1, keepdims=True) + EPS)


def _ada_kernel(c_ref, w_ref, b_ref, o_ref):
    c = c_ref[...]
    ca = (c * _sigmoid(c)).astype(BF16)
    o_ref[...] = jnp.dot(ca, w_ref[...].astype(BF16), preferred_element_type=F32) + b_ref[...]


def _ada(c, w_ada, b_ada):
    bsz = c.shape[0]
    n = w_ada.shape[1]
    bn = 1024
    return pl.pallas_call(
        _ada_kernel,
        grid=(n // bn,),
        in_specs=[
            pl.BlockSpec((bsz, D_MODEL), lambda j: (0, 0)),
            pl.BlockSpec((D_MODEL, bn), lambda j: (0, j)),
            pl.BlockSpec((1, bn), lambda j: (0, j)),
        ],
        out_specs=pl.BlockSpec((bsz, bn), lambda j: (0, j)),
        out_shape=jax.ShapeDtypeStruct((bsz, n), F32),
        compiler_params=pltpu.CompilerParams(dimension_semantics=("arbitrary",),
                                             vmem_limit_bytes=VMEM_LIMIT),
        name="ada_mod",
    )(c, w_ada, b_ada)


def _causal_conv(ext, w_ref, c0, width):
    r = ext.shape[0] - HALO
    n = ext.shape[1]
    acc = w_ref[width - 1:width, c0:c0 + n] * ext[HALO:HALO + r, :]
    for s in range(1, width):
        acc = acc + w_ref[width - 1 - s:width - s, c0:c0 + n] * pltpu.roll(ext, s, axis=0)[HALO:HALO + r, :]
    return acc


def _in_kernel(x_ref, mod_ref, nw_ref, wm_ref, wab_ref, cw_ref, scw_ref, gp_ref,
               proj_ref, gb_ref, e_s, p_s, *, per_b):
    tm = x_ref.shape[0]

    @pl.when(pl.program_id(0) % per_b == 0)
    def _():
        e_s[:, 0:HALO, :] = jnp.zeros((e_s.shape[0], HALO, D_DN), F32)

    x = x_ref[...]
    h = x * _rms_scale(x) * nw_ref[...]
    h = h * (1.0 + mod_ref[1:2, :]) + mod_ref[0:1, :]
    hb = h.astype(BF16)

    def proj(c0, n=D_DN):
        return jnp.dot(hb, wm_ref[:, c0:c0 + n], preferred_element_type=F32)

    half = tm // 2
    for part in range(3):
        e_s[part, HALO:HALO + tm, :] = proj(part * D_DN)
    e_s[3, HALO:HALO + tm, :] = proj(COL_SC)
    p_s[0] = proj(COL_Z)
    p_s[1] = proj(COL_SH)
    p_s[2] = proj(COL_SB)
    ab = jnp.dot(hb, wab_ref[...], preferred_element_type=F32)

    for part in range(3):
        c0 = part * D_DN
        for hf in range(2):
            r0 = hf * half
            acc = _causal_conv(e_s[part, r0:r0 + half + HALO, :], cw_ref, c0, DN_CONV)
            y = acc * _sigmoid(acc)
            if part < 2:
                parts = []
                for hd in range(N_HEADS):
                    yh = y[:, hd * HEAD_DIM:(hd + 1) * HEAD_DIM]
                    nrm = lax.rsqrt(jnp.sum(yh * yh, axis=-1, keepdims=True) + EPS)
                    if part == 0:
                        nrm = nrm * (HEAD_DIM ** -0.5)
                    parts.append(yh * nrm)
                y = jnp.concatenate(parts, axis=1)
            proj_ref[r0:r0 + half, c0:c0 + D_DN] = y.astype(BF16)

    p_z = p_s[0]
    proj_ref[:, COL_ZG:COL_ZG + D_DN] = (p_z * _sigmoid(p_z)).astype(BF16)
    e_s[3, HALO:HALO + tm, :] = e_s[3, HALO:HALO + tm, :] * p_s[1]
    conv = _causal_conv(e_s[3], scw_ref, 0, SC_CONV)
    proj_ref[:, COL_YSC:COL_YSC + D_SC] = (p_s[2] * conv).astype(BF16)
    e_s[:, 0:HALO, :] = e_s[:, tm:tm + HALO, :]

    xg = ab + gp_ref[1:2, :]
    softplus = jnp.maximum(xg, 0.0) + jnp.log1p(jnp.exp(-jnp.abs(xg)))
    lane = lax.broadcasted_iota(jnp.int32, ab.shape, 1)
    gb_ref[...] = jnp.where(lane < N_HEADS, -jnp.exp(gp_ref[0:1, :]) * softplus, _sigmoid(ab))


def _in_proj(x2, mod3, nw, w_main, w_ab, cw, scw, gp, seq):
    t = x2.shape[0]
    tm = TM_IN
    per_b = seq // tm
    const = lambda i: (0, 0)
    return pl.pallas_call(
        functools.partial(_in_kernel, per_b=per_b),
        grid=(t // tm,),
        in_specs=[
            pl.BlockSpec((tm, D_MODEL), lambda i: (i, 0)),
            pl.BlockSpec((None, 6, D_MODEL), lambda i: (i // per_b, 0, 0)),
            pl.BlockSpec((1, D_MODEL), const),
            pl.BlockSpec((D_MODEL, N_MAIN), const),
            pl.BlockSpec((D_MODEL, LANES), const),
            pl.BlockSpec((DN_CONV, N_QKV), const),
            pl.BlockSpec((SC_CONV, D_SC), const),
            pl.BlockSpec((8, LANES), const),
        ],
        out_specs=[
            pl.BlockSpec((tm, N_P2), lambda i: (i, 0)),
            pl.BlockSpec((tm, LANES), lambda i: (i, 0)),
        ],
        out_shape=[
            jax.ShapeDtypeStruct((t, N_P2), BF16),
            jax.ShapeDtypeStruct((t, LANES), F32),
        ],
        scratch_shapes=[
            pltpu.VMEM((4, HALO + tm, D_DN), F32),
            pltpu.VMEM((3, tm, D_DN), F32),
        ],
        compiler_params=pltpu.CompilerParams(dimension_semantics=("arbitrary",),
                                             vmem_limit_bytes=VMEM_LIMIT),
        name="in_proj",
    )(x2, mod3, nw, w_main, w_ab, cw, scw, gp)


def _mix_kernel(proj_ref, gb_ref, x_ref, mod_ref, dnw_ref, wout_ref, pw_ref, out_ref,
                o_s, state_s, aq_s, bb_s, op_s, gam_s, bdm_s, lvl_s):
    tq = x_ref.shape[0]
    t = pl.program_id(1)

    @pl.when(t == 0)
    def _():
        state_s[...] = jnp.zeros_like(state_s)

    q_ref = proj_ref.at[:, 0:D_DN]
    k_ref = proj_ref.at[:, D_DN:2 * D_DN]
    v_ref = proj_ref.at[:, 2 * D_DN:3 * D_DN]

    n_chunks = tq // CHUNK
    quad = N_HEADS * CHUNK
    row_q = lax.broadcasted_iota(jnp.int32, (CHUNK, quad), 0)
    lane_q = lax.broadcasted_iota(jnp.int32, (CHUNK, quad), 1)
    col_q = lane_q & (CHUNK - 1)
    head_q = lane_q >> 6
    tri_incl = row_q >= col_q
    tri_strict = row_q > col_q
    eye_q = row_q == col_q
    eye_f = eye_q.astype(F32)
    row_l = lax.broadcasted_iota(jnp.int32, (CHUNK, LANES), 0)

    def level_mask(r, c, l):
        return (((r >> l) & 1) == 1) & (((c >> l) & 1) == 0) & ((r >> (l + 1)) == (c >> (l + 1)))

    @pl.when((pl.program_id(0) == 0) & (t == 0))
    def _():
        r2 = lax.broadcasted_iota(jnp.int32, (quad, quad), 0)
        c2 = lax.broadcasted_iota(jnp.int32, (quad, quad), 1)
        same_head = (r2 >> 6) == (c2 >> 6)
        bdm_s[...] = same_head.astype(F32).astype(BF16)
        for l in range(1, 6):
            lm = same_head & level_mask(r2 & (CHUNK - 1), c2 & (CHUNK - 1), l)
            lvl_s[l - 1] = lm.astype(F32).astype(BF16)

    lm0_f = level_mask(row_q, col_q, 0).astype(F32)
    zero_blk = jnp.zeros((CHUNK, HEAD_DIM), BF16)

    def head_bcast(cols, width):
        if width == HEAD_DIM:
            return jnp.concatenate([jnp.broadcast_to(c_, (CHUNK, HEAD_DIM)) for c_ in cols], axis=1)
        full = [jnp.broadcast_to(c_, (CHUNK, quad)) for c_ in cols]
        return jnp.where(head_q == 0, full[0],
                         jnp.where(head_q == 1, full[1], jnp.where(head_q == 2, full[2], full[3])))

    def bd_wide(a):
        rows = []
        for h in range(N_HEADS):
            rows.append(jnp.concatenate(
                [a[:, j * HEAD_DIM:(j + 1) * HEAD_DIM] if j == h else zero_blk for j in range(N_HEADS)],
                axis=1))
        return jnp.concatenate(rows, axis=0)

    def tile4(a):
        return jnp.concatenate([a] * N_HEADS, axis=0)

    chunks = range(n_chunks)
    rows = [slice(c * CHUNK, (c + 1) * CHUNK) for c in chunks]
    x_inv, m_t, attn16, b_j, eg_j, g_w = [], [], [], [], [], []
    for c in chunks:
        bt = gb_ref[rows[c], :]
        gcum = bt
        for s in (1, 2, 4, 8, 16, 32):
            gcum = gcum + jnp.where(row_l >= s, pltpu.roll(gcum, s, axis=0), 0.0)
        g_cols = [gcum[:, h:h + 1] for h in range(N_HEADS)]
        b_cols = [bt[:, N_HEADS + h:N_HEADS + h + 1] for h in range(N_HEADS)]
        g_i = head_bcast(g_cols, CHUNK)
        b_i = head_bcast(b_cols, CHUNK)
        g_w.append(head_bcast(g_cols, HEAD_DIM))
        g_j = jnp.sum(jnp.where(eye_q, g_i, 0.0), axis=0, keepdims=True)
        b_j.append(jnp.sum(jnp.where(eye_q, b_i, 0.0), axis=0, keepdims=True))
        eg_j.append(jnp.exp(g_j))
        decay = jnp.exp(jnp.where(tri_incl, g_i - g_j, -1e30))

        k16 = k_ref[rows[c], :]
        kq = jnp.concatenate([k16, q_ref[rows[c], :]], axis=0)
        kq_k = lax.dot_general(kq, bd_wide(k16), (((1,), (1,)), ((), ())),
                               preferred_element_type=F32)
        m_st = jnp.where(tri_strict, kq_k[:CHUNK] * b_i * decay, 0.0)
        attn16.append((kq_k[CHUNK:] * decay).astype(BF16))
        x_inv.append(eye_f - m_st * lm0_f)
        m_t.append(tile4(m_st.astype(BF16)))

    for l in range(1, 6):
        x16 = [x_inv[c].astype(BF16) for c in chunks]
        y = [jnp.dot(x16[c], m_t[c] * lvl_s[l - 1], preferred_element_type=F32) for c in chunks]
        z = [jnp.dot(y[c].astype(BF16), tile4(x16[c]) * bdm_s[...], preferred_element_type=F32)
             for c in chunks]
        x_inv = [x_inv[c] - z[c] for c in chunks]

    for c in chunks:
        k16 = k_ref[rows[c], :]
        q = q_ref[rows[c], :].astype(F32)
        k = k16.astype(F32)
        xb = x_inv[c] * b_j[c]
        u = jnp.dot(xb.astype(BF16), bd_wide(v_ref[rows[c], :]), preferred_element_type=F32)
        w = jnp.dot((xb * eg_j[c]).astype(BF16), bd_wide(k16), preferred_element_type=F32)
        u16 = u.astype(BF16)
        w16 = w.astype(BF16)
        a_wu = jnp.dot(attn16[c], jnp.concatenate([bd_wide(w16), bd_wide(u16)], axis=1),
                       preferred_element_type=F32)
        g_last = g_w[c][CHUNK - 1:CHUNK, :]
        aq_s[c, 2 * CHUNK:3 * CHUNK, :] = (q * jnp.exp(g_w[c]) - a_wu[:, :D_DN]).astype(BF16)
        op_s[c] = a_wu[:, D_DN:]
        gam_s[c] = jnp.broadcast_to(jnp.exp(g_last), (8, D_DN))
        k_dec = (k * jnp.exp(g_last - g_w[c])).astype(BF16)
        for h in range(N_HEADS):
            hs = slice(h * HEAD_DIM, (h + 1) * HEAD_DIM)
            ab_h = _mm_tn(k_dec[:, hs], jnp.concatenate([w16[:, hs], u16[:, hs]], axis=1))
            aq_s[c, 0:2 * CHUNK, hs] = ab_h[:, :HEAD_DIM].astype(BF16)
            bb_s[c, :, hs] = ab_h[:, HEAD_DIM:]

    state = state_s[...]
    zero_sq = jnp.zeros((HEAD_DIM, HEAD_DIM), BF16)
    for c in range(n_chunks):
        s16 = state.astype(BF16)
        new_parts = []
        o_parts_c = []
        for p in range(N_HEADS // 2):
            ps = slice(p * 2 * HEAD_DIM, (p + 1) * 2 * HEAD_DIM)
            s_a = s16[:, (2 * p) * HEAD_DIM:(2 * p + 1) * HEAD_DIM]
            s_b = s16[:, (2 * p + 1) * HEAD_DIM:(2 * p + 2) * HEAD_DIM]
            bd = jnp.concatenate([jnp.concatenate([s_a, zero_sq], axis=1),
                                  jnp.concatenate([zero_sq, s_b], axis=1)], axis=0)
            r = jnp.dot(aq_s[c, :, ps], bd, preferred_element_type=F32)
            new_parts.append(gam_s[c, 0:1, ps] * state[:, ps] + bb_s[c, :, ps] - r[:2 * CHUNK])
            o_parts_c.append(op_s[c, :, ps] + r[2 * CHUNK:])
        state = jnp.concatenate(new_parts, axis=1)
        o_s[c * CHUNK:(c + 1) * CHUNK, :] = jnp.concatenate(o_parts_c, axis=1)
    state_s[...] = state

    zg = proj_ref[:, COL_ZG:COL_ZG + D_DN].astype(F32)
    o_parts = []
    for h in range(N_HEADS):
        hs = slice(h * HEAD_DIM, (h + 1) * HEAD_DIM)
        oh = o_s[:, hs]
        o_parts.append(oh * _rms_scale(oh) * dnw_ref[...] * zg[:, hs])
    o_dn = jnp.concatenate(o_parts, axis=1)

    y = (jnp.dot(o_dn.astype(BF16), wout_ref[0:D_DN, :], preferred_element_type=F32)
         + jnp.dot(proj_ref[:, COL_YSC:COL_YSC + D_SC], wout_ref[D_DN:D_MODEL, :],
                   preferred_element_type=F32))
    out_ref[...] = x_ref[...] + mod_ref[2:3, :] * (y * _rms_scale(y) * pw_ref[...])


def _mixer(proj, gb, x2, mod3, dnw, w_out, pw, bsz, seq):
    tq = TQ_MIX
    per_b = seq // tq
    t = x2.shape[0]

    def tile(b, s):
        return (b * per_b + s, 0)

    const = lambda b, s: (0, 0)
    return pl.pallas_call(
        _mix_kernel,
        grid=(bsz, per_b),
        in_specs=[
            pl.BlockSpec((tq, N_P2), tile),
            pl.BlockSpec((tq, LANES), tile),
            pl.BlockSpec((tq, D_MODEL), tile),
            pl.BlockSpec((None, 6, D_MODEL), lambda b, s: (b, 0, 0)),
            pl.BlockSpec((1, HEAD_DIM), const),
            pl.BlockSpec((D_MODEL, D_MODEL), const),
            pl.BlockSpec((1, D_MODEL), const),
        ],
        out_specs=pl.BlockSpec((tq, D_MODEL), tile),
        out_shape=jax.ShapeDtypeStruct((t, D_MODEL), F32),
        scratch_shapes=[
            pltpu.VMEM((tq, D_DN), F32),
            pltpu.VMEM((HEAD_DIM, D_DN), F32),
            pltpu.VMEM((tq // CHUNK, 3 * CHUNK, D_DN), BF16),
            pltpu.VMEM((tq // CHUNK, HEAD_DIM, D_DN), F32),
            pltpu.VMEM((tq // CHUNK, CHUNK, D_DN), F32),
            pltpu.VMEM((tq // CHUNK, 8, D_DN), F32),
            pltpu.VMEM((N_HEADS * CHUNK, N_HEADS * CHUNK), BF16),
            pltpu.VMEM((5, N_HEADS * CHUNK, N_HEADS * CHUNK), BF16),
        ],
        compiler_params=pltpu.CompilerParams(dimension_semantics=("arbitrary", "arbitrary"),
                                             vmem_limit_bytes=VMEM_LIMIT),
        name="token_mixer",
    )(proj, gb, x2, mod3, dnw, w_out, pw)


def _ffn_kernel(x_ref, mod_ref, nw_ref, w1_ref, w2_ref, pw_ref, out_ref):
    x = x_ref[...]
    h = x * _rms_scale(x) * nw_ref[...]
    h = h * (1.0 + mod_ref[4:5, :]) + mod_ref[3:4, :]
    hb = h.astype(BF16)
    y = jnp.zeros(x.shape, F32)
    for j in range(D_FF // FF_CHUNK):
        a = jnp.dot(hb, w1_ref[:, j * FF_CHUNK:(j + 1) * FF_CHUNK], preferred_element_type=F32)
        a = jnp.square(jnp.maximum(a, 0.0)).astype(BF16)
        y = y + jnp.dot(a, w2_ref[j * FF_CHUNK:(j + 1) * FF_CHUNK, :], preferred_element_type=F32)
    out_ref[...] = x + mod_ref[5:6, :] * (y * _rms_scale(y) * pw_ref[...])


def _ffn(x2, mod3, nw, w1, w2, pw, seq):
    t = x2.shape[0]
    tm = TM_FFN
    per_b = seq // tm
    return pl.pallas_call(
        _ffn_kernel,
        grid=(t // tm,),
        in_specs=[
            pl.BlockSpec((tm, D_MODEL), lambda i: (i, 0)),
            pl.BlockSpec((None, 6, D_MODEL), lambda i: (i // per_b, 0, 0)),
            pl.BlockSpec((1, D_MODEL), lambda i: (0, 0)),
            pl.BlockSpec((D_MODEL, D_FF), lambda i: (0, 0)),
            pl.BlockSpec((D_FF, D_MODEL), lambda i: (0, 0)),
            pl.BlockSpec((1, D_MODEL), lambda i: (0, 0)),
        ],
        out_specs=pl.BlockSpec((tm, D_MODEL), lambda i: (i, 0)),
        out_shape=jax.ShapeDtypeStruct((t, D_MODEL), F32),
        compiler_params=pltpu.CompilerParams(dimension_semantics=("arbitrary",),
                                             vmem_limit_bytes=VMEM_LIMIT),
        name="ffn",
    )(x2, mod3, nw, w1, w2, pw)


def _layer(x2, c, w_ada, b_ada, pre_mix_w, post_mix_w, w_in, dn_conv_w, dn_a_log, dn_dt_bias,
           dn_norm_w, sc_conv_w, w_out, pre_ffn_w, post_ffn_w, w_ff1, w_ff2, bsz, seq):
    mod3 = _ada(c, w_ada, b_ada[None, :]).reshape(bsz, 6, D_MODEL)

    n_gate = 2 * N_HEADS
    w_main = jnp.concatenate([w_in[:, :COL_SB], w_in[:, COL_SB + n_gate:]], axis=1).astype(BF16)
    w_ab = jnp.pad(w_in[:, COL_SB:COL_SB + n_gate], ((0, 0), (0, LANES - n_gate))).astype(BF16)
    gp = jnp.zeros((8, LANES), F32)
    gp = gp.at[0, :N_HEADS].set(dn_a_log).at[1, :N_HEADS].set(dn_dt_bias)
    proj, gb = _in_proj(x2, mod3, pre_mix_w[None, :], w_main, w_ab, dn_conv_w, sc_conv_w, gp, seq)

    x2 = _mixer(proj, gb, x2, mod3, dn_norm_w[None, :], w_out.astype(BF16), post_mix_w[None, :],
                bsz, seq)

    return _ffn(x2, mod3, pre_ffn_w[None, :], w_ff1.astype(BF16), w_ff2.astype(BF16),
                post_ffn_w[None, :], seq)


def kernel(x, c, w_ada, b_ada, pre_mix_norm_w, post_mix_norm_w, w_in, dn_conv_w, dn_a_log, dn_dt_bias, dn_norm_w, sc_conv_w, w_out, pre_ffn_norm_w, post_ffn_norm_w, w_ff1, w_ff2):
    bsz, seq, d = x.shape
    x2 = x.reshape(bsz * seq, d)
    for l in range(w_ada.shape[0]):
        x2 = _layer(x2, c, w_ada[l], b_ada[l], pre_mix_norm_w[l], post_mix_norm_w[l], w_in[l],
                    dn_conv_w[l], dn_a_log[l], dn_dt_bias[l], dn_norm_w[l], sc_conv_w[l], w_out[l],
                    pre_ffn_norm_w[l], post_ffn_norm_w[l], w_ff1[l], w_ff2[l], bsz, seq)
    return x2.reshape(bsz, seq, d)
```

```python
import functools

import jax
import jax.numpy as jnp
from jax import lax
from jax.experimental import pallas as pl
from jax.experimental.pallas import tpu as pltpu

F32 = jnp.float32
BF16 = jnp.bfloat16

D_MODEL = 1024
N_HEADS = 4
HEAD_DIM = 128
D_DN = N_HEADS * HEAD_DIM
D_SC = D_MODEL - D_DN
DN_CONV = 4
SC_CONV = 3
CHUNK = 64
D_FF = 4 * D_MODEL
EPS = 1e-6
N_QKV = 3 * D_DN
N_MAIN = 4 * D_DN + 3 * D_SC
COL_Z = N_QKV
COL_SB = 4 * D_DN
COL_SC = COL_SB + D_SC
COL_SH = COL_SC + D_SC
N_P2 = N_QKV + D_DN + D_SC
COL_ZG = N_QKV
COL_YSC = N_QKV + D_DN
LANES = 128
HALO = 8

TM_IN = 512
TQ_MIX = 512
TM_FFN = 512
FF_CHUNK = 1024
VMEM_LIMIT = 56 * 1024 * 1024


def _sigmoid(x):
    return 0.5 + 0.5 * jnp.tanh(0.5 * x)


def _mm(a, b):
    return jnp.dot(a.astype(BF16), b.astype(BF16), preferred_element_type=F32)


def _mm_nt(a, b):
    return lax.dot_general(a.astype(BF16), b.astype(BF16), (((1,), (1,)), ((), ())),
                           preferred_element_type=F32)


def _mm_tn(a, b):
    return lax.dot_general(a.astype(BF16), b.astype(BF16), (((0,), (0,)), ((), ())),
                           preferred_element_type=F32)


def _rms_scale(x):
    return lax.rsqrt(jnp.mean(x * x, axis=-1, keepdims=True) + EPS)


def _ada_kernel(c_ref, w_ref, b_ref, o_ref):
    c = c_ref[...]
    ca = (c * _sigmoid(c)).astype(BF16)
    o_ref[...] = jnp.dot(ca, w_ref[...].astype(BF16), preferred_element_type=F32) + b_ref[...]


def _ada(c, w_ada, b_ada):
    bsz = c.shape[0]
    n = w_ada.shape[1]
    bn = 1024
    return pl.pallas_call(
        _ada_kernel,
        grid=(n // bn,),
        in_specs=[
            pl.BlockSpec((bsz, D_MODEL), lambda j: (0, 0)),
            pl.BlockSpec((D_MODEL, bn), lambda j: (0, j)),
            pl.BlockSpec((1, bn), lambda j: (0, j)),
        ],
        out_specs=pl.BlockSpec((bsz, bn), lambda j: (0, j)),
        out_shape=jax.ShapeDtypeStruct((bsz, n), F32),
        compiler_params=pltpu.CompilerParams(dimension_semantics=("arbitrary",),
                                             vmem_limit_bytes=VMEM_LIMIT),
        name="ada_mod",
    )(c, w_ada, b_ada)


def _causal_conv(ext, w_ref, c0, width):
    r = ext.shape[0] - HALO
    n = ext.shape[1]
    acc = w_ref[width - 1:width, c0:c0 + n] * ext[HALO:HALO + r, :]
    for s in range(1, width):
        acc = acc + w_ref[width - 1 - s:width - s, c0:c0 + n] * pltpu.roll(ext, s, axis=0)[HALO:HALO + r, :]
    return acc


def _in_kernel(x_ref, mod_ref, nw_ref, wm_ref, wab_ref, cw_ref, scw_ref, gp_ref,
               proj_ref, gb_ref, e_s, p_s, *, per_b):
    tm = x_ref.shape[0]

    @pl.when(pl.program_id(0) % per_b == 0)
    def _():
        e_s[:, 0:HALO, :] = jnp.zeros((e_s.shape[0], HALO, D_DN), F32)

    x = x_ref[...]
    h = x * _rms_scale(x) * nw_ref[...]
    h = h * (1.0 + mod_ref[1:2, :]) + mod_ref[0:1, :]
    hb = h.astype(BF16)

    def proj(c0, n=D_DN):
        return jnp.dot(hb, wm_ref[:, c0:c0 + n], preferred_element_type=F32)

    half = tm // 2
    for part in range(3):
        e_s[part, HALO:HALO + tm, :] = proj(part * D_DN)
    e_s[3, HALO:HALO + tm, :] = proj(COL_SC)
    p_s[0] = proj(COL_Z)
    p_s[1] = proj(COL_SH)
    p_s[2] = proj(COL_SB)
    ab = jnp.dot(hb, wab_ref[...], preferred_element_type=F32)

    for part in range(3):
        c0 = part * D_DN
        for hf in range(2):
            r0 = hf * half
            acc = _causal_conv(e_s[part, r0:r0 + half + HALO, :], cw_ref, c0, DN_CONV)
            y = acc * _sigmoid(acc)
            if part < 2:
                parts = []
                for hd in range(N_HEADS):
                    yh = y[:, hd * HEAD_DIM:(hd + 1) * HEAD_DIM]
                    nrm = lax.rsqrt(jnp.sum(yh * yh, axis=-1, keepdims=True) + EPS)
                    if part == 0:
                        nrm = nrm * (HEAD_DIM ** -0.5)
                    parts.append(yh * nrm)
                y = jnp.concatenate(parts, axis=1)
            proj_ref[r0:r0 + half, c0:c0 + D_DN] = y.astype(BF16)

    p_z = p_s[0]
    proj_ref[:, COL_ZG:COL_ZG + D_DN] = (p_z * _sigmoid(p_z)).astype(BF16)
    e_s[3, HALO:HALO + tm, :] = e_s[3, HALO:HALO + tm, :] * p_s[1]
    conv = _causal_conv(e_s[3], scw_ref, 0, SC_CONV)
    proj_ref[:, COL_YSC:COL_YSC + D_SC] = (p_s[2] * conv).astype(BF16)
    e_s[:, 0:HALO, :] = e_s[:, tm:tm + HALO, :]

    xg = ab + gp_ref[1:2, :]
    softplus = jnp.maximum(xg, 0.0) + jnp.log1p(jnp.exp(-jnp.abs(xg)))
    lane = lax.broadcasted_iota(jnp.int32, ab.shape, 1)
    gb_ref[...] = jnp.where(lane < N_HEADS, -jnp.exp(gp_ref[0:1, :]) * softplus, _sigmoid(ab))


def _in_proj(x2, mod3, nw, w_main, w_ab, cw, scw, gp, seq):
    t = x2.shape[0]
    tm = TM_IN
    per_b = seq // tm
    const = lambda i: (0, 0)
    return pl.pallas_call(
        functools.partial(_in_kernel, per_b=per_b),
        grid=(t // tm,),
        in_specs=[
            pl.BlockSpec((tm, D_MODEL), lambda i: (i, 0)),
            pl.BlockSpec((None, 6, D_MODEL), lambda i: (i // per_b, 0, 0)),
            pl.BlockSpec((1, D_MODEL), const),
            pl.BlockSpec((D_MODEL, N_MAIN), const),
            pl.BlockSpec((D_MODEL, LANES), const),
            pl.BlockSpec((DN_CONV, N_QKV), const),
            pl.BlockSpec((SC_CONV, D_SC), const),
            pl.BlockSpec((8, LANES), const),
        ],
        out_specs=[
            pl.BlockSpec((tm, N_P2), lambda i: (i, 0)),
            pl.BlockSpec((tm, LANES), lambda i: (i, 0)),
        ],
        out_shape=[
            jax.ShapeDtypeStruct((t, N_P2), BF16),
            jax.ShapeDtypeStruct((t, LANES), F32),
        ],
        scratch_shapes=[
            pltpu.VMEM((4, HALO + tm, D_DN), F32),
            pltpu.VMEM((3, tm, D_DN), F32),
        ],
        compiler_params=pltpu.CompilerParams(dimension_semantics=("arbitrary",),
                                             vmem_limit_bytes=VMEM_LIMIT),
        name="in_proj",
    )(x2, mod3, nw, w_main, w_ab, cw, scw, gp)


def _mix_tile(proj_ref, gb_ref, x_ref, mod_ref, dnw_ref, wout_ref, pw_ref, dst_ref,
              o_s, state_s, aq_s, bb_s, op_s, gam_s, bdm_s, lvl_s, *, seq_start, first_step):
    tq = x_ref.shape[0]

    @pl.when(seq_start)
    def _():
        state_s[...] = jnp.zeros_like(state_s)

    q_ref = proj_ref.at[:, 0:D_DN]
    k_ref = proj_ref.at[:, D_DN:2 * D_DN]
    v_ref = proj_ref.at[:, 2 * D_DN:3 * D_DN]

    n_chunks = tq // CHUNK
    quad = N_HEADS * CHUNK
    row_q = lax.broadcasted_iota(jnp.int32, (CHUNK, quad), 0)
    lane_q = lax.broadcasted_iota(jnp.int32, (CHUNK, quad), 1)
    col_q = lane_q & (CHUNK - 1)
    head_q = lane_q >> 6
    tri_incl = row_q >= col_q
    tri_strict = row_q > col_q
    eye_q = row_q == col_q
    eye_f = eye_q.astype(F32)
    row_l = lax.broadcasted_iota(jnp.int32, (CHUNK, LANES), 0)

    def level_mask(r, c, l):
        return (((r >> l) & 1) == 1) & (((c >> l) & 1) == 0) & ((r >> (l + 1)) == (c >> (l + 1)))

    @pl.when(first_step)
    def _():
        r2 = lax.broadcasted_iota(jnp.int32, (quad, quad), 0)
        c2 = lax.broadcasted_iota(jnp.int32, (quad, quad), 1)
        same_head = (r2 >> 6) == (c2 >> 6)
        bdm_s[...] = same_head.astype(F32).astype(BF16)
        for l in range(1, 6):
            lm = same_head & level_mask(r2 & (CHUNK - 1), c2 & (CHUNK - 1), l)
            lvl_s[l - 1] = lm.astype(F32).astype(BF16)

    lm0_f = level_mask(row_q, col_q, 0).astype(F32)
    zero_blk = jnp.zeros((CHUNK, HEAD_DIM), BF16)

    def head_bcast(cols, width):
        if width == HEAD_DIM:
            return jnp.concatenate([jnp.broadcast_to(c_, (CHUNK, HEAD_DIM)) for c_ in cols], axis=1)
        full = [jnp.broadcast_to(c_, (CHUNK, quad)) for c_ in cols]
        return jnp.where(head_q == 0, full[0],
                         jnp.where(head_q == 1, full[1], jnp.where(head_q == 2, full[2], full[3])))

    def bd_wide(a):
        rows = []
        for h in range(N_HEADS):
            rows.append(jnp.concatenate(
                [a[:, j * HEAD_DIM:(j + 1) * HEAD_DIM] if j == h else zero_blk for j in range(N_HEADS)],
                axis=1))
        return jnp.concatenate(rows, axis=0)

    def tile4(a):
        return jnp.concatenate([a] * N_HEADS, axis=0)

    chunks = range(n_chunks)
    rows = [slice(c * CHUNK, (c + 1) * CHUNK) for c in chunks]
    x_inv, m_t, attn16, b_j, eg_j, g_w = [], [], [], [], [], []
    for c in chunks:
        bt = gb_ref[rows[c], :]
        gcum = bt
        for s in (1, 2, 4, 8, 16, 32):
            gcum = gcum + jnp.where(row_l >= s, pltpu.roll(gcum, s, axis=0), 0.0)
        g_cols = [gcum[:, h:h + 1] for h in range(N_HEADS)]
        b_cols = [bt[:, N_HEADS + h:N_HEADS + h + 1] for h in range(N_HEADS)]
        g_i = head_bcast(g_cols, CHUNK)
        b_i = head_bcast(b_cols, CHUNK)
        g_w.append(head_bcast(g_cols, HEAD_DIM))
        g_j = jnp.sum(jnp.where(eye_q, g_i, 0.0), axis=0, keepdims=True)
        b_j.append(jnp.sum(jnp.where(eye_q, b_i, 0.0), axis=0, keepdims=True))
        eg_j.append(jnp.exp(g_j))
        decay = jnp.exp(jnp.where(tri_incl, g_i - g_j, -1e30))

        k16 = k_ref[rows[c], :]
        kq = jnp.concatenate([k16, q_ref[rows[c], :]], axis=0)
        kq_k = lax.dot_general(kq, bd_wide(k16), (((1,), (1,)), ((), ())),
                               preferred_element_type=F32)
        m_st = jnp.where(tri_strict, kq_k[:CHUNK] * b_i * decay, 0.0)
        attn16.append((kq_k[CHUNK:] * decay).astype(BF16))
        x_inv.append(eye_f - m_st * lm0_f)
        m_t.append(tile4(m_st.astype(BF16)))
        if c % 2 == 1:
            yield

    for l in range(1, 6):
        x16 = [x_inv[c].astype(BF16) for c in chunks]
        y = [jnp.dot(x16[c], m_t[c] * lvl_s[l - 1], preferred_element_type=F32) for c in chunks]
        z = [jnp.dot(y[c].astype(BF16), tile4(x16[c]) * bdm_s[...], preferred_element_type=F32)
             for c in chunks]
        x_inv = [x_inv[c] - z[c] for c in chunks]
        yield

    for c in chunks:
        k16 = k_ref[rows[c], :]
        q = q_ref[rows[c], :].astype(F32)
        k = k16.astype(F32)
        xb = x_inv[c] * b_j[c]
        u = jnp.dot(xb.astype(BF16), bd_wide(v_ref[rows[c], :]), preferred_element_type=F32)
        w = jnp.dot((xb * eg_j[c]).astype(BF16), bd_wide(k16), preferred_element_type=F32)
        u16 = u.astype(BF16)
        w16 = w.astype(BF16)
        a_wu = jnp.dot(attn16[c], jnp.concatenate([bd_wide(w16), bd_wide(u16)], axis=1),
                       preferred_element_type=F32)
        g_last = g_w[c][CHUNK - 1:CHUNK, :]
        aq_s[c, 2 * CHUNK:3 * CHUNK, :] = (q * jnp.exp(g_w[c]) - a_wu[:, :D_DN]).astype(BF16)
        op_s[c] = a_wu[:, D_DN:]
        gam_s[c] = jnp.broadcast_to(jnp.exp(g_last), (8, D_DN))
        k_dec = (k * jnp.exp(g_last - g_w[c])).astype(BF16)
        for h in range(N_HEADS):
            hs = slice(h * HEAD_DIM, (h + 1) * HEAD_DIM)
            ab_h = _mm_tn(k_dec[:, hs], jnp.concatenate([w16[:, hs], u16[:, hs]], axis=1))
            aq_s[c, 0:2 * CHUNK, hs] = ab_h[:, :HEAD_DIM].astype(BF16)
            bb_s[c, :, hs] = ab_h[:, HEAD_DIM:]
        if c % 2 == 1:
            yield

    state = state_s[...]
    zero_sq = jnp.zeros((HEAD_DIM, HEAD_DIM), BF16)
    for c in range(n_chunks):
        s16 = state.astype(BF16)
        new_parts = []
        o_parts_c = []
        for p in range(N_HEADS // 2):
            ps = slice(p * 2 * HEAD_DIM, (p + 1) * 2 * HEAD_DIM)
            s_a = s16[:, (2 * p) * HEAD_DIM:(2 * p + 1) * HEAD_DIM]
            s_b = s16[:, (2 * p + 1) * HEAD_DIM:(2 * p + 2) * HEAD_DIM]
            bd = jnp.concatenate([jnp.concatenate([s_a, zero_sq], axis=1),
                                  jnp.concatenate([zero_sq, s_b], axis=1)], axis=0)
            r = jnp.dot(aq_s[c, :, ps], bd, preferred_element_type=F32)
            new_parts.append(gam_s[c, 0:1, ps] * state[:, ps] + bb_s[c, :, ps] - r[:2 * CHUNK])
            o_parts_c.append(op_s[c, :, ps] + r[2 * CHUNK:])
        state = jnp.concatenate(new_parts, axis=1)
        o_s[c * CHUNK:(c + 1) * CHUNK, :] = jnp.concatenate(o_parts_c, axis=1)
        if c % 2 == 1:
            if c == n_chunks - 1:
                state_s[...] = state
            yield

    half = tq // 2
    for hf in range(2):
        rh = slice(hf * half, (hf + 1) * half)
        zg = proj_ref[rh, COL_ZG:COL_ZG + D_DN].astype(F32)
        o_parts = []
        for h in range(N_HEADS):
            hs = slice(h * HEAD_DIM, (h + 1) * HEAD_DIM)
            oh = o_s[rh, hs]
            o_parts.append(oh * _rms_scale(oh) * dnw_ref[...] * zg[:, hs])
        o_dn = jnp.concatenate(o_parts, axis=1)
        y = (jnp.dot(o_dn.astype(BF16), wout_ref[0:D_DN, :], preferred_element_type=F32)
             + jnp.dot(proj_ref[rh, COL_YSC:COL_YSC + D_SC], wout_ref[D_DN:D_MODEL, :],
                       preferred_element_type=F32))
        dst_ref[rh, :] = x_ref[rh, :] + mod_ref[2:3, :] * (y * _rms_scale(y) * pw_ref[...])
        if hf == 0:
            yield


def _ffn_tile(src_ref, mod_ref, nw_ref, w1_ref, w2_ref, pw_ref, dst_ref):
    x = src_ref[...]
    h = x * _rms_scale(x) * nw_ref[...]
    h = h * (1.0 + mod_ref[4:5, :]) + mod_ref[3:4, :]
    hb = h.astype(BF16)
    nh = D_MODEL // 2
    y = [jnp.zeros((x.shape[0], nh), F32) for _ in range(2)]
    for j in range(D_FF // FF_CHUNK):
        a = []
        for s in range(2):
            c0 = j * FF_CHUNK + s * (FF_CHUNK // 2)
            a_s = jnp.dot(hb, w1_ref[:, c0:c0 + FF_CHUNK // 2], preferred_element_type=F32)
            a.append(jnp.square(jnp.maximum(a_s, 0.0)).astype(BF16))
            yield
        a = jnp.concatenate(a, axis=1)
        for s in range(2):
            y[s] = y[s] + jnp.dot(a, w2_ref[j * FF_CHUNK:(j + 1) * FF_CHUNK, s * nh:(s + 1) * nh],
                                  preferred_element_type=F32)
            yield
    y = jnp.concatenate(y, axis=1)
    dst_ref[...] = src_ref[...] + mod_ref[5:6, :] * (y * _rms_scale(y) * pw_ref[...])


def _interleave(*gens):
    live = list(gens)
    while live:
        for g in list(live):
            try:
                next(g)
            except StopIteration:
                live.remove(g)


def _mix_ffn_kernel(proj_ref, gb_ref, x_ref, modm_ref, dnw_ref, wout_ref, pwm_ref,
                    modf_ref, nwf_ref, w1_ref, w2_ref, pwf_ref, out_ref,
                    x1_s, *mix_scratch, per_b):
    i = pl.program_id(0)
    last = pl.num_programs(0) - 2
    slot = i % 2

    @pl.when(i == 0)
    def _():
        x1_s[1] = jnp.zeros(x1_s.shape[1:], F32)

    _interleave(
        _mix_tile(proj_ref, gb_ref, x_ref, modm_ref, dnw_ref, wout_ref, pwm_ref, x1_s.at[slot],
                  *mix_scratch, seq_start=(jnp.minimum(i, last) % per_b) == 0, first_step=i == 0),
        _ffn_tile(x1_s.at[1 - slot], modf_ref, nwf_ref, w1_ref, w2_ref, pwf_ref, out_ref))


def _mixer_ffn(proj, gb, x2, mod3, dnw, w_out, pwm, nwf, w1, w2, pwf, seq):
    tq = TQ_MIX
    per_b = seq // tq
    t = x2.shape[0]
    n_tiles = t // tq

    def tile_m(i):
        return (jnp.minimum(i, n_tiles - 1), 0)

    def tile_f(i):
        return (jnp.maximum(i - 1, 0), 0)

    const = lambda i: (0, 0)
    resident = dict(pipeline_mode=pl.Buffered(1))
    return pl.pallas_call(
        functools.partial(_mix_ffn_kernel, per_b=per_b),
        grid=(n_tiles + 1,),
        in_specs=[
            pl.BlockSpec((tq, N_P2), tile_m),
            pl.BlockSpec((tq, LANES), tile_m),
            pl.BlockSpec((tq, D_MODEL), tile_m),
            pl.BlockSpec((None, 6, D_MODEL), lambda i: (jnp.minimum(i, n_tiles - 1) // per_b, 0, 0)),
            pl.BlockSpec((1, HEAD_DIM), const),
            pl.BlockSpec((D_MODEL, D_MODEL), const, **resident),
            pl.BlockSpec((1, D_MODEL), const),
            pl.BlockSpec((None, 6, D_MODEL), lambda i: (jnp.maximum(i - 1, 0) // per_b, 0, 0)),
            pl.BlockSpec((1, D_MODEL), const),
            pl.BlockSpec((D_MODEL, D_FF), const, **resident),
            pl.BlockSpec((D_FF, D_MODEL), const, **resident),
            pl.BlockSpec((1, D_MODEL), const),
        ],
        out_specs=pl.BlockSpec((tq, D_MODEL), tile_f),
        out_shape=jax.ShapeDtypeStruct((t, D_MODEL), F32),
        scratch_shapes=[
            pltpu.VMEM((2, tq, D_MODEL), F32),
            pltpu.VMEM((tq, D_DN), F32),
            pltpu.VMEM((HEAD_DIM, D_DN), F32),
            pltpu.VMEM((tq // CHUNK, 3 * CHUNK, D_DN), BF16),
            pltpu.VMEM((tq // CHUNK, HEAD_DIM, D_DN), F32),
            pltpu.VMEM((tq // CHUNK, CHUNK, D_DN), F32),
            pltpu.VMEM((tq // CHUNK, 8, D_DN), F32),
            pltpu.VMEM((N_HEADS * CHUNK, N_HEADS * CHUNK), BF16),
            pltpu.VMEM((5, N_HEADS * CHUNK, N_HEADS * CHUNK), BF16),
        ],
        compiler_params=pltpu.CompilerParams(dimension_semantics=("arbitrary",),
                                             vmem_limit_bytes=VMEM_LIMIT),
        name="mixer_ffn",
    )(proj, gb, x2, mod3, dnw, w_out, pwm, mod3, nwf, w1, w2, pwf)


def _layer(x2, c, w_ada, b_ada, pre_mix_w, post_mix_w, w_in, dn_conv_w, dn_a_log, dn_dt_bias,
           dn_norm_w, sc_conv_w, w_out, pre_ffn_w, post_ffn_w, w_ff1, w_ff2, bsz, seq):
    mod3 = _ada(c, w_ada, b_ada[None, :]).reshape(bsz, 6, D_MODEL)

    n_gate = 2 * N_HEADS
    w_main = jnp.concatenate([w_in[:, :COL_SB], w_in[:, COL_SB + n_gate:]], axis=1).astype(BF16)
    w_ab = jnp.pad(w_in[:, COL_SB:COL_SB + n_gate], ((0, 0), (0, LANES - n_gate))).astype(BF16)
    gp = jnp.zeros((8, LANES), F32)
    gp = gp.at[0, :N_HEADS].set(dn_a_log).at[1, :N_HEADS].set(dn_dt_bias)
    proj, gb = _in_proj(x2, mod3, pre_mix_w[None, :], w_main, w_ab, dn_conv_w, sc_conv_w, gp, seq)

    return _mixer_ffn(proj, gb, x2, mod3, dn_norm_w[None, :], w_out.astype(BF16), post_mix_w[None, :],
                      pre_ffn_w[None, :], w_ff1.astype(BF16), w_ff2.astype(BF16), post_ffn_w[None, :],
                      seq)


def kernel(x, c, w_ada, b_ada, pre_mix_norm_w, post_mix_norm_w, w_in, dn_conv_w, dn_a_log, dn_dt_bias, dn_norm_w, sc_conv_w, w_out, pre_ffn_norm_w, post_ffn_norm_w, w_ff1, w_ff2):
    bsz, seq, d = x.shape
    x2 = x.reshape(bsz * seq, d)
    for l in range(w_ada.shape[0]):
        x2 = _layer(x2, c, w_ada[l], b_ada[l], pre_mix_norm_w[l], post_mix_norm_w[l], w_in[l],
                    dn_conv_w[l], dn_a_log[l], dn_dt_bias[l], dn_norm_w[l], sc_conv_w[l], w_out[l],
                    pre_ffn_norm_w[l], post_ffn_norm_w[l], w_ff1[l], w_ff2[l], bsz, seq)
    return x2.reshape(bsz, seq, d)
```

```python
import functools

import jax
import jax.numpy as jnp
from jax import lax
from jax.experimental import pallas as pl
from jax.experimental.pallas import tpu as pltpu

F32 = jnp.float32
BF16 = jnp.bfloat16

D_MODEL = 1024
N_HEADS = 4
HEAD_DIM = 128
D_DN = N_HEADS * HEAD_DIM
D_SC = D_MODEL - D_DN
DN_CONV = 4
SC_CONV = 3
CHUNK = 64
D_FF = 4 * D_MODEL
EPS = 1e-6
N_QKV = 3 * D_DN
N_MAIN = 4 * D_DN + 3 * D_SC
COL_Z = N_QKV
COL_SB = 4 * D_DN
COL_SC = COL_SB + D_SC
COL_SH = COL_SC + D_SC
N_P2 = N_QKV + D_DN + D_SC
COL_ZG = N_QKV
COL_YSC = N_QKV + D_DN
LANES = 128
HALO = 8

TM_IN = 512
TQ_MIX = 512
GROUP_CHUNKS = 4
TM_FFN = 512
FF_CHUNK = 1024
VMEM_LIMIT = 56 * 1024 * 1024


def _sigmoid(x):
    return 0.5 + 0.5 * jnp.tanh(0.5 * x)


def _mm_tn(a, b):
    return lax.dot_general(a, b, (((0,), (0,)), ((), ())), preferred_element_type=F32)


def _rms_scale(x):
    return lax.rsqrt(jnp.mean(x * x, axis=-1, keepdims=True) + EPS)


def _interleave(gens, lead):
    live = []
    pending = list(gens)
    rnd = 0
    while live or pending:
        if pending and rnd % lead == 0:
            live.append(pending.pop(0))
        for g in list(live):
            try:
                next(g)
            except StopIteration:
                live.remove(g)
        rnd += 1


def _ada_kernel(c_ref, w_ref, b_ref, o_ref):
    c = c_ref[...]
    ca = (c * _sigmoid(c)).astype(BF16)
    o_ref[...] = jnp.dot(ca, w_ref[...].astype(BF16), preferred_element_type=F32) + b_ref[...]


def _ada(c, w_ada, b_ada):
    bsz = c.shape[0]
    n = w_ada.shape[1]
    bn = 1024
    return pl.pallas_call(
        _ada_kernel,
        grid=(n // bn,),
        in_specs=[
            pl.BlockSpec((bsz, D_MODEL), lambda j: (0, 0)),
            pl.BlockSpec((D_MODEL, bn), lambda j: (0, j)),
            pl.BlockSpec((1, bn), lambda j: (0, j)),
        ],
        out_specs=pl.BlockSpec((bsz, bn), lambda j: (0, j)),
        out_shape=jax.ShapeDtypeStruct((bsz, n), F32),
        compiler_params=pltpu.CompilerParams(dimension_semantics=("arbitrary",),
                                             vmem_limit_bytes=VMEM_LIMIT),
        name="ada_mod",
    )(c, w_ada, b_ada)


def _causal_conv(ext, w_ref, c0, width):
    r = ext.shape[0] - HALO
    n = ext.shape[1]
    acc = w_ref[width - 1:width, c0:c0 + n] * ext[HALO:HALO + r, :]
    for s in range(1, width):
        acc = acc + w_ref[width - 1 - s:width - s, c0:c0 + n] * pltpu.roll(ext, s, axis=0)[HALO:HALO + r, :]
    return acc


def _in_half(hf, x_ref, mod_ref, nw_ref, wm_ref, wab_ref, cw_ref, scw_ref, gp_ref,
             proj_ref, gb_ref, e_s, p_s):
    half = x_ref.shape[0] // 2
    r0 = hf * half
    rows = slice(r0, r0 + half)
    erows = slice(HALO + r0, HALO + r0 + half)
    wrows = slice(r0, r0 + half + HALO)

    x = x_ref[rows, :]
    h = x * _rms_scale(x) * nw_ref[...]
    h = h * (1.0 + mod_ref[1:2, :]) + mod_ref[0:1, :]
    hb = h.astype(BF16)
    yield

    def proj(c0, n=D_DN):
        return jnp.dot(hb, wm_ref[:, c0:c0 + n], preferred_element_type=F32)

    for part in range(3):
        e_s[part, erows, :] = proj(part * D_DN)
        yield
    e_s[3, erows, :] = proj(COL_SC)
    p_s[0, rows, :] = proj(COL_Z)
    yield
    p_s[1, rows, :] = proj(COL_SH)
    p_s[2, rows, :] = proj(COL_SB)
    ab = jnp.dot(hb, wab_ref[...], preferred_element_type=F32)
    yield

    for part in range(3):
        c0 = part * D_DN
        acc = _causal_conv(e_s[part, wrows, :], cw_ref, c0, DN_CONV)
        y = acc * _sigmoid(acc)
        if part < 2:
            parts = []
            for hd in range(N_HEADS):
                yh = y[:, hd * HEAD_DIM:(hd + 1) * HEAD_DIM]
                nrm = lax.rsqrt(jnp.sum(yh * yh, axis=-1, keepdims=True) + EPS)
                if part == 0:
                    nrm = nrm * (HEAD_DIM ** -0.5)
                parts.append(yh * nrm)
            y = jnp.concatenate(parts, axis=1)
        proj_ref[rows, c0:c0 + D_DN] = y.astype(BF16)
        yield

    p_z = p_s[0, rows, :]
    proj_ref[rows, COL_ZG:COL_ZG + D_DN] = (p_z * _sigmoid(p_z)).astype(BF16)
    e_s[3, erows, :] = e_s[3, erows, :] * p_s[1, rows, :]
    conv = _causal_conv(e_s[3, wrows, :], scw_ref, 0, SC_CONV)
    proj_ref[rows, COL_YSC:COL_YSC + D_SC] = (p_s[2, rows, :] * conv).astype(BF16)

    xg = ab + gp_ref[1:2, :]
    softplus = jnp.maximum(xg, 0.0) + jnp.log1p(jnp.exp(-jnp.abs(xg)))
    lane = lax.broadcasted_iota(jnp.int32, ab.shape, 1)
    gb_ref[rows, :] = jnp.where(lane < N_HEADS, -jnp.exp(gp_ref[0:1, :]) * softplus, _sigmoid(ab))


def _in_kernel(x_ref, mod_ref, nw_ref, wm_ref, wab_ref, cw_ref, scw_ref, gp_ref,
               proj_ref, gb_ref, e_s, p_s, *, per_b):
    tm = x_ref.shape[0]

    @pl.when(pl.program_id(0) % per_b == 0)
    def _():
        e_s[:, 0:HALO, :] = jnp.zeros((e_s.shape[0], HALO, D_DN), F32)

    _interleave([_in_half(hf, x_ref, mod_ref, nw_ref, wm_ref, wab_ref, cw_ref, scw_ref, gp_ref,
                          proj_ref, gb_ref, e_s, p_s) for hf in range(2)], lead=5)
    e_s[:, 0:HALO, :] = e_s[:, tm:tm + HALO, :]


def _in_proj(x2, mod3, nw, w_main, w_ab, cw, scw, gp, seq):
    t = x2.shape[0]
    tm = TM_IN
    per_b = seq // tm
    const = lambda i: (0, 0)
    return pl.pallas_call(
        functools.partial(_in_kernel, per_b=per_b),
        grid=(t // tm,),
        in_specs=[
            pl.BlockSpec((tm, D_MODEL), lambda i: (i, 0)),
            pl.BlockSpec((None, 6, D_MODEL), lambda i: (i // per_b, 0, 0)),
            pl.BlockSpec((1, D_MODEL), const),
            pl.BlockSpec((D_MODEL, N_MAIN), const),
            pl.BlockSpec((D_MODEL, LANES), const),
            pl.BlockSpec((DN_CONV, N_QKV), const),
            pl.BlockSpec((SC_CONV, D_SC), const),
            pl.BlockSpec((8, LANES), const),
        ],
        out_specs=[
            pl.BlockSpec((tm, N_P2), lambda i: (i, 0)),
            pl.BlockSpec((tm, LANES), lambda i: (i, 0)),
        ],
        out_shape=[
            jax.ShapeDtypeStruct((t, N_P2), BF16),
            jax.ShapeDtypeStruct((t, LANES), F32),
        ],
        scratch_shapes=[
            pltpu.VMEM((4, HALO + tm, D_DN), F32),
            pltpu.VMEM((3, tm, D_DN), F32),
        ],
        compiler_params=pltpu.CompilerParams(dimension_semantics=("arbitrary",),
                                             vmem_limit_bytes=VMEM_LIMIT),
        name="in_proj",
    )(x2, mod3, nw, w_main, w_ab, cw, scw, gp)


def _mix_kernel(proj_ref, gb_ref, x_ref, mod_ref, dnw_ref, wout_ref, pw_ref, out_ref,
                o_s, state_s, aq_s, bb_s, op_s, gam_s, bdm_s, lvl_s, xs_s, mt_s, at_s, gw_s, rv_s):
    tq = x_ref.shape[0]
    t = pl.program_id(1)

    @pl.when(t == 0)
    def _():
        state_s[...] = jnp.zeros_like(state_s)

    q_ref = proj_ref.at[:, 0:D_DN]
    k_ref = proj_ref.at[:, D_DN:2 * D_DN]
    v_ref = proj_ref.at[:, 2 * D_DN:3 * D_DN]

    n_chunks = tq // CHUNK
    quad = N_HEADS * CHUNK
    row_q = lax.broadcasted_iota(jnp.int32, (CHUNK, quad), 0)
    lane_q = lax.broadcasted_iota(jnp.int32, (CHUNK, quad), 1)
    col_q = lane_q & (CHUNK - 1)
    head_q = lane_q >> 6
    tri_incl = row_q >= col_q
    tri_strict = row_q > col_q
    eye_q = row_q == col_q
    eye_f = eye_q.astype(F32)
    row_l = lax.broadcasted_iota(jnp.int32, (CHUNK, LANES), 0)

    def level_mask(r, c, l):
        return (((r >> l) & 1) == 1) & (((c >> l) & 1) == 0) & ((r >> (l + 1)) == (c >> (l + 1)))

    @pl.when((pl.program_id(0) == 0) & (t == 0))
    def _():
        r2 = lax.broadcasted_iota(jnp.int32, (quad, quad), 0)
        c2 = lax.broadcasted_iota(jnp.int32, (quad, quad), 1)
        same_head = (r2 >> 6) == (c2 >> 6)
        bdm_s[...] = same_head.astype(F32).astype(BF16)
        for l in range(1, 6):
            lm = same_head & level_mask(r2 & (CHUNK - 1), c2 & (CHUNK - 1), l)
            lvl_s[l - 1] = lm.astype(F32).astype(BF16)

    lm0_f = level_mask(row_q, col_q, 0).astype(F32)
    zero_blk = jnp.zeros((CHUNK, HEAD_DIM), BF16)
    zero_sq = jnp.zeros((HEAD_DIM, HEAD_DIM), BF16)

    def head_bcast(cols, width):
        if width == HEAD_DIM:
            return jnp.concatenate([jnp.broadcast_to(c_, (CHUNK, HEAD_DIM)) for c_ in cols], axis=1)
        full = [jnp.broadcast_to(c_, (CHUNK, quad)) for c_ in cols]
        return jnp.where(head_q == 0, full[0],
                         jnp.where(head_q == 1, full[1], jnp.where(head_q == 2, full[2], full[3])))

    def bd_wide(a):
        rows = []
        for h in range(N_HEADS):
            rows.append(jnp.concatenate(
                [a[:, j * HEAD_DIM:(j + 1) * HEAD_DIM] if j == h else zero_blk for j in range(N_HEADS)],
                axis=1))
        return jnp.concatenate(rows, axis=0)

    def tile4(a):
        return jnp.concatenate([a] * N_HEADS, axis=0)

    carried = {"state": None}

    def group(gi):
        chunks = range(gi * GROUP_CHUNKS, (gi + 1) * GROUP_CHUNKS)
        rows = {c: slice(c * CHUNK, (c + 1) * CHUNK) for c in chunks}

        for n, c in enumerate(chunks):
            bt = gb_ref[rows[c], :]
            gcum = bt
            for s in (1, 2, 4, 8, 16, 32):
                gcum = gcum + jnp.where(row_l >= s, pltpu.roll(gcum, s, axis=0), 0.0)
            g_cols = [gcum[:, h:h + 1] for h in range(N_HEADS)]
            b_cols = [bt[:, N_HEADS + h:N_HEADS + h + 1] for h in range(N_HEADS)]
            g_i = head_bcast(g_cols, CHUNK)
            b_i = head_bcast(b_cols, CHUNK)
            gw_s[c] = head_bcast(g_cols, HEAD_DIM)
            g_j = jnp.sum(jnp.where(eye_q, g_i, 0.0), axis=0, keepdims=True)
            b_j = jnp.sum(jnp.where(eye_q, b_i, 0.0), axis=0, keepdims=True)
            rv_s[c, 0:1, :] = b_j
            rv_s[c, 1:2, :] = b_j * jnp.exp(g_j)
            decay = jnp.exp(jnp.where(tri_incl, g_i - g_j, -1e30))

            k16 = k_ref[rows[c], :]
            kq = jnp.concatenate([k16, q_ref[rows[c], :]], axis=0)
            kq_k = lax.dot_general(kq, bd_wide(k16), (((1,), (1,)), ((), ())),
                                   preferred_element_type=F32)
            m_st = jnp.where(tri_strict, kq_k[:CHUNK] * b_i * decay, 0.0)
            at_s[c] = (kq_k[CHUNK:] * decay).astype(BF16)
            xs_s[c] = eye_f - m_st * lm0_f
            mt_s[c] = m_st.astype(BF16)
            if n % 2 == 1:
                yield

        for l in range(1, 6):
            for c in chunks:
                x = xs_s[c]
                x16 = x.astype(BF16)
                y = jnp.dot(x16, tile4(mt_s[c]) * lvl_s[l - 1], preferred_element_type=F32)
                z = jnp.dot(y.astype(BF16), tile4(x16) * bdm_s[...], preferred_element_type=F32)
                xs_s[c] = x - z
            yield

        for n, c in enumerate(chunks):
            k16 = k_ref[rows[c], :]
            q = q_ref[rows[c], :].astype(F32)
            k = k16.astype(F32)
            x = xs_s[c]
            u = jnp.dot((x * rv_s[c, 0:1, :]).astype(BF16), bd_wide(v_ref[rows[c], :]),
                        preferred_element_type=F32)
            w = jnp.dot((x * rv_s[c, 1:2, :]).astype(BF16), bd_wide(k16), preferred_element_type=F32)
            u16 = u.astype(BF16)
            w16 = w.astype(BF16)
            a_wu = jnp.dot(at_s[c], jnp.concatenate([bd_wide(w16), bd_wide(u16)], axis=1),
                           preferred_element_type=F32)
            g_wide = gw_s[c]
            g_last = g_wide[CHUNK - 1:CHUNK, :]
            aq_s[c, 2 * CHUNK:3 * CHUNK, :] = (q * jnp.exp(g_wide) - a_wu[:, :D_DN]).astype(BF16)
            op_s[c] = a_wu[:, D_DN:]
            gam_s[c] = jnp.broadcast_to(jnp.exp(g_last), (8, D_DN))
            k_dec = (k * jnp.exp(g_last - g_wide)).astype(BF16)
            for h in range(N_HEADS):
                hs = slice(h * HEAD_DIM, (h + 1) * HEAD_DIM)
                ab_h = _mm_tn(k_dec[:, hs], jnp.concatenate([w16[:, hs], u16[:, hs]], axis=1))
                aq_s[c, 0:2 * CHUNK, hs] = ab_h[:, :HEAD_DIM].astype(BF16)
                bb_s[c, :, hs] = ab_h[:, HEAD_DIM:]
            if n % 2 == 1:
                yield

        state = state_s[...] if gi == 0 else carried["state"]
        for c in chunks:
            s16 = state.astype(BF16)
            new_parts = []
            o_parts_c = []
            for p in range(N_HEADS // 2):
                ps = slice(p * 2 * HEAD_DIM, (p + 1) * 2 * HEAD_DIM)
                s_a = s16[:, (2 * p) * HEAD_DIM:(2 * p + 1) * HEAD_DIM]
                s_b = s16[:, (2 * p + 1) * HEAD_DIM:(2 * p + 2) * HEAD_DIM]
                bd = jnp.concatenate([jnp.concatenate([s_a, zero_sq], axis=1),
                                      jnp.concatenate([zero_sq, s_b], axis=1)], axis=0)
                r = jnp.dot(aq_s[c, :, ps], bd, preferred_element_type=F32)
                new_parts.append(gam_s[c, 0:1, ps] * state[:, ps] + bb_s[c, :, ps] - r[:2 * CHUNK])
                o_parts_c.append(op_s[c, :, ps] + r[2 * CHUNK:])
            state = jnp.concatenate(new_parts, axis=1)
            o_s[rows[c], :] = jnp.concatenate(o_parts_c, axis=1)
        carried["state"] = state
        if gi == n_chunks // GROUP_CHUNKS - 1:
            state_s[...] = state
        yield

        rg = slice(gi * GROUP_CHUNKS * CHUNK, (gi + 1) * GROUP_CHUNKS * CHUNK)
        zg = proj_ref[rg, COL_ZG:COL_ZG + D_DN].astype(F32)
        o_parts = []
        for h in range(N_HEADS):
            hs = slice(h * HEAD_DIM, (h + 1) * HEAD_DIM)
            oh = o_s[rg, hs]
            o_parts.append(oh * _rms_scale(oh) * dnw_ref[...] * zg[:, hs])
        o_dn = jnp.concatenate(o_parts, axis=1)
        y = (jnp.dot(o_dn.astype(BF16), wout_ref[0:D_DN, :], preferred_element_type=F32)
             + jnp.dot(proj_ref[rg, COL_YSC:COL_YSC + D_SC], wout_ref[D_DN:D_MODEL, :],
                       preferred_element_type=F32))
        out_ref[rg, :] = x_ref[rg, :] + mod_ref[2:3, :] * (y * _rms_scale(y) * pw_ref[...])

    _interleave([group(gi) for gi in range(n_chunks // GROUP_CHUNKS)], lead=2)


def _mixer(proj, gb, x2, mod3, dnw, w_out, pw, bsz, seq):
    tq = TQ_MIX
    per_b = seq // tq
    t = x2.shape[0]

    def tile(b, s):
        return (b * per_b + s, 0)

    const = lambda b, s: (0, 0)
    return pl.pallas_call(
        _mix_kernel,
        grid=(bsz, per_b),
        in_specs=[
            pl.BlockSpec((tq, N_P2), tile),
            pl.BlockSpec((tq, LANES), tile),
            pl.BlockSpec((tq, D_MODEL), tile),
            pl.BlockSpec((None, 6, D_MODEL), lambda b, s: (b, 0, 0)),
            pl.BlockSpec((1, HEAD_DIM), const),
            pl.BlockSpec((D_MODEL, D_MODEL), const),
            pl.BlockSpec((1, D_MODEL), const),
        ],
        out_specs=pl.BlockSpec((tq, D_MODEL), tile),
        out_shape=jax.ShapeDtypeStruct((t, D_MODEL), F32),
        scratch_shapes=[
            pltpu.VMEM((tq, D_DN), F32),
            pltpu.VMEM((HEAD_DIM, D_DN), F32),
            pltpu.VMEM((tq // CHUNK, 3 * CHUNK, D_DN), BF16),
            pltpu.VMEM((tq // CHUNK, HEAD_DIM, D_DN), F32),
            pltpu.VMEM((tq // CHUNK, CHUNK, D_DN), F32),
            pltpu.VMEM((tq // CHUNK, 8, D_DN), F32),
            pltpu.VMEM((N_HEADS * CHUNK, N_HEADS * CHUNK), BF16),
            pltpu.VMEM((5, N_HEADS * CHUNK, N_HEADS * CHUNK), BF16),
            pltpu.VMEM((tq // CHUNK, CHUNK, N_HEADS * CHUNK), F32),
            pltpu.VMEM((tq // CHUNK, CHUNK, N_HEADS * CHUNK), BF16),
            pltpu.VMEM((tq // CHUNK, CHUNK, N_HEADS * CHUNK), BF16),
            pltpu.VMEM((tq // CHUNK, CHUNK, D_DN), F32),
            pltpu.VMEM((tq // CHUNK, 8, N_HEADS * CHUNK), F32),
        ],
        compiler_params=pltpu.CompilerParams(dimension_semantics=("arbitrary", "arbitrary"),
                                             vmem_limit_bytes=VMEM_LIMIT),
        name="token_mixer",
    )(proj, gb, x2, mod3, dnw, w_out, pw)


def _ffn_kernel(x_ref, mod_ref, nw_ref, w1_ref, w2_ref, pw_ref, out_ref):
    x = x_ref[...]
    h = x * _rms_scale(x) * nw_ref[...]
    h = h * (1.0 + mod_ref[4:5, :]) + mod_ref[3:4, :]
    hb = h.astype(BF16)
    y = jnp.zeros(x.shape, F32)
    for j in range(D_FF // FF_CHUNK):
        a = jnp.dot(hb, w1_ref[:, j * FF_CHUNK:(j + 1) * FF_CHUNK], preferred_element_type=F32)
        a = jnp.square(jnp.maximum(a, 0.0)).astype(BF16)
        y = y + jnp.dot(a, w2_ref[j * FF_CHUNK:(j + 1) * FF_CHUNK, :], preferred_element_type=F32)
    out_ref[...] = x + mod_ref[5:6, :] * (y * _rms_scale(y) * pw_ref[...])


def _ffn(x2, mod3, nw, w1, w2, pw, seq):
    t = x2.shape[0]
    tm = TM_FFN
    per_b = seq // tm
    return pl.pallas_call(
        _ffn_kernel,
        grid=(t // tm,),
        in_specs=[
            pl.BlockSpec((tm, D_MODEL), lambda i: (i, 0)),
            pl.BlockSpec((None, 6, D_MODEL), lambda i: (i // per_b, 0, 0)),
            pl.BlockSpec((1, D_MODEL), lambda i: (0, 0)),
            pl.BlockSpec((D_MODEL, D_FF), lambda i: (0, 0)),
            pl.BlockSpec((D_FF, D_MODEL), lambda i: (0, 0)),
            pl.BlockSpec((1, D_MODEL), lambda i: (0, 0)),
        ],
        out_specs=pl.BlockSpec((tm, D_MODEL), lambda i: (i, 0)),
        out_shape=jax.ShapeDtypeStruct((t, D_MODEL), F32),
        compiler_params=pltpu.CompilerParams(dimension_semantics=("arbitrary",),
                                             vmem_limit_bytes=VMEM_LIMIT),
        name="ffn",
    )(x2, mod3, nw, w1, w2, pw)


def _layer(x2, c, w_ada, b_ada, pre_mix_w, post_mix_w, w_in, dn_conv_w, dn_a_log, dn_dt_bias,
           dn_norm_w, sc_conv_w, w_out, pre_ffn_w, post_ffn_w, w_ff1, w_ff2, bsz, seq):
    mod3 = _ada(c, w_ada, b_ada[None, :]).reshape(bsz, 6, D_MODEL)

    n_gate = 2 * N_HEADS
    w_main = jnp.concatenate([w_in[:, :COL_SB], w_in[:, COL_SB + n_gate:]], axis=1).astype(BF16)
    w_ab = jnp.pad(w_in[:, COL_SB:COL_SB + n_gate], ((0, 0), (0, LANES - n_gate))).astype(BF16)
    gp = jnp.zeros((8, LANES), F32)
    gp = gp.at[0, :N_HEADS].set(dn_a_log).at[1, :N_HEADS].set(dn_dt_bias)
    proj, gb = _in_proj(x2, mod3, pre_mix_w[None, :], w_main, w_ab, dn_conv_w, sc_conv_w, gp, seq)

    x2 = _mixer(proj, gb, x2, mod3, dn_norm_w[None, :], w_out.astype(BF16), post_mix_w[None, :],
                bsz, seq)

    return _ffn(x2, mod3, pre_ffn_w[None, :], w_ff1.astype(BF16), w_ff2.astype(BF16),
                post_ffn_w[None, :], seq)


def kernel(x, c, w_ada, b_ada, pre_mix_norm_w, post_mix_norm_w, w_in, dn_conv_w, dn_a_log, dn_dt_bias, dn_norm_w, sc_conv_w, w_out, pre_ffn_norm_w, post_ffn_norm_w, w_ff1, w_ff2):
    bsz, seq, d = x.shape
    x2 = x.reshape(bsz * seq, d)
    for l in range(w_ada.shape[0]):
        x2 = _layer(x2, c, w_ada[l], b_ada[l], pre_mix_norm_w[l], post_mix_norm_w[l], w_in[l],
                    dn_conv_w[l], dn_a_log[l], dn_dt_bias[l], dn_norm_w[l], sc_conv_w[l], w_out[l],
                    pre_ffn_norm_w[l], post_ffn_norm_w[l], w_ff1[l], w_ff2[l], bsz, seq)
    return x2.reshape(bsz, seq, d)
```

```python
import functools

import jax
import jax.numpy as jnp
from jax import lax
from jax.experimental import pallas as pl
from jax.experimental.pallas import tpu as pltpu

F32 = jnp.float32
BF16 = jnp.bfloat16

D_MODEL = 1024
N_HEADS = 4
HEAD_DIM = 128
D_DN = N_HEADS * HEAD_DIM
D_SC = D_MODEL - D_DN
DN_CONV = 4
SC_CONV = 3
CHUNK = 64
D_FF = 4 * D_MODEL
EPS = 1e-6
N_QKV = 3 * D_DN
N_MAIN = 4 * D_DN + 3 * D_SC
COL_Z = N_QKV
COL_SB = 4 * D_DN
COL_SC = COL_SB + D_SC
COL_SH = COL_SC + D_SC
N_P2 = N_QKV + D_DN + D_SC
COL_ZG = N_QKV
COL_YSC = N_QKV + D_DN
LANES = 128
HALO = 8

TM_IN = 512
TQ_MIX = 512
GROUP_CHUNKS = 4
TM_FFN = 512
FF_CHUNK = 1024
VMEM_LIMIT = 56 * 1024 * 1024


def _sigmoid(x):
    return 0.5 + 0.5 * jnp.tanh(0.5 * x)


def _mm_tn(a, b):
    return lax.dot_general(a, b, (((0,), (0,)), ((), ())), preferred_element_type=F32)


def _rms_scale(x):
    return lax.rsqrt(jnp.mean(x * x, axis=-1, keepdims=True) + EPS)


def _interleave(gens, lead):
    live = []
    pending = list(gens)
    rnd = 0
    while live or pending:
        if pending and rnd % lead == 0:
            live.append(pending.pop(0))
        for g in list(live):
            try:
                next(g)
            except StopIteration:
                live.remove(g)
        rnd += 1


def _ada_kernel(c_ref, w_ref, b_ref, o_ref):
    c = c_ref[...]
    ca = (c * _sigmoid(c)).astype(BF16)
    o_ref[...] = jnp.dot(ca, w_ref[...].astype(BF16), preferred_element_type=F32) + b_ref[...]


def _ada(c, w_ada, b_ada):
    bsz = c.shape[0]
    n = w_ada.shape[1]
    bn = 1024
    return pl.pallas_call(
        _ada_kernel,
        grid=(n // bn,),
        in_specs=[
            pl.BlockSpec((bsz, D_MODEL), lambda j: (0, 0)),
            pl.BlockSpec((D_MODEL, bn), lambda j: (0, j)),
            pl.BlockSpec((1, bn), lambda j: (0, j)),
        ],
        out_specs=pl.BlockSpec((bsz, bn), lambda j: (0, j)),
        out_shape=jax.ShapeDtypeStruct((bsz, n), F32),
        compiler_params=pltpu.CompilerParams(dimension_semantics=("arbitrary",),
                                             vmem_limit_bytes=VMEM_LIMIT),
        name="ada_mod",
    )(c, w_ada, b_ada)


def _causal_conv(ext, w_ref, c0, width):
    r = ext.shape[0] - HALO
    n = ext.shape[1]
    acc = w_ref[width - 1:width, c0:c0 + n] * ext[HALO:HALO + r, :]
    for s in range(1, width):
        acc = acc + w_ref[width - 1 - s:width - s, c0:c0 + n] * pltpu.roll(ext, s, axis=0)[HALO:HALO + r, :]
    return acc


def _in_half(hf, x_ref, mod_ref, nw_ref, wm_ref, wab_ref, cw_ref, scw_ref, gp_ref,
             proj_ref, gb_ref, e_s, p_s):
    half = x_ref.shape[0] // 2
    r0 = hf * half
    rows = slice(r0, r0 + half)
    erows = slice(HALO + r0, HALO + r0 + half)
    wrows = slice(r0, r0 + half + HALO)

    x = x_ref[rows, :]
    h = x * _rms_scale(x) * nw_ref[...]
    h = h * (1.0 + mod_ref[1:2, :]) + mod_ref[0:1, :]
    hb = h.astype(BF16)
    yield

    def proj(c0, n=D_DN):
        return jnp.dot(hb, wm_ref[:, c0:c0 + n], preferred_element_type=F32)

    for part in range(3):
        e_s[part, erows, :] = proj(part * D_DN)
        yield
    e_s[3, erows, :] = proj(COL_SC)
    p_s[0, rows, :] = proj(COL_Z)
    yield
    p_s[1, rows, :] = proj(COL_SH)
    p_s[2, rows, :] = proj(COL_SB)
    ab = jnp.dot(hb, wab_ref[...], preferred_element_type=F32)
    yield

    for part in range(3):
        c0 = part * D_DN
        acc = _causal_conv(e_s[part, wrows, :], cw_ref, c0, DN_CONV)
        y = acc * _sigmoid(acc)
        if part < 2:
            parts = []
            for hd in range(N_HEADS):
                yh = y[:, hd * HEAD_DIM:(hd + 1) * HEAD_DIM]
                nrm = lax.rsqrt(jnp.sum(yh * yh, axis=-1, keepdims=True) + EPS)
                if part == 0:
                    nrm = nrm * (HEAD_DIM ** -0.5)
                parts.append(yh * nrm)
            y = jnp.concatenate(parts, axis=1)
        proj_ref[rows, c0:c0 + D_DN] = y.astype(BF16)
        yield

    p_z = p_s[0, rows, :]
    proj_ref[rows, COL_ZG:COL_ZG + D_DN] = (p_z * _sigmoid(p_z)).astype(BF16)
    e_s[3, erows, :] = e_s[3, erows, :] * p_s[1, rows, :]
    conv = _causal_conv(e_s[3, wrows, :], scw_ref, 0, SC_CONV)
    proj_ref[rows, COL_YSC:COL_YSC + D_SC] = (p_s[2, rows, :] * conv).astype(BF16)

    xg = ab + gp_ref[1:2, :]
    softplus = jnp.maximum(xg, 0.0) + jnp.log1p(jnp.exp(-jnp.abs(xg)))
    lane = lax.broadcasted_iota(jnp.int32, ab.shape, 1)
    gb_ref[rows, :] = jnp.where(lane < N_HEADS, -jnp.exp(gp_ref[0:1, :]) * softplus, _sigmoid(ab))


def _in_kernel(x_ref, mod_ref, nw_ref, win_ref, cw_ref, scw_ref, gp_ref,
               proj_ref, gb_ref, e_s, p_s, wm_ref, wab_ref, *, per_b):
    tm = x_ref.shape[0]

    @pl.when(pl.program_id(0) == 0)
    def _():
        n_gate = 2 * N_HEADS
        rb = D_MODEL // 4
        for r in range(4):
            rs = slice(r * rb, (r + 1) * rb)
            wm_ref[rs, 0:COL_SB] = win_ref[rs, 0:COL_SB].astype(BF16)
            wm_ref[rs, COL_SB:N_MAIN] = win_ref[rs, COL_SB + n_gate:N_MAIN + n_gate].astype(BF16)
            wab_ref[rs, :] = win_ref[rs, COL_SB:COL_SB + LANES].astype(BF16)

    @pl.when(pl.program_id(0) % per_b == 0)
    def _():
        e_s[:, 0:HALO, :] = jnp.zeros((e_s.shape[0], HALO, D_DN), F32)

    _interleave([_in_half(hf, x_ref, mod_ref, nw_ref, wm_ref, wab_ref, cw_ref, scw_ref, gp_ref,
                          proj_ref, gb_ref, e_s, p_s) for hf in range(2)], lead=5)
    e_s[:, 0:HALO, :] = e_s[:, tm:tm + HALO, :]


def _in_proj(x2, mod3, nw, w_in, cw, scw, gp, seq):
    t = x2.shape[0]
    tm = TM_IN
    per_b = seq // tm
    const = lambda i: (0, 0)
    return pl.pallas_call(
        functools.partial(_in_kernel, per_b=per_b),
        grid=(t // tm,),
        in_specs=[
            pl.BlockSpec((tm, D_MODEL), lambda i: (i, 0)),
            pl.BlockSpec((None, 6, D_MODEL), lambda i: (i // per_b, 0, 0)),
            pl.BlockSpec((1, D_MODEL), const),
            pl.BlockSpec(w_in.shape, const, pipeline_mode=pl.Buffered(1)),
            pl.BlockSpec((DN_CONV, N_QKV), const),
            pl.BlockSpec((SC_CONV, D_SC), const),
            pl.BlockSpec((8, LANES), const),
        ],
        out_specs=[
            pl.BlockSpec((tm, N_P2), lambda i: (i, 0)),
            pl.BlockSpec((tm, LANES), lambda i: (i, 0)),
        ],
        out_shape=[
            jax.ShapeDtypeStruct((t, N_P2), BF16),
            jax.ShapeDtypeStruct((t, LANES), F32),
        ],
        scratch_shapes=[
            pltpu.VMEM((4, HALO + tm, D_DN), F32),
            pltpu.VMEM((3, tm, D_DN), F32),
            pltpu.VMEM((D_MODEL, N_MAIN), BF16),
            pltpu.VMEM((D_MODEL, LANES), BF16),
        ],
        compiler_params=pltpu.CompilerParams(dimension_semantics=("arbitrary",),
                                             vmem_limit_bytes=VMEM_LIMIT),
        name="in_proj",
    )(x2, mod3, nw, w_in, cw, scw, gp)


def _mix_kernel(proj_ref, gb_ref, x_ref, mod_ref, dnw_ref, wout_ref, pw_ref, out_ref,
                o_s, state_s, aq_s, bb_s, op_s, gam_s, bdm_s, lvl_s, xs_s, mt_s, at_s, gw_s, rv_s):
    tq = x_ref.shape[0]
    t = pl.program_id(1)

    @pl.when(t == 0)
    def _():
        state_s[...] = jnp.zeros_like(state_s)

    q_ref = proj_ref.at[:, 0:D_DN]
    k_ref = proj_ref.at[:, D_DN:2 * D_DN]
    v_ref = proj_ref.at[:, 2 * D_DN:3 * D_DN]

    n_chunks = tq // CHUNK
    quad = N_HEADS * CHUNK
    row_q = lax.broadcasted_iota(jnp.int32, (CHUNK, quad), 0)
    lane_q = lax.broadcasted_iota(jnp.int32, (CHUNK, quad), 1)
    col_q = lane_q & (CHUNK - 1)
    head_q = lane_q >> 6
    tri_incl = row_q >= col_q
    tri_strict = row_q > col_q
    eye_q = row_q == col_q
    eye_f = eye_q.astype(F32)
    row_l = lax.broadcasted_iota(jnp.int32, (CHUNK, LANES), 0)

    def level_mask(r, c, l):
        return (((r >> l) & 1) == 1) & (((c >> l) & 1) == 0) & ((r >> (l + 1)) == (c >> (l + 1)))

    @pl.when((pl.program_id(0) == 0) & (t == 0))
    def _():
        r2 = lax.broadcasted_iota(jnp.int32, (quad, quad), 0)
        c2 = lax.broadcasted_iota(jnp.int32, (quad, quad), 1)
        same_head = (r2 >> 6) == (c2 >> 6)
        bdm_s[...] = same_head.astype(F32).astype(BF16)
        for l in range(1, 6):
            lm = same_head & level_mask(r2 & (CHUNK - 1), c2 & (CHUNK - 1), l)
            lvl_s[l - 1] = lm.astype(F32).astype(BF16)

    lm0_f = level_mask(row_q, col_q, 0).astype(F32)
    zero_blk = jnp.zeros((CHUNK, HEAD_DIM), BF16)
    zero_sq = jnp.zeros((HEAD_DIM, HEAD_DIM), BF16)

    def head_bcast(cols, width):
        if width == HEAD_DIM:
            return jnp.concatenate([jnp.broadcast_to(c_, (CHUNK, HEAD_DIM)) for c_ in cols], axis=1)
        full = [jnp.broadcast_to(c_, (CHUNK, quad)) for c_ in cols]
        return jnp.where(head_q == 0, full[0],
                         jnp.where(head_q == 1, full[1], jnp.where(head_q == 2, full[2], full[3])))

    def bd_wide(a):
        rows = []
        for h in range(N_HEADS):
            rows.append(jnp.concatenate(
                [a[:, j * HEAD_DIM:(j + 1) * HEAD_DIM] if j == h else zero_blk for j in range(N_HEADS)],
                axis=1))
        return jnp.concatenate(rows, axis=0)

    def tile4(a):
        return jnp.concatenate([a] * N_HEADS, axis=0)

    carried = {"state": None}

    def group(gi):
        chunks = range(gi * GROUP_CHUNKS, (gi + 1) * GROUP_CHUNKS)
        rows = {c: slice(c * CHUNK, (c + 1) * CHUNK) for c in chunks}

        for n, c in enumerate(chunks):
            bt = gb_ref[rows[c], :]
            gcum = bt
            for s in (1, 2, 4, 8, 16, 32):
                gcum = gcum + jnp.where(row_l >= s, pltpu.roll(gcum, s, axis=0), 0.0)
            g_cols = [gcum[:, h:h + 1] for h in range(N_HEADS)]
            b_cols = [bt[:, N_HEADS + h:N_HEADS + h + 1] for h in range(N_HEADS)]
            g_i = head_bcast(g_cols, CHUNK)
            b_i = head_bcast(b_cols, CHUNK)
            gw_s[c] = head_bcast(g_cols, HEAD_DIM)
            g_j = jnp.sum(jnp.where(eye_q, g_i, 0.0), axis=0, keepdims=True)
            b_j = jnp.sum(jnp.where(eye_q, b_i, 0.0), axis=0, keepdims=True)
            rv_s[c, 0:1, :] = b_j
            rv_s[c, 1:2, :] = b_j * jnp.exp(g_j)
            decay = jnp.exp(jnp.where(tri_incl, g_i - g_j, -1e30))

            k16 = k_ref[rows[c], :]
            kq = jnp.concatenate([k16, q_ref[rows[c], :]], axis=0)
            kq_k = lax.dot_general(kq, bd_wide(k16), (((1,), (1,)), ((), ())),
                                   preferred_element_type=F32)
            m_st = jnp.where(tri_strict, kq_k[:CHUNK] * b_i * decay, 0.0)
            at_s[c] = (kq_k[CHUNK:] * decay).astype(BF16)
            xs_s[c] = eye_f - m_st * lm0_f
            mt_s[c] = m_st.astype(BF16)
            if n % 2 == 1:
                yield

        for l in range(1, 6):
            for c in chunks:
                x = xs_s[c]
                x16 = x.astype(BF16)
                y = jnp.dot(x16, tile4(mt_s[c]) * lvl_s[l - 1], preferred_element_type=F32)
                z = jnp.dot(y.astype(BF16), tile4(x16) * bdm_s[...], preferred_element_type=F32)
                xs_s[c] = x - z
            yield

        for n, c in enumerate(chunks):
            k16 = k_ref[rows[c], :]
            q = q_ref[rows[c], :].astype(F32)
            k = k16.astype(F32)
            x = xs_s[c]
            u = jnp.dot((x * rv_s[c, 0:1, :]).astype(BF16), bd_wide(v_ref[rows[c], :]),
                        preferred_element_type=F32)
            w = jnp.dot((x * rv_s[c, 1:2, :]).astype(BF16), bd_wide(k16), preferred_element_type=F32)
            u16 = u.astype(BF16)
            w16 = w.astype(BF16)
            a_wu = jnp.dot(at_s[c], jnp.concatenate([bd_wide(w16), bd_wide(u16)], axis=1),
                           preferred_element_type=F32)
            g_wide = gw_s[c]
            g_last = g_wide[CHUNK - 1:CHUNK, :]
            aq_s[c, 2 * CHUNK:3 * CHUNK, :] = (q * jnp.exp(g_wide) - a_wu[:, :D_DN]).astype(BF16)
            op_s[c] = a_wu[:, D_DN:]
            gam_s[c] = jnp.broadcast_to(jnp.exp(g_last), (8, D_DN))
            k_dec = (k * jnp.exp(g_last - g_wide)).astype(BF16)
            for h in range(N_HEADS):
                hs = slice(h * HEAD_DIM, (h + 1) * HEAD_DIM)
                ab_h = _mm_tn(k_dec[:, hs], jnp.concatenate([w16[:, hs], u16[:, hs]], axis=1))
                aq_s[c, 0:2 * CHUNK, hs] = ab_h[:, :HEAD_DIM].astype(BF16)
                bb_s[c, :, hs] = ab_h[:, HEAD_DIM:]
            if n % 2 == 1:
                yield

        state = state_s[...] if gi == 0 else carried["state"]
        for c in chunks:
            s16 = state.astype(BF16)
            new_parts = []
            o_parts_c = []
            for p in range(N_HEADS // 2):
                ps = slice(p * 2 * HEAD_DIM, (p + 1) * 2 * HEAD_DIM)
                s_a = s16[:, (2 * p) * HEAD_DIM:(2 * p + 1) * HEAD_DIM]
                s_b = s16[:, (2 * p + 1) * HEAD_DIM:(2 * p + 2) * HEAD_DIM]
                bd = jnp.concatenate([jnp.concatenate([s_a, zero_sq], axis=1),
                                      jnp.concatenate([zero_sq, s_b], axis=1)], axis=0)
                r = jnp.dot(aq_s[c, :, ps], bd, preferred_element_type=F32)
                new_parts.append(gam_s[c, 0:1, ps] * state[:, ps] + bb_s[c, :, ps] - r[:2 * CHUNK])
                o_parts_c.append(op_s[c, :, ps] + r[2 * CHUNK:])
            state = jnp.concatenate(new_parts, axis=1)
            o_s[rows[c], :] = jnp.concatenate(o_parts_c, axis=1)
        carried["state"] = state
        if gi == n_chunks // GROUP_CHUNKS - 1:
            state_s[...] = state
        yield

        rg = slice(gi * GROUP_CHUNKS * CHUNK, (gi + 1) * GROUP_CHUNKS * CHUNK)
        zg = proj_ref[rg, COL_ZG:COL_ZG + D_DN].astype(F32)
        o_parts = []
        for h in range(N_HEADS):
            hs = slice(h * HEAD_DIM, (h + 1) * HEAD_DIM)
            oh = o_s[rg, hs]
            o_parts.append(oh * _rms_scale(oh) * dnw_ref[...] * zg[:, hs])
        o_dn = jnp.concatenate(o_parts, axis=1)
        y = (jnp.dot(o_dn.astype(BF16), wout_ref[0:D_DN, :], preferred_element_type=F32)
             + jnp.dot(proj_ref[rg, COL_YSC:COL_YSC + D_SC], wout_ref[D_DN:D_MODEL, :],
                       preferred_element_type=F32))
        out_ref[rg, :] = x_ref[rg, :] + mod_ref[2:3, :] * (y * _rms_scale(y) * pw_ref[...])

    _interleave([group(gi) for gi in range(n_chunks // GROUP_CHUNKS)], lead=2)


def _mixer(proj, gb, x2, mod3, dnw, w_out, pw, bsz, seq):
    tq = TQ_MIX
    per_b = seq // tq
    t = x2.shape[0]

    def tile(b, s):
        return (b * per_b + s, 0)

    const = lambda b, s: (0, 0)
    return pl.pallas_call(
        _mix_kernel,
        grid=(bsz, per_b),
        in_specs=[
            pl.BlockSpec((tq, N_P2), tile),
            pl.BlockSpec((tq, LANES), tile),
            pl.BlockSpec((tq, D_MODEL), tile),
            pl.BlockSpec((None, 6, D_MODEL), lambda b, s: (b, 0, 0)),
            pl.BlockSpec((1, HEAD_DIM), const),
            pl.BlockSpec((D_MODEL, D_MODEL), const),
            pl.BlockSpec((1, D_MODEL), const),
        ],
        out_specs=pl.BlockSpec((tq, D_MODEL), tile),
        out_shape=jax.ShapeDtypeStruct((t, D_MODEL), F32),
        scratch_shapes=[
            pltpu.VMEM((tq, D_DN), F32),
            pltpu.VMEM((HEAD_DIM, D_DN), F32),
            pltpu.VMEM((tq // CHUNK, 3 * CHUNK, D_DN), BF16),
            pltpu.VMEM((tq // CHUNK, HEAD_DIM, D_DN), F32),
            pltpu.VMEM((tq // CHUNK, CHUNK, D_DN), F32),
            pltpu.VMEM((tq // CHUNK, 8, D_DN), F32),
            pltpu.VMEM((N_HEADS * CHUNK, N_HEADS * CHUNK), BF16),
            pltpu.VMEM((5, N_HEADS * CHUNK, N_HEADS * CHUNK), BF16),
            pltpu.VMEM((tq // CHUNK, CHUNK, N_HEADS * CHUNK), F32),
            pltpu.VMEM((tq // CHUNK, CHUNK, N_HEADS * CHUNK), BF16),
            pltpu.VMEM((tq // CHUNK, CHUNK, N_HEADS * CHUNK), BF16),
            pltpu.VMEM((tq // CHUNK, CHUNK, D_DN), F32),
            pltpu.VMEM((tq // CHUNK, 8, N_HEADS * CHUNK), F32),
        ],
        compiler_params=pltpu.CompilerParams(dimension_semantics=("arbitrary", "arbitrary"),
                                             vmem_limit_bytes=VMEM_LIMIT),
        name="token_mixer",
    )(proj, gb, x2, mod3, dnw, w_out, pw)


def _ffn_kernel(x_ref, mod_ref, nw_ref, w1_hbm, w2_hbm, pw_ref, out_ref, w1_ref, w2_ref, stage_s, sem):
    @pl.when(pl.program_id(0) == 0)
    def _():
        pieces = []
        for j in range(D_FF // FF_CHUNK):
            cs = slice(j * FF_CHUNK, (j + 1) * FF_CHUNK)
            pieces.append((w1_hbm.at[:, cs], w1_ref.at[:, cs]))
        for j in range(D_FF // FF_CHUNK):
            cs = slice(j * FF_CHUNK, (j + 1) * FF_CHUNK)
            pieces.append((w2_hbm.at[cs, :], w2_ref.at[cs, :]))

        def copy(n):
            return pltpu.make_async_copy(pieces[n][0], stage_s.at[n % 2], sem.at[n % 2])

        copy(0).start()
        for n in range(len(pieces)):
            if n + 1 < len(pieces):
                copy(n + 1).start()
            copy(n).wait()
            pieces[n][1][...] = stage_s[n % 2].astype(BF16)

    x = x_ref[...]
    h = x * _rms_scale(x) * nw_ref[...]
    h = h * (1.0 + mod_ref[4:5, :]) + mod_ref[3:4, :]
    hb = h.astype(BF16)
    y = jnp.zeros(x.shape, F32)
    for j in range(D_FF // FF_CHUNK):
        a = jnp.dot(hb, w1_ref[:, j * FF_CHUNK:(j + 1) * FF_CHUNK], preferred_element_type=F32)
        a = jnp.square(jnp.maximum(a, 0.0)).astype(BF16)
        y = y + jnp.dot(a, w2_ref[j * FF_CHUNK:(j + 1) * FF_CHUNK, :], preferred_element_type=F32)
    out_ref[...] = x + mod_ref[5:6, :] * (y * _rms_scale(y) * pw_ref[...])


def _ffn(x2, mod3, nw, w1, w2, pw, seq):
    t = x2.shape[0]
    tm = TM_FFN
    per_b = seq // tm
    return pl.pallas_call(
        _ffn_kernel,
        grid=(t // tm,),
        in_specs=[
            pl.BlockSpec((tm, D_MODEL), lambda i: (i, 0)),
            pl.BlockSpec((None, 6, D_MODEL), lambda i: (i // per_b, 0, 0)),
            pl.BlockSpec((1, D_MODEL), lambda i: (0, 0)),
            pl.BlockSpec(memory_space=pl.ANY),
            pl.BlockSpec(memory_space=pl.ANY),
            pl.BlockSpec((1, D_MODEL), lambda i: (0, 0)),
        ],
        out_specs=pl.BlockSpec((tm, D_MODEL), lambda i: (i, 0)),
        out_shape=jax.ShapeDtypeStruct((t, D_MODEL), F32),
        scratch_shapes=[
            pltpu.VMEM((D_MODEL, D_FF), BF16),
            pltpu.VMEM((D_FF, D_MODEL), BF16),
            pltpu.VMEM((2, FF_CHUNK, FF_CHUNK), F32),
            pltpu.SemaphoreType.DMA((2,)),
        ],
        compiler_params=pltpu.CompilerParams(dimension_semantics=("arbitrary",),
                                             vmem_limit_bytes=VMEM_LIMIT),
        name="ffn",
    )(x2, mod3, nw, w1, w2, pw)


def _layer(x2, c, w_ada, b_ada, pre_mix_w, post_mix_w, w_in, dn_conv_w, dn_a_log, dn_dt_bias,
           dn_norm_w, sc_conv_w, w_out, pre_ffn_w, post_ffn_w, w_ff1, w_ff2, bsz, seq):
    mod3 = _ada(c, w_ada, b_ada[None, :]).reshape(bsz, 6, D_MODEL)

    gp = jnp.zeros((8, LANES), F32)
    gp = gp.at[0, :N_HEADS].set(dn_a_log).at[1, :N_HEADS].set(dn_dt_bias)
    proj, gb = _in_proj(x2, mod3, pre_mix_w[None, :], w_in, dn_conv_w, sc_conv_w, gp, seq)

    x2 = _mixer(proj, gb, x2, mod3, dn_norm_w[None, :], w_out.astype(BF16), post_mix_w[None, :],
                bsz, seq)

    return _ffn(x2, mod3, pre_ffn_w[None, :], w_ff1, w_ff2, post_ffn_w[None, :], seq)


def kernel(x, c, w_ada, b_ada, pre_mix_norm_w, post_mix_norm_w, w_in, dn_conv_w, dn_a_log, dn_dt_bias, dn_norm_w, sc_conv_w, w_out, pre_ffn_norm_w, post_ffn_norm_w, w_ff1, w_ff2):
    bsz, seq, d = x.shape
    x2 = x.reshape(bsz * seq, d)
    for l in range(w_ada.shape[0]):
        x2 = _layer(x2, c, w_ada[l], b_ada[l], pre_mix_norm_w[l], post_mix_norm_w[l], w_in[l],
                    dn_conv_w[l], dn_a_log[l], dn_dt_bias[l], dn_norm_w[l], sc_conv_w[l], w_out[l],
                    pre_ffn_norm_w[l], post_ffn_norm_w[l], w_ff1[l], w_ff2[l], bsz, seq)
    return x2.reshape(bsz, seq, d)
```

```python
import functools

import jax
import jax.numpy as jnp
from jax import lax
from jax.experimental import pallas as pl
from jax.experimental.pallas import tpu as pltpu

F32 = jnp.float32
BF16 = jnp.bfloat16

D_MODEL = 1024
N_HEADS = 4
HEAD_DIM = 128
D_DN = N_HEADS * HEAD_DIM
D_SC = D_MODEL - D_DN
DN_CONV = 4
SC_CONV = 3
CHUNK = 64
D_FF = 4 * D_MODEL
EPS = 1e-6
N_QKV = 3 * D_DN
N_MAIN = 4 * D_DN + 3 * D_SC
COL_Z = N_QKV
COL_SB = 4 * D_DN
COL_SC = COL_SB + D_SC
COL_SH = COL_SC + D_SC
N_P2 = N_QKV + D_DN + D_SC
COL_ZG = N_QKV
COL_YSC = N_QKV + D_DN
LANES = 128
HALO = 8

TM_IN = 512
TQ_MIX = 512
GROUP_CHUNKS = 4
TM_FFN = 512
FF_CHUNK = 1024
VMEM_LIMIT = 56 * 1024 * 1024


def _sigmoid(x):
    return 0.5 + 0.5 * jnp.tanh(0.5 * x)


def _mm_tn(a, b):
    return lax.dot_general(a, b, (((0,), (0,)), ((), ())), preferred_element_type=F32)


def _rms_scale(x):
    return lax.rsqrt(jnp.mean(x * x, axis=-1, keepdims=True) + EPS)


def _interleave(gens, lead):
    live = []
    pending = list(gens)
    rnd = 0
    while live or pending:
        if pending and rnd % lead == 0:
            live.append(pending.pop(0))
        for g in list(live):
            try:
                next(g)
            except StopIteration:
                live.remove(g)
        rnd += 1


def _ada_kernel(c_ref, w_ref, b_ref, o_ref):
    c = c_ref[...]
    ca = (c * _sigmoid(c)).astype(BF16)
    o_ref[...] = jnp.dot(ca, w_ref[...].astype(BF16), preferred_element_type=F32) + b_ref[...]


def _ada(c, w_ada, b_ada):
    bsz = c.shape[0]
    n = w_ada.shape[1]
    bn = 1024
    return pl.pallas_call(
        _ada_kernel,
        grid=(n // bn,),
        in_specs=[
            pl.BlockSpec((bsz, D_MODEL), lambda j: (0, 0)),
            pl.BlockSpec((D_MODEL, bn), lambda j: (0, j)),
            pl.BlockSpec((1, bn), lambda j: (0, j)),
        ],
        out_specs=pl.BlockSpec((bsz, bn), lambda j: (0, j)),
        out_shape=jax.ShapeDtypeStruct((bsz, n), F32),
        compiler_params=pltpu.CompilerParams(dimension_semantics=("arbitrary",),
                                             vmem_limit_bytes=VMEM_LIMIT),
        name="ada_mod",
    )(c, w_ada, b_ada)


def _causal_conv(ext, w_ref, c0, width):
    r = ext.shape[0] - HALO
    n = ext.shape[1]
    acc = w_ref[width - 1:width, c0:c0 + n] * ext[HALO:HALO + r, :]
    for s in range(1, width):
        acc = acc + w_ref[width - 1 - s:width - s, c0:c0 + n] * pltpu.roll(ext, s, axis=0)[HALO:HALO + r, :]
    return acc


def _in_half(hf, x_ref, mod_ref, nw_ref, wm_ref, wab_ref, cw_ref, scw_ref, gp_ref,
             proj_ref, gb_ref, e_s, p_s):
    half = x_ref.shape[0] // 2
    r0 = hf * half
    rows = slice(r0, r0 + half)
    erows = slice(HALO + r0, HALO + r0 + half)
    wrows = slice(r0, r0 + half + HALO)

    x = x_ref[rows, :]
    h = x * _rms_scale(x) * nw_ref[...]
    h = h * (1.0 + mod_ref[1:2, :]) + mod_ref[0:1, :]
    hb = h.astype(BF16)
    yield

    def nt(w):
        return lax.dot_general(hb, w, (((1,), (1,)), ((), ())), preferred_element_type=F32)

    def proj(c0, n=D_DN):
        return nt(wm_ref[c0:c0 + n, :])

    for part in range(3):
        e_s[part, erows, :] = proj(part * D_DN)
        yield
    e_s[3, erows, :] = proj(COL_SC)
    p_s[0, rows, :] = proj(COL_Z)
    yield
    p_s[1, rows, :] = proj(COL_SH)
    p_s[2, rows, :] = proj(COL_SB)
    ab = nt(wab_ref[...])
    yield

    for part in range(3):
        c0 = part * D_DN
        acc = _causal_conv(e_s[part, wrows, :], cw_ref, c0, DN_CONV)
        y = acc * _sigmoid(acc)
        if part < 2:
            parts = []
            for hd in range(N_HEADS):
                yh = y[:, hd * HEAD_DIM:(hd + 1) * HEAD_DIM]
                nrm = lax.rsqrt(jnp.sum(yh * yh, axis=-1, keepdims=True) + EPS)
                if part == 0:
                    nrm = nrm * (HEAD_DIM ** -0.5)
                parts.append(yh * nrm)
            y = jnp.concatenate(parts, axis=1)
        proj_ref[rows, c0:c0 + D_DN] = y.astype(BF16)
        yield

    p_z = p_s[0, rows, :]
    proj_ref[rows, COL_ZG:COL_ZG + D_DN] = (p_z * _sigmoid(p_z)).astype(BF16)
    e_s[3, erows, :] = e_s[3, erows, :] * p_s[1, rows, :]
    conv = _causal_conv(e_s[3, wrows, :], scw_ref, 0, SC_CONV)
    proj_ref[rows, COL_YSC:COL_YSC + D_SC] = (p_s[2, rows, :] * conv).astype(BF16)

    xg = ab + gp_ref[1:2, :]
    softplus = jnp.maximum(xg, 0.0) + jnp.log1p(jnp.exp(-jnp.abs(xg)))
    lane = lax.broadcasted_iota(jnp.int32, ab.shape, 1)
    gb_ref[rows, :] = jnp.where(lane < N_HEADS, -jnp.exp(gp_ref[0:1, :]) * softplus, _sigmoid(ab))


def _in_kernel(x_ref, mod_ref, nw_ref, win_ref, cw_ref, scw_ref, gp_ref,
               proj_ref, gb_ref, e_s, p_s, wm_ref, wab_ref, *, per_b):
    tm = x_ref.shape[0]

    @pl.when(pl.program_id(0) == 0)
    def _():
        n_gate = 2 * N_HEADS
        rb = D_DN
        for r in range(N_MAIN // rb):
            src = r * rb + (n_gate if r * rb >= COL_SB else 0)
            wm_ref[r * rb:(r + 1) * rb, :] = win_ref[src:src + rb, :].astype(BF16)
        wab_ref[...] = win_ref[COL_SB:COL_SB + LANES, :].astype(BF16)

    @pl.when(pl.program_id(0) % per_b == 0)
    def _():
        e_s[:, 0:HALO, :] = jnp.zeros((e_s.shape[0], HALO, D_DN), F32)

    _interleave([_in_half(hf, x_ref, mod_ref, nw_ref, wm_ref, wab_ref, cw_ref, scw_ref, gp_ref,
                          proj_ref, gb_ref, e_s, p_s) for hf in range(2)], lead=5)
    e_s[:, 0:HALO, :] = e_s[:, tm:tm + HALO, :]


def _in_proj(x2, mod3, nw, w_in, cw, scw, gp, seq):
    t = x2.shape[0]
    tm = TM_IN
    per_b = seq // tm
    const = lambda i: (0, 0)
    return pl.pallas_call(
        functools.partial(_in_kernel, per_b=per_b),
        grid=(t // tm,),
        in_specs=[
            pl.BlockSpec((tm, D_MODEL), lambda i: (i, 0)),
            pl.BlockSpec((None, 6, D_MODEL), lambda i: (i // per_b, 0, 0)),
            pl.BlockSpec((1, D_MODEL), const),
            pl.BlockSpec(w_in.shape, const, pipeline_mode=pl.Buffered(1)),
            pl.BlockSpec((DN_CONV, N_QKV), const),
            pl.BlockSpec((SC_CONV, D_SC), const),
            pl.BlockSpec((8, LANES), const),
        ],
        out_specs=[
            pl.BlockSpec((tm, N_P2), lambda i: (i, 0)),
            pl.BlockSpec((tm, LANES), lambda i: (i, 0)),
        ],
        out_shape=[
            jax.ShapeDtypeStruct((t, N_P2), BF16),
            jax.ShapeDtypeStruct((t, LANES), F32),
        ],
        scratch_shapes=[
            pltpu.VMEM((4, HALO + tm, D_DN), F32),
            pltpu.VMEM((3, tm, D_DN), F32),
            pltpu.VMEM((N_MAIN, D_MODEL), BF16),
            pltpu.VMEM((LANES, D_MODEL), BF16),
        ],
        compiler_params=pltpu.CompilerParams(dimension_semantics=("arbitrary",),
                                             vmem_limit_bytes=VMEM_LIMIT),
        name="in_proj",
    )(x2, mod3, nw, w_in, cw, scw, gp)


def _mix_kernel(proj_ref, gb_ref, x_ref, mod_ref, dnw_ref, wout_ref, pw_ref, out_ref,
                o_s, state_s, aq_s, bb_s, op_s, gam_s, bdm_s, lvl_s, xs_s, mt_s, at_s, gw_s, rv_s):
    tq = x_ref.shape[0]
    t = pl.program_id(1)

    @pl.when(t == 0)
    def _():
        state_s[...] = jnp.zeros_like(state_s)

    q_ref = proj_ref.at[:, 0:D_DN]
    k_ref = proj_ref.at[:, D_DN:2 * D_DN]
    v_ref = proj_ref.at[:, 2 * D_DN:3 * D_DN]

    n_chunks = tq // CHUNK
    quad = N_HEADS * CHUNK
    row_q = lax.broadcasted_iota(jnp.int32, (CHUNK, quad), 0)
    lane_q = lax.broadcasted_iota(jnp.int32, (CHUNK, quad), 1)
    col_q = lane_q & (CHUNK - 1)
    head_q = lane_q >> 6
    tri_incl = row_q >= col_q
    tri_strict = row_q > col_q
    eye_q = row_q == col_q
    eye_f = eye_q.astype(F32)
    row_l = lax.broadcasted_iota(jnp.int32, (CHUNK, LANES), 0)

    def level_mask(r, c, l):
        return (((r >> l) & 1) == 1) & (((c >> l) & 1) == 0) & ((r >> (l + 1)) == (c >> (l + 1)))

    @pl.when((pl.program_id(0) == 0) & (t == 0))
    def _():
        r2 = lax.broadcasted_iota(jnp.int32, (quad, quad), 0)
        c2 = lax.broadcasted_iota(jnp.int32, (quad, quad), 1)
        same_head = (r2 >> 6) == (c2 >> 6)
        bdm_s[...] = same_head.astype(F32).astype(BF16)
        for l in range(1, 6):
            lm = same_head & level_mask(r2 & (CHUNK - 1), c2 & (CHUNK - 1), l)
            lvl_s[l - 1] = lm.astype(F32).astype(BF16)

    lm0_f = level_mask(row_q, col_q, 0).astype(F32)
    zero_blk = jnp.zeros((CHUNK, HEAD_DIM), BF16)
    zero_sq = jnp.zeros((HEAD_DIM, HEAD_DIM), BF16)

    def head_bcast(cols, width):
        if width == HEAD_DIM:
            return jnp.concatenate([jnp.broadcast_to(c_, (CHUNK, HEAD_DIM)) for c_ in cols], axis=1)
        full = [jnp.broadcast_to(c_, (CHUNK, quad)) for c_ in cols]
        return jnp.where(head_q == 0, full[0],
                         jnp.where(head_q == 1, full[1], jnp.where(head_q == 2, full[2], full[3])))

    def bd_wide(a):
        rows = []
        for h in range(N_HEADS):
            rows.append(jnp.concatenate(
                [a[:, j * HEAD_DIM:(j + 1) * HEAD_DIM] if j == h else zero_blk for j in range(N_HEADS)],
                axis=1))
        return jnp.concatenate(rows, axis=0)

    def tile4(a):
        return jnp.concatenate([a] * N_HEADS, axis=0)

    carried = {"state": None}

    def group(gi):
        chunks = range(gi * GROUP_CHUNKS, (gi + 1) * GROUP_CHUNKS)
        rows = {c: slice(c * CHUNK, (c + 1) * CHUNK) for c in chunks}

        for n, c in enumerate(chunks):
            bt = gb_ref[rows[c], :]
            gcum = bt
            for s in (1, 2, 4, 8, 16, 32):
                gcum = gcum + jnp.where(row_l >= s, pltpu.roll(gcum, s, axis=0), 0.0)
            g_cols = [gcum[:, h:h + 1] for h in range(N_HEADS)]
            b_cols = [bt[:, N_HEADS + h:N_HEADS + h + 1] for h in range(N_HEADS)]
            g_i = head_bcast(g_cols, CHUNK)
            b_i = head_bcast(b_cols, CHUNK)
            gw_s[c] = head_bcast(g_cols, HEAD_DIM)
            g_j = jnp.sum(jnp.where(eye_q, g_i, 0.0), axis=0, keepdims=True)
            b_j = jnp.sum(jnp.where(eye_q, b_i, 0.0), axis=0, keepdims=True)
            rv_s[c, 0:1, :] = b_j
            rv_s[c, 1:2, :] = b_j * jnp.exp(g_j)
            decay = jnp.exp(jnp.where(tri_incl, g_i - g_j, -1e30))

            k16 = k_ref[rows[c], :]
            kq = jnp.concatenate([k16, q_ref[rows[c], :]], axis=0)
            kq_k = lax.dot_general(kq, bd_wide(k16), (((1,), (1,)), ((), ())),
                                   preferred_element_type=F32)
            m_st = jnp.where(tri_strict, kq_k[:CHUNK] * b_i * decay, 0.0)
            at_s[c] = (kq_k[CHUNK:] * decay).astype(BF16)
            xs_s[c] = eye_f - m_st * lm0_f
            mt_s[c] = m_st.astype(BF16)
            if n % 2 == 1:
                yield

        for l in range(1, 6):
            for c in chunks:
                x = xs_s[c]
                x16 = x.astype(BF16)
                y = jnp.dot(x16, tile4(mt_s[c]) * lvl_s[l - 1], preferred_element_type=F32)
                z = jnp.dot(y.astype(BF16), tile4(x16) * bdm_s[...], preferred_element_type=F32)
                xs_s[c] = x - z
            yield

        for n, c in enumerate(chunks):
            k16 = k_ref[rows[c], :]
            q = q_ref[rows[c], :].astype(F32)
            k = k16.astype(F32)
            x = xs_s[c]
            u = jnp.dot((x * rv_s[c, 0:1, :]).astype(BF16), bd_wide(v_ref[rows[c], :]),
                        preferred_element_type=F32)
            w = jnp.dot((x * rv_s[c, 1:2, :]).astype(BF16), bd_wide(k16), preferred_element_type=F32)
            u16 = u.astype(BF16)
            w16 = w.astype(BF16)
            a_wu = jnp.dot(at_s[c], jnp.concatenate([bd_wide(w16), bd_wide(u16)], axis=1),
                           preferred_element_type=F32)
            g_wide = gw_s[c]
            g_last = g_wide[CHUNK - 1:CHUNK, :]
            aq_s[c, 2 * CHUNK:3 * CHUNK, :] = (q * jnp.exp(g_wide) - a_wu[:, :D_DN]).astype(BF16)
            op_s[c] = a_wu[:, D_DN:]
            gam_s[c] = jnp.broadcast_to(jnp.exp(g_last), (8, D_DN))
            k_dec = (k * jnp.exp(g_last - g_wide)).astype(BF16)
            for h in range(N_HEADS):
                hs = slice(h * HEAD_DIM, (h + 1) * HEAD_DIM)
                ab_h = _mm_tn(k_dec[:, hs], jnp.concatenate([w16[:, hs], u16[:, hs]], axis=1))
                aq_s[c, 0:2 * CHUNK, hs] = ab_h[:, :HEAD_DIM].astype(BF16)
                bb_s[c, :, hs] = ab_h[:, HEAD_DIM:]
            if n % 2 == 1:
                yield

        state = state_s[...] if gi == 0 else carried["state"]
        for c in chunks:
            s16 = state.astype(BF16)
            new_parts = []
            o_parts_c = []
            for p in range(N_HEADS // 2):
                ps = slice(p * 2 * HEAD_DIM, (p + 1) * 2 * HEAD_DIM)
                s_a = s16[:, (2 * p) * HEAD_DIM:(2 * p + 1) * HEAD_DIM]
                s_b = s16[:, (2 * p + 1) * HEAD_DIM:(2 * p + 2) * HEAD_DIM]
                bd = jnp.concatenate([jnp.concatenate([s_a, zero_sq], axis=1),
                                      jnp.concatenate([zero_sq, s_b], axis=1)], axis=0)
                r = jnp.dot(aq_s[c, :, ps], bd, preferred_element_type=F32)
                new_parts.append(gam_s[c, 0:1, ps] * state[:, ps] + bb_s[c, :, ps] - r[:2 * CHUNK])
                o_parts_c.append(op_s[c, :, ps] + r[2 * CHUNK:])
            state = jnp.concatenate(new_parts, axis=1)
            o_s[rows[c], :] = jnp.concatenate(o_parts_c, axis=1)
        carried["state"] = state
        if gi == n_chunks // GROUP_CHUNKS - 1:
            state_s[...] = state
        yield

        rg = slice(gi * GROUP_CHUNKS * CHUNK, (gi + 1) * GROUP_CHUNKS * CHUNK)
        zg = proj_ref[rg, COL_ZG:COL_ZG + D_DN].astype(F32)
        o_parts = []
        for h in range(N_HEADS):
            hs = slice(h * HEAD_DIM, (h + 1) * HEAD_DIM)
            oh = o_s[rg, hs]
            o_parts.append(oh * _rms_scale(oh) * dnw_ref[...] * zg[:, hs])
        o_dn = jnp.concatenate(o_parts, axis=1)
        y = (jnp.dot(o_dn.astype(BF16), wout_ref[0:D_DN, :], preferred_element_type=F32)
             + jnp.dot(proj_ref[rg, COL_YSC:COL_YSC + D_SC], wout_ref[D_DN:D_MODEL, :],
                       preferred_element_type=F32))
        out_ref[rg, :] = x_ref[rg, :] + mod_ref[2:3, :] * (y * _rms_scale(y) * pw_ref[...])

    _interleave([group(gi) for gi in range(n_chunks // GROUP_CHUNKS)], lead=2)


def _mixer(proj, gb, x2, mod3, dnw, w_out, pw, bsz, seq):
    tq = TQ_MIX
    per_b = seq // tq
    t = x2.shape[0]

    def tile(b, s):
        return (b * per_b + s, 0)

    const = lambda b, s: (0, 0)
    return pl.pallas_call(
        _mix_kernel,
        grid=(bsz, per_b),
        in_specs=[
            pl.BlockSpec((tq, N_P2), tile),
            pl.BlockSpec((tq, LANES), tile),
            pl.BlockSpec((tq, D_MODEL), tile),
            pl.BlockSpec((None, 6, D_MODEL), lambda b, s: (b, 0, 0)),
            pl.BlockSpec((1, HEAD_DIM), const),
            pl.BlockSpec((D_MODEL, D_MODEL), const),
            pl.BlockSpec((1, D_MODEL), const),
        ],
        out_specs=pl.BlockSpec((tq, D_MODEL), tile),
        out_shape=jax.ShapeDtypeStruct((t, D_MODEL), F32),
        scratch_shapes=[
            pltpu.VMEM((tq, D_DN), F32),
            pltpu.VMEM((HEAD_DIM, D_DN), F32),
            pltpu.VMEM((tq // CHUNK, 3 * CHUNK, D_DN), BF16),
            pltpu.VMEM((tq // CHUNK, HEAD_DIM, D_DN), F32),
            pltpu.VMEM((tq // CHUNK, CHUNK, D_DN), F32),
            pltpu.VMEM((tq // CHUNK, 8, D_DN), F32),
            pltpu.VMEM((N_HEADS * CHUNK, N_HEADS * CHUNK), BF16),
            pltpu.VMEM((5, N_HEADS * CHUNK, N_HEADS * CHUNK), BF16),
            pltpu.VMEM((tq // CHUNK, CHUNK, N_HEADS * CHUNK), F32),
            pltpu.VMEM((tq // CHUNK, CHUNK, N_HEADS * CHUNK), BF16),
            pltpu.VMEM((tq // CHUNK, CHUNK, N_HEADS * CHUNK), BF16),
            pltpu.VMEM((tq // CHUNK, CHUNK, D_DN), F32),
            pltpu.VMEM((tq // CHUNK, 8, N_HEADS * CHUNK), F32),
        ],
        compiler_params=pltpu.CompilerParams(dimension_semantics=("arbitrary", "arbitrary"),
                                             vmem_limit_bytes=VMEM_LIMIT),
        name="token_mixer",
    )(proj, gb, x2, mod3, dnw, w_out, pw)


def _ffn_kernel(x_ref, mod_ref, nw_ref, w1_hbm, w2_hbm, pw_ref, out_ref, w1_ref, w2_ref, stage_s, sem):
    @pl.when(pl.program_id(0) == 0)
    def _():
        pieces = []
        for j in range(D_FF // FF_CHUNK):
            cs = slice(j * FF_CHUNK, (j + 1) * FF_CHUNK)
            pieces.append((w1_hbm.at[:, cs], w1_ref.at[:, cs]))
        for j in range(D_FF // FF_CHUNK):
            cs = slice(j * FF_CHUNK, (j + 1) * FF_CHUNK)
            pieces.append((w2_hbm.at[cs, :], w2_ref.at[cs, :]))

        def copy(n):
            return pltpu.make_async_copy(pieces[n][0], stage_s.at[n % 2], sem.at[n % 2])

        copy(0).start()
        for n in range(len(pieces)):
            if n + 1 < len(pieces):
                copy(n + 1).start()
            copy(n).wait()
            pieces[n][1][...] = stage_s[n % 2].astype(BF16)

    x = x_ref[...]
    h = x * _rms_scale(x) * nw_ref[...]
    h = h * (1.0 + mod_ref[4:5, :]) + mod_ref[3:4, :]
    hb = h.astype(BF16)
    y = jnp.zeros(x.shape, F32)
    for j in range(D_FF // FF_CHUNK):
        a = jnp.dot(hb, w1_ref[:, j * FF_CHUNK:(j + 1) * FF_CHUNK], preferred_element_type=F32)
        a = jnp.square(jnp.maximum(a, 0.0)).astype(BF16)
        y = y + jnp.dot(a, w2_ref[j * FF_CHUNK:(j + 1) * FF_CHUNK, :], preferred_element_type=F32)
    out_ref[...] = x + mod_ref[5:6, :] * (y * _rms_scale(y) * pw_ref[...])


def _ffn(x2, mod3, nw, w1, w2, pw, seq):
    t = x2.shape[0]
    tm = TM_FFN
    per_b = seq // tm
    return pl.pallas_call(
        _ffn_kernel,
        grid=(t // tm,),
        in_specs=[
            pl.BlockSpec((tm, D_MODEL), lambda i: (i, 0)),
            pl.BlockSpec((None, 6, D_MODEL), lambda i: (i // per_b, 0, 0)),
            pl.BlockSpec((1, D_MODEL), lambda i: (0, 0)),
            pl.BlockSpec(memory_space=pl.ANY),
            pl.BlockSpec(memory_space=pl.ANY),
            pl.BlockSpec((1, D_MODEL), lambda i: (0, 0)),
        ],
        out_specs=pl.BlockSpec((tm, D_MODEL), lambda i: (i, 0)),
        out_shape=jax.ShapeDtypeStruct((t, D_MODEL), F32),
        scratch_shapes=[
            pltpu.VMEM((D_MODEL, D_FF), BF16),
            pltpu.VMEM((D_FF, D_MODEL), BF16),
            pltpu.VMEM((2, FF_CHUNK, FF_CHUNK), F32),
            pltpu.SemaphoreType.DMA((2,)),
        ],
        compiler_params=pltpu.CompilerParams(dimension_semantics=("arbitrary",),
                                             vmem_limit_bytes=VMEM_LIMIT),
        name="ffn",
    )(x2, mod3, nw, w1, w2, pw)


def _layer(x2, c, w_ada, b_ada, pre_mix_w, post_mix_w, w_in, dn_conv_w, dn_a_log, dn_dt_bias,
           dn_norm_w, sc_conv_w, w_out, pre_ffn_w, post_ffn_w, w_ff1, w_ff2, bsz, seq):
    mod3 = _ada(c, w_ada, b_ada[None, :]).reshape(bsz, 6, D_MODEL)

    gp = jnp.zeros((8, LANES), F32)
    gp = gp.at[0, :N_HEADS].set(dn_a_log).at[1, :N_HEADS].set(dn_dt_bias)
    proj, gb = _in_proj(x2, mod3, pre_mix_w[None, :], w_in.T, dn_conv_w, sc_conv_w, gp, seq)

    x2 = _mixer(proj, gb, x2, mod3, dn_norm_w[None, :], w_out.astype(BF16), post_mix_w[None, :],
                bsz, seq)

    return _ffn(x2, mod3, pre_ffn_w[None, :], w_ff1, w_ff2, post_ffn_w[None, :], seq)


def kernel(x, c, w_ada, b_ada, pre_mix_norm_w, post_mix_norm_w, w_in, dn_conv_w, dn_a_log, dn_dt_bias, dn_norm_w, sc_conv_w, w_out, pre_ffn_norm_w, post_ffn_norm_w, w_ff1, w_ff2):
    bsz, seq, d = x.shape
    x2 = x.reshape(bsz * seq, d)
    for l in range(w_ada.shape[0]):
        x2 = _layer(x2, c, w_ada[l], b_ada[l], pre_mix_norm_w[l], post_mix_norm_w[l], w_in[l],
                    dn_conv_w[l], dn_a_log[l], dn_dt_bias[l], dn_norm_w[l], sc_conv_w[l], w_out[l],
                    pre_ffn_norm_w[l], post_ffn_norm_w[l], w_ff1[l], w_ff2[l], bsz, seq)
    return x2.reshape(bsz, seq, d)
```

```python
import functools

import jax
import jax.numpy as jnp
from jax import lax
from jax.experimental import pallas as pl
from jax.experimental.pallas import tpu as pltpu

F32 = jnp.float32
BF16 = jnp.bfloat16

D_MODEL = 1024
N_HEADS = 4
HEAD_DIM = 128
D_DN = N_HEADS * HEAD_DIM
D_SC = D_MODEL - D_DN
DN_CONV = 4
SC_CONV = 3
CHUNK = 64
D_FF = 4 * D_MODEL
EPS = 1e-6
N_QKV = 3 * D_DN
N_MAIN = 4 * D_DN + 3 * D_SC
COL_Z = N_QKV
COL_SB = 4 * D_DN
COL_SC = COL_SB + D_SC
COL_SH = COL_SC + D_SC
N_P2 = N_QKV + D_DN + D_SC
COL_ZG = N_QKV
COL_YSC = N_QKV + D_DN
LANES = 128
HALO = 8

TM_IN = 512
TQ_MIX = 512
GROUP_CHUNKS = 4
TM_FFN = 512
FF_CHUNK = 1024
VMEM_LIMIT = 56 * 1024 * 1024


def _sigmoid(x):
    return 0.5 + 0.5 * jnp.tanh(0.5 * x)


def _mm_tn(a, b):
    return lax.dot_general(a, b, (((0,), (0,)), ((), ())), preferred_element_type=F32)


def _rms_scale(x):
    return lax.rsqrt(jnp.mean(x * x, axis=-1, keepdims=True) + EPS)


def _interleave(gens, lead):
    live = []
    pending = list(gens)
    rnd = 0
    while live or pending:
        if pending and rnd % lead == 0:
            live.append(pending.pop(0))
        for g in list(live):
            try:
                next(g)
            except StopIteration:
                live.remove(g)
        rnd += 1


def _ada_kernel(c_ref, w_ref, b_ref, o_ref):
    c = c_ref[...]
    ca = (c * _sigmoid(c)).astype(BF16)
    o_ref[...] = jnp.dot(ca, w_ref[...].astype(BF16), preferred_element_type=F32) + b_ref[...]


def _ada(c, w_ada, b_ada):
    bsz = c.shape[0]
    n = w_ada.shape[1]
    bn = 1024
    return pl.pallas_call(
        _ada_kernel,
        grid=(n // bn,),
        in_specs=[
            pl.BlockSpec((bsz, D_MODEL), lambda j: (0, 0)),
            pl.BlockSpec((D_MODEL, bn), lambda j: (0, j)),
            pl.BlockSpec((1, bn), lambda j: (0, j)),
        ],
        out_specs=pl.BlockSpec((bsz, bn), lambda j: (0, j)),
        out_shape=jax.ShapeDtypeStruct((bsz, n), F32),
        compiler_params=pltpu.CompilerParams(dimension_semantics=("arbitrary",),
                                             vmem_limit_bytes=VMEM_LIMIT),
        name="ada_mod",
    )(c, w_ada, b_ada)


def _causal_conv(ext, w_ref, c0, width):
    r = ext.shape[0] - HALO
    n = ext.shape[1]
    acc = w_ref[width - 1:width, c0:c0 + n] * ext[HALO:HALO + r, :]
    for s in range(1, width):
        acc = acc + w_ref[width - 1 - s:width - s, c0:c0 + n] * pltpu.roll(ext, s, axis=0)[HALO:HALO + r, :]
    return acc


def _in_half(hf, x_ref, mod_ref, nw_ref, wm_ref, wab_ref, cw_ref, scw_ref, gp_ref,
             proj_ref, gb_ref, e_s, p_s):
    half = x_ref.shape[0] // 2
    r0 = hf * half
    rows = slice(r0, r0 + half)
    erows = slice(HALO + r0, HALO + r0 + half)
    wrows = slice(r0, r0 + half + HALO)

    x = x_ref[rows, :]
    h = x * _rms_scale(x) * nw_ref[...]
    h = h * (1.0 + mod_ref[1:2, :]) + mod_ref[0:1, :]
    hb = h.astype(BF16)
    yield

    def proj(c0, n=D_DN):
        return jnp.dot(hb, wm_ref[:, c0:c0 + n], preferred_element_type=F32)

    for part in range(3):
        e_s[part, erows, :] = proj(part * D_DN)
        yield
    e_s[3, erows, :] = proj(COL_SC)
    p_s[0, rows, :] = proj(COL_Z)
    yield
    p_s[1, rows, :] = proj(COL_SH)
    p_s[2, rows, :] = proj(COL_SB)
    ab = jnp.dot(hb, wab_ref[...], preferred_element_type=F32)
    yield

    for part in range(3):
        c0 = part * D_DN
        acc = _causal_conv(e_s[part, wrows, :], cw_ref, c0, DN_CONV)
        y = acc * _sigmoid(acc)
        if part < 2:
            parts = []
            for hd in range(N_HEADS):
                yh = y[:, hd * HEAD_DIM:(hd + 1) * HEAD_DIM]
                nrm = lax.rsqrt(jnp.sum(yh * yh, axis=-1, keepdims=True) + EPS)
                if part == 0:
                    nrm = nrm * (HEAD_DIM ** -0.5)
                parts.append(yh * nrm)
            y = jnp.concatenate(parts, axis=1)
        proj_ref[rows, c0:c0 + D_DN] = y.astype(BF16)
        yield

    p_z = p_s[0, rows, :]
    proj_ref[rows, COL_ZG:COL_ZG + D_DN] = (p_z * _sigmoid(p_z)).astype(BF16)
    e_s[3, erows, :] = e_s[3, erows, :] * p_s[1, rows, :]
    conv = _causal_conv(e_s[3, wrows, :], scw_ref, 0, SC_CONV)
    proj_ref[rows, COL_YSC:COL_YSC + D_SC] = (p_s[2, rows, :] * conv).astype(BF16)

    xg = ab + gp_ref[1:2, :]
    softplus = jnp.maximum(xg, 0.0) + jnp.log1p(jnp.exp(-jnp.abs(xg)))
    lane = lax.broadcasted_iota(jnp.int32, ab.shape, 1)
    gb_ref[rows, :] = jnp.where(lane < N_HEADS, -jnp.exp(gp_ref[0:1, :]) * softplus, _sigmoid(ab))


def _in_kernel(x_ref, mod_ref, nw_ref, win_ref, cw_ref, scw_ref, gp_ref,
               proj_ref, gb_ref, e_s, p_s, wm_ref, wab_ref, *, per_b):
    tm = x_ref.shape[0]

    @pl.when(pl.program_id(0) == 0)
    def _():
        n_gate = 2 * N_HEADS
        cb = D_DN // 2
        for r in range(N_MAIN // cb):
            src = r * cb + (n_gate if r * cb >= COL_SB else 0)
            wm_ref[:, r * cb:(r + 1) * cb] = win_ref[src:src + cb, :].T.astype(BF16)
        wab_ref[...] = win_ref[COL_SB:COL_SB + LANES, :].T.astype(BF16)

    @pl.when(pl.program_id(0) % per_b == 0)
    def _():
        e_s[:, 0:HALO, :] = jnp.zeros((e_s.shape[0], HALO, D_DN), F32)

    _interleave([_in_half(hf, x_ref, mod_ref, nw_ref, wm_ref, wab_ref, cw_ref, scw_ref, gp_ref,
                          proj_ref, gb_ref, e_s, p_s) for hf in range(2)], lead=5)
    e_s[:, 0:HALO, :] = e_s[:, tm:tm + HALO, :]


def _in_proj(x2, mod3, nw, w_in, cw, scw, gp, seq):
    t = x2.shape[0]
    tm = TM_IN
    per_b = seq // tm
    const = lambda i: (0, 0)
    return pl.pallas_call(
        functools.partial(_in_kernel, per_b=per_b),
        grid=(t // tm,),
        in_specs=[
            pl.BlockSpec((tm, D_MODEL), lambda i: (i, 0)),
            pl.BlockSpec((None, 6, D_MODEL), lambda i: (i // per_b, 0, 0)),
            pl.BlockSpec((1, D_MODEL), const),
            pl.BlockSpec(w_in.shape, const, pipeline_mode=pl.Buffered(1)),
            pl.BlockSpec((DN_CONV, N_QKV), const),
            pl.BlockSpec((SC_CONV, D_SC), const),
            pl.BlockSpec((8, LANES), const),
        ],
        out_specs=[
            pl.BlockSpec((tm, N_P2), lambda i: (i, 0)),
            pl.BlockSpec((tm, LANES), lambda i: (i, 0)),
        ],
        out_shape=[
            jax.ShapeDtypeStruct((t, N_P2), BF16),
            jax.ShapeDtypeStruct((t, LANES), F32),
        ],
        scratch_shapes=[
            pltpu.VMEM((4, HALO + tm, D_DN), F32),
            pltpu.VMEM((3, tm, D_DN), F32),
            pltpu.VMEM((D_MODEL, N_MAIN), BF16),
            pltpu.VMEM((D_MODEL, LANES), BF16),
        ],
        compiler_params=pltpu.CompilerParams(dimension_semantics=("arbitrary",),
                                             vmem_limit_bytes=VMEM_LIMIT),
        name="in_proj",
    )(x2, mod3, nw, w_in, cw, scw, gp)


def _mix_kernel(proj_ref, gb_ref, x_ref, mod_ref, dnw_ref, wout_ref, pw_ref, out_ref,
                o_s, state_s, aq_s, bb_s, op_s, gam_s, bdm_s, lvl_s, xs_s, mt_s, at_s, gw_s, rv_s):
    tq = x_ref.shape[0]
    t = pl.program_id(1)

    @pl.when(t == 0)
    def _():
        state_s[...] = jnp.zeros_like(state_s)

    q_ref = proj_ref.at[:, 0:D_DN]
    k_ref = proj_ref.at[:, D_DN:2 * D_DN]
    v_ref = proj_ref.at[:, 2 * D_DN:3 * D_DN]

    n_chunks = tq // CHUNK
    quad = N_HEADS * CHUNK
    row_q = lax.broadcasted_iota(jnp.int32, (CHUNK, quad), 0)
    lane_q = lax.broadcasted_iota(jnp.int32, (CHUNK, quad), 1)
    col_q = lane_q & (CHUNK - 1)
    head_q = lane_q >> 6
    tri_incl = row_q >= col_q
    tri_strict = row_q > col_q
    eye_q = row_q == col_q
    eye_f = eye_q.astype(F32)
    row_l = lax.broadcasted_iota(jnp.int32, (CHUNK, LANES), 0)

    def level_mask(r, c, l):
        return (((r >> l) & 1) == 1) & (((c >> l) & 1) == 0) & ((r >> (l + 1)) == (c >> (l + 1)))

    @pl.when((pl.program_id(0) == 0) & (t == 0))
    def _():
        r2 = lax.broadcasted_iota(jnp.int32, (quad, quad), 0)
        c2 = lax.broadcasted_iota(jnp.int32, (quad, quad), 1)
        same_head = (r2 >> 6) == (c2 >> 6)
        bdm_s[...] = same_head.astype(F32).astype(BF16)
        for l in range(1, 6):
            lm = same_head & level_mask(r2 & (CHUNK - 1), c2 & (CHUNK - 1), l)
            lvl_s[l - 1] = lm.astype(F32).astype(BF16)

    lm0_f = level_mask(row_q, col_q, 0).astype(F32)
    zero_blk = jnp.zeros((CHUNK, HEAD_DIM), BF16)
    zero_sq = jnp.zeros((HEAD_DIM, HEAD_DIM), BF16)

    def head_bcast(cols, width):
        if width == HEAD_DIM:
            return jnp.concatenate([jnp.broadcast_to(c_, (CHUNK, HEAD_DIM)) for c_ in cols], axis=1)
        full = [jnp.broadcast_to(c_, (CHUNK, quad)) for c_ in cols]
        return jnp.where(head_q == 0, full[0],
                         jnp.where(head_q == 1, full[1], jnp.where(head_q == 2, full[2], full[3])))

    def bd_wide(a):
        rows = []
        for h in range(N_HEADS):
            rows.append(jnp.concatenate(
                [a[:, j * HEAD_DIM:(j + 1) * HEAD_DIM] if j == h else zero_blk for j in range(N_HEADS)],
                axis=1))
        return jnp.concatenate(rows, axis=0)

    def tile4(a):
        return jnp.concatenate([a] * N_HEADS, axis=0)

    carried = {"state": None}

    def group(gi):
        chunks = range(gi * GROUP_CHUNKS, (gi + 1) * GROUP_CHUNKS)
        rows = {c: slice(c * CHUNK, (c + 1) * CHUNK) for c in chunks}

        for n, c in enumerate(chunks):
            bt = gb_ref[rows[c], :]
            gcum = bt
            for s in (1, 2, 4, 8, 16, 32):
                gcum = gcum + jnp.where(row_l >= s, pltpu.roll(gcum, s, axis=0), 0.0)
            g_cols = [gcum[:, h:h + 1] for h in range(N_HEADS)]
            b_cols = [bt[:, N_HEADS + h:N_HEADS + h + 1] for h in range(N_HEADS)]
            g_i = head_bcast(g_cols, CHUNK)
            b_i = head_bcast(b_cols, CHUNK)
            gw_s[c] = head_bcast(g_cols, HEAD_DIM)
            g_j = jnp.sum(jnp.where(eye_q, g_i, 0.0), axis=0, keepdims=True)
            b_j = jnp.sum(jnp.where(eye_q, b_i, 0.0), axis=0, keepdims=True)
            rv_s[c, 0:1, :] = b_j
            rv_s[c, 1:2, :] = b_j * jnp.exp(g_j)
            decay = jnp.exp(jnp.where(tri_incl, g_i - g_j, -1e30))

            k16 = k_ref[rows[c], :]
            kq = jnp.concatenate([k16, q_ref[rows[c], :]], axis=0)
            kq_k = lax.dot_general(kq, bd_wide(k16), (((1,), (1,)), ((), ())),
                                   preferred_element_type=F32)
            m_st = jnp.where(tri_strict, kq_k[:CHUNK] * b_i * decay, 0.0)
            at_s[c] = (kq_k[CHUNK:] * decay).astype(BF16)
            xs_s[c] = eye_f - m_st * lm0_f
            mt_s[c] = m_st.astype(BF16)
            if n % 2 == 1:
                yield

        for l in range(1, 6):
            for c in chunks:
                x = xs_s[c]
                x16 = x.astype(BF16)
                y = jnp.dot(x16, tile4(mt_s[c]) * lvl_s[l - 1], preferred_element_type=F32)
                z = jnp.dot(y.astype(BF16), tile4(x16) * bdm_s[...], preferred_element_type=F32)
                xs_s[c] = x - z
            yield

        for n, c in enumerate(chunks):
            k16 = k_ref[rows[c], :]
            q = q_ref[rows[c], :].astype(F32)
            k = k16.astype(F32)
            x = xs_s[c]
            u = jnp.dot((x * rv_s[c, 0:1, :]).astype(BF16), bd_wide(v_ref[rows[c], :]),
                        preferred_element_type=F32)
            w = jnp.dot((x * rv_s[c, 1:2, :]).astype(BF16), bd_wide(k16), preferred_element_type=F32)
            u16 = u.astype(BF16)
            w16 = w.astype(BF16)
            a_wu = jnp.dot(at_s[c], jnp.concatenate([bd_wide(w16), bd_wide(u16)], axis=1),
                           preferred_element_type=F32)
            g_wide = gw_s[c]
            g_last = g_wide[CHUNK - 1:CHUNK, :]
            aq_s[c, 2 * CHUNK:3 * CHUNK, :] = (q * jnp.exp(g_wide) - a_wu[:, :D_DN]).astype(BF16)
            op_s[c] = a_wu[:, D_DN:]
            gam_s[c] = jnp.broadcast_to(jnp.exp(g_last), (8, D_DN))
            k_dec = (k * jnp.exp(g_last - g_wide)).astype(BF16)
            for h in range(N_HEADS):
                hs = slice(h * HEAD_DIM, (h + 1) * HEAD_DIM)
                ab_h = _mm_tn(k_dec[:, hs], jnp.concatenate([w16[:, hs], u16[:, hs]], axis=1))
                aq_s[c, 0:2 * CHUNK, hs] = ab_h[:, :HEAD_DIM].astype(BF16)
                bb_s[c, :, hs] = ab_h[:, HEAD_DIM:]
            if n % 2 == 1:
                yield

        state = state_s[...] if gi == 0 else carried["state"]
        for c in chunks:
            s16 = state.astype(BF16)
            new_parts = []
            o_parts_c = []
            for p in range(N_HEADS // 2):
                ps = slice(p * 2 * HEAD_DIM, (p + 1) * 2 * HEAD_DIM)
                s_a = s16[:, (2 * p) * HEAD_DIM:(2 * p + 1) * HEAD_DIM]
                s_b = s16[:, (2 * p + 1) * HEAD_DIM:(2 * p + 2) * HEAD_DIM]
                bd = jnp.concatenate([jnp.concatenate([s_a, zero_sq], axis=1),
                                      jnp.concatenate([zero_sq, s_b], axis=1)], axis=0)
                r = jnp.dot(aq_s[c, :, ps], bd, preferred_element_type=F32)
                new_parts.append(gam_s[c, 0:1, ps] * state[:, ps] + bb_s[c, :, ps] - r[:2 * CHUNK])
                o_parts_c.append(op_s[c, :, ps] + r[2 * CHUNK:])
            state = jnp.concatenate(new_parts, axis=1)
            o_s[rows[c], :] = jnp.concatenate(o_parts_c, axis=1)
        carried["state"] = state
        if gi == n_chunks // GROUP_CHUNKS - 1:
            state_s[...] = state
        yield

        rg = slice(gi * GROUP_CHUNKS * CHUNK, (gi + 1) * GROUP_CHUNKS * CHUNK)
        zg = proj_ref[rg, COL_ZG:COL_ZG + D_DN].astype(F32)
        o_parts = []
        for h in range(N_HEADS):
            hs = slice(h * HEAD_DIM, (h + 1) * HEAD_DIM)
            oh = o_s[rg, hs]
            o_parts.append(oh * _rms_scale(oh) * dnw_ref[...] * zg[:, hs])
        o_dn = jnp.concatenate(o_parts, axis=1)
        y = (jnp.dot(o_dn.astype(BF16), wout_ref[0:D_DN, :], preferred_element_type=F32)
             + jnp.dot(proj_ref[rg, COL_YSC:COL_YSC + D_SC], wout_ref[D_DN:D_MODEL, :],
                       preferred_element_type=F32))
        out_ref[rg, :] = x_ref[rg, :] + mod_ref[2:3, :] * (y * _rms_scale(y) * pw_ref[...])

    _interleave([group(gi) for gi in range(n_chunks // GROUP_CHUNKS)], lead=2)


def _mixer(proj, gb, x2, mod3, dnw, w_out, pw, bsz, seq):
    tq = TQ_MIX
    per_b = seq // tq
    t = x2.shape[0]

    def tile(b, s):
        return (b * per_b + s, 0)

    const = lambda b, s: (0, 0)
    return pl.pallas_call(
        _mix_kernel,
        grid=(bsz, per_b),
        in_specs=[
            pl.BlockSpec((tq, N_P2), tile),
            pl.BlockSpec((tq, LANES), tile),
            pl.BlockSpec((tq, D_MODEL), tile),
            pl.BlockSpec((None, 6, D_MODEL), lambda b, s: (b, 0, 0)),
            pl.BlockSpec((1, HEAD_DIM), const),
            pl.BlockSpec((D_MODEL, D_MODEL), const),
            pl.BlockSpec((1, D_MODEL), const),
        ],
        out_specs=pl.BlockSpec((tq, D_MODEL), tile),
        out_shape=jax.ShapeDtypeStruct((t, D_MODEL), F32),
        scratch_shapes=[
            pltpu.VMEM((tq, D_DN), F32),
            pltpu.VMEM((HEAD_DIM, D_DN), F32),
            pltpu.VMEM((tq // CHUNK, 3 * CHUNK, D_DN), BF16),
            pltpu.VMEM((tq // CHUNK, HEAD_DIM, D_DN), F32),
            pltpu.VMEM((tq // CHUNK, CHUNK, D_DN), F32),
            pltpu.VMEM((tq // CHUNK, 8, D_DN), F32),
            pltpu.VMEM((N_HEADS * CHUNK, N_HEADS * CHUNK), BF16),
            pltpu.VMEM((5, N_HEADS * CHUNK, N_HEADS * CHUNK), BF16),
            pltpu.VMEM((tq // CHUNK, CHUNK, N_HEADS * CHUNK), F32),
            pltpu.VMEM((tq // CHUNK, CHUNK, N_HEADS * CHUNK), BF16),
            pltpu.VMEM((tq // CHUNK, CHUNK, N_HEADS * CHUNK), BF16),
            pltpu.VMEM((tq // CHUNK, CHUNK, D_DN), F32),
            pltpu.VMEM((tq // CHUNK, 8, N_HEADS * CHUNK), F32),
        ],
        compiler_params=pltpu.CompilerParams(dimension_semantics=("arbitrary", "arbitrary"),
                                             vmem_limit_bytes=VMEM_LIMIT),
        name="token_mixer",
    )(proj, gb, x2, mod3, dnw, w_out, pw)


def _ffn_kernel(x_ref, mod_ref, nw_ref, w1_hbm, w2_hbm, pw_ref, out_ref, w1_ref, w2_ref, stage_s, sem):
    @pl.when(pl.program_id(0) == 0)
    def _():
        pieces = []
        for j in range(D_FF // FF_CHUNK):
            cs = slice(j * FF_CHUNK, (j + 1) * FF_CHUNK)
            pieces.append((w1_hbm.at[:, cs], w1_ref.at[:, cs]))
        for j in range(D_FF // FF_CHUNK):
            cs = slice(j * FF_CHUNK, (j + 1) * FF_CHUNK)
            pieces.append((w2_hbm.at[cs, :], w2_ref.at[cs, :]))

        def copy(n):
            return pltpu.make_async_copy(pieces[n][0], stage_s.at[n % 2], sem.at[n % 2])

        copy(0).start()
        for n in range(len(pieces)):
            if n + 1 < len(pieces):
                copy(n + 1).start()
            copy(n).wait()
            pieces[n][1][...] = stage_s[n % 2].astype(BF16)

    x = x_ref[...]
    h = x * _rms_scale(x) * nw_ref[...]
    h = h * (1.0 + mod_ref[4:5, :]) + mod_ref[3:4, :]
    hb = h.astype(BF16)
    y = jnp.zeros(x.shape, F32)
    for j in range(D_FF // FF_CHUNK):
        a = jnp.dot(hb, w1_ref[:, j * FF_CHUNK:(j + 1) * FF_CHUNK], preferred_element_type=F32)
        a = jnp.square(jnp.maximum(a, 0.0)).astype(BF16)
        y = y + jnp.dot(a, w2_ref[j * FF_CHUNK:(j + 1) * FF_CHUNK, :], preferred_element_type=F32)
    out_ref[...] = x + mod_ref[5:6, :] * (y * _rms_scale(y) * pw_ref[...])


def _ffn(x2, mod3, nw, w1, w2, pw, seq):
    t = x2.shape[0]
    tm = TM_FFN
    per_b = seq // tm
    return pl.pallas_call(
        _ffn_kernel,
        grid=(t // tm,),
        in_specs=[
            pl.BlockSpec((tm, D_MODEL), lambda i: (i, 0)),
            pl.BlockSpec((None, 6, D_MODEL), lambda i: (i // per_b, 0, 0)),
            pl.BlockSpec((1, D_MODEL), lambda i: (0, 0)),
            pl.BlockSpec(memory_space=pl.ANY),
            pl.BlockSpec(memory_space=pl.ANY),
            pl.BlockSpec((1, D_MODEL), lambda i: (0, 0)),
        ],
        out_specs=pl.BlockSpec((tm, D_MODEL), lambda i: (i, 0)),
        out_shape=jax.ShapeDtypeStruct((t, D_MODEL), F32),
        scratch_shapes=[
            pltpu.VMEM((D_MODEL, D_FF), BF16),
            pltpu.VMEM((D_FF, D_MODEL), BF16),
            pltpu.VMEM((2, FF_CHUNK, FF_CHUNK), F32),
            pltpu.SemaphoreType.DMA((2,)),
        ],
        compiler_params=pltpu.CompilerParams(dimension_semantics=("arbitrary",),
                                             vmem_limit_bytes=VMEM_LIMIT),
        name="ffn",
    )(x2, mod3, nw, w1, w2, pw)


def _layer(x2, c, w_ada, b_ada, pre_mix_w, post_mix_w, w_in, dn_conv_w, dn_a_log, dn_dt_bias,
           dn_norm_w, sc_conv_w, w_out, pre_ffn_w, post_ffn_w, w_ff1, w_ff2, bsz, seq):
    mod3 = _ada(c, w_ada, b_ada[None, :]).reshape(bsz, 6, D_MODEL)

    gp = jnp.zeros((8, LANES), F32)
    gp = gp.at[0, :N_HEADS].set(dn_a_log).at[1, :N_HEADS].set(dn_dt_bias)
    proj, gb = _in_proj(x2, mod3, pre_mix_w[None, :], w_in.T, dn_conv_w, sc_conv_w, gp, seq)

    x2 = _mixer(proj, gb, x2, mod3, dn_norm_w[None, :], w_out.astype(BF16), post_mix_w[None, :],
                bsz, seq)

    return _ffn(x2, mod3, pre_ffn_w[None, :], w_ff1, w_ff2, post_ffn_w[None, :], seq)


def kernel(x, c, w_ada, b_ada, pre_mix_norm_w, post_mix_norm_w, w_in, dn_conv_w, dn_a_log, dn_dt_bias, dn_norm_w, sc_conv_w, w_out, pre_ffn_norm_w, post_ffn_norm_w, w_ff1, w_ff2):
    bsz, seq, d = x.shape
    x2 = x.reshape(bsz * seq, d)
    for l in range(w_ada.shape[0]):
        x2 = _layer(x2, c, w_ada[l], b_ada[l], pre_mix_norm_w[l], post_mix_norm_w[l], w_in[l],
                    dn_conv_w[l], dn_a_log[l], dn_dt_bias[l], dn_norm_w[l], sc_conv_w[l], w_out[l],
                    pre_ffn_norm_w[l], post_ffn_norm_w[l], w_ff1[l], w_ff2[l], bsz, seq)
    return x2.reshape(bsz, seq, d)
```

```python
import functools

import jax
import jax.numpy as jnp
from jax import lax
from jax.experimental import pallas as pl
from jax.experimental.pallas import tpu as pltpu

F32 = jnp.float32
BF16 = jnp.bfloat16

D_MODEL = 1024
N_HEADS = 4
HEAD_DIM = 128
D_DN = N_HEADS * HEAD_DIM
D_SC = D_MODEL - D_DN
DN_CONV = 4
SC_CONV = 3
CHUNK = 64
D_FF = 4 * D_MODEL
EPS = 1e-6
N_QKV = 3 * D_DN
N_MAIN = 4 * D_DN + 3 * D_SC
COL_Z = N_QKV
COL_SB = 4 * D_DN
COL_SC = COL_SB + D_SC
COL_SH = COL_SC + D_SC
N_P2 = N_QKV + D_DN + D_SC
COL_ZG = N_QKV
COL_YSC = N_QKV + D_DN
LANES = 128
HALO = 8

TM_IN = 512
TQ_MIX = 1024
GROUP_CHUNKS = 4
TM_FFN = 512
FF_CHUNK = 1024
VMEM_LIMIT = 56 * 1024 * 1024


def _sigmoid(x):
    return 0.5 + 0.5 * jnp.tanh(0.5 * x)


def _mm_tn(a, b):
    return lax.dot_general(a, b, (((0,), (0,)), ((), ())), preferred_element_type=F32)


def _rms_scale(x):
    return lax.rsqrt(jnp.mean(x * x, axis=-1, keepdims=True) + EPS)


def _interleave(gens, lead):
    live = []
    pending = list(gens)
    rnd = 0
    while live or pending:
        if pending and rnd % lead == 0:
            live.append(pending.pop(0))
        for g in list(live):
            try:
                next(g)
            except StopIteration:
                live.remove(g)
        rnd += 1


def _ada_kernel(c_ref, w_ref, b_ref, o_ref):
    c = c_ref[...]
    ca = (c * _sigmoid(c)).astype(BF16)
    o_ref[...] = jnp.dot(ca, w_ref[...].astype(BF16), preferred_element_type=F32) + b_ref[...]


def _ada(c, w_ada, b_ada):
    bsz = c.shape[0]
    n = w_ada.shape[1]
    bn = 1024
    return pl.pallas_call(
        _ada_kernel,
        grid=(n // bn,),
        in_specs=[
            pl.BlockSpec((bsz, D_MODEL), lambda j: (0, 0)),
            pl.BlockSpec((D_MODEL, bn), lambda j: (0, j)),
            pl.BlockSpec((1, bn), lambda j: (0, j)),
        ],
        out_specs=pl.BlockSpec((bsz, bn), lambda j: (0, j)),
        out_shape=jax.ShapeDtypeStruct((bsz, n), F32),
        compiler_params=pltpu.CompilerParams(dimension_semantics=("arbitrary",),
                                             vmem_limit_bytes=VMEM_LIMIT),
        name="ada_mod",
    )(c, w_ada, b_ada)


def _causal_conv(ext, w_ref, c0, width):
    r = ext.shape[0] - HALO
    n = ext.shape[1]
    acc = w_ref[width - 1:width, c0:c0 + n] * ext[HALO:HALO + r, :]
    for s in range(1, width):
        acc = acc + w_ref[width - 1 - s:width - s, c0:c0 + n] * pltpu.roll(ext, s, axis=0)[HALO:HALO + r, :]
    return acc


def _in_half(hf, x_ref, mod_ref, nw_ref, wm_ref, wab_ref, cw_ref, scw_ref, gp_ref,
             proj_ref, gb_ref, e_s, p_s):
    half = x_ref.shape[0] // 2
    r0 = hf * half
    rows = slice(r0, r0 + half)
    erows = slice(HALO + r0, HALO + r0 + half)
    wrows = slice(r0, r0 + half + HALO)

    x = x_ref[rows, :]
    h = x * _rms_scale(x) * nw_ref[...]
    h = h * (1.0 + mod_ref[1:2, :]) + mod_ref[0:1, :]
    hb = h.astype(BF16)
    yield

    def proj(c0, n=D_DN):
        return jnp.dot(hb, wm_ref[:, c0:c0 + n], preferred_element_type=F32)

    for part in range(3):
        e_s[part, erows, :] = proj(part * D_DN)
        yield
    e_s[3, erows, :] = proj(COL_SC)
    p_s[0, rows, :] = proj(COL_Z)
    yield
    p_s[1, rows, :] = proj(COL_SH)
    p_s[2, rows, :] = proj(COL_SB)
    ab = jnp.dot(hb, wab_ref[...], preferred_element_type=F32)
    yield

    for part in range(3):
        c0 = part * D_DN
        acc = _causal_conv(e_s[part, wrows, :], cw_ref, c0, DN_CONV)
        y = acc * _sigmoid(acc)
        if part < 2:
            parts = []
            for hd in range(N_HEADS):
                yh = y[:, hd * HEAD_DIM:(hd + 1) * HEAD_DIM]
                nrm = lax.rsqrt(jnp.sum(yh * yh, axis=-1, keepdims=True) + EPS)
                if part == 0:
                    nrm = nrm * (HEAD_DIM ** -0.5)
                parts.append(yh * nrm)
            y = jnp.concatenate(parts, axis=1)
        proj_ref[rows, c0:c0 + D_DN] = y.astype(BF16)
        yield

    p_z = p_s[0, rows, :]
    proj_ref[rows, COL_ZG:COL_ZG + D_DN] = (p_z * _sigmoid(p_z)).astype(BF16)
    e_s[3, erows, :] = e_s[3, erows, :] * p_s[1, rows, :]
    conv = _causal_conv(e_s[3, wrows, :], scw_ref, 0, SC_CONV)
    proj_ref[rows, COL_YSC:COL_YSC + D_SC] = (p_s[2, rows, :] * conv).astype(BF16)

    xg = ab + gp_ref[1:2, :]
    softplus = jnp.maximum(xg, 0.0) + jnp.log1p(jnp.exp(-jnp.abs(xg)))
    lane = lax.broadcasted_iota(jnp.int32, ab.shape, 1)
    gb_ref[rows, :] = jnp.where(lane < N_HEADS, -jnp.exp(gp_ref[0:1, :]) * softplus, _sigmoid(ab))


def _in_kernel(x_ref, mod_ref, nw_ref, win_ref, cw_ref, scw_ref, gp_ref,
               proj_ref, gb_ref, e_s, p_s, wm_ref, wab_ref, *, per_b):
    tm = x_ref.shape[0]

    @pl.when(pl.program_id(0) == 0)
    def _():
        n_gate = 2 * N_HEADS
        cb = D_DN // 2
        for r in range(N_MAIN // cb):
            src = r * cb + (n_gate if r * cb >= COL_SB else 0)
            wm_ref[:, r * cb:(r + 1) * cb] = win_ref[src:src + cb, :].T.astype(BF16)
        wab_ref[...] = win_ref[COL_SB:COL_SB + LANES, :].T.astype(BF16)

    @pl.when(pl.program_id(0) % per_b == 0)
    def _():
        e_s[:, 0:HALO, :] = jnp.zeros((e_s.shape[0], HALO, D_DN), F32)

    _interleave([_in_half(hf, x_ref, mod_ref, nw_ref, wm_ref, wab_ref, cw_ref, scw_ref, gp_ref,
                          proj_ref, gb_ref, e_s, p_s) for hf in range(2)], lead=5)
    e_s[:, 0:HALO, :] = e_s[:, tm:tm + HALO, :]


def _in_proj(x2, mod3, nw, w_in, cw, scw, gp, seq):
    t = x2.shape[0]
    tm = TM_IN
    per_b = seq // tm
    const = lambda i: (0, 0)
    return pl.pallas_call(
        functools.partial(_in_kernel, per_b=per_b),
        grid=(t // tm,),
        in_specs=[
            pl.BlockSpec((tm, D_MODEL), lambda i: (i, 0)),
            pl.BlockSpec((None, 6, D_MODEL), lambda i: (i // per_b, 0, 0)),
            pl.BlockSpec((1, D_MODEL), const),
            pl.BlockSpec(w_in.shape, const, pipeline_mode=pl.Buffered(1)),
            pl.BlockSpec((DN_CONV, N_QKV), const),
            pl.BlockSpec((SC_CONV, D_SC), const),
            pl.BlockSpec((8, LANES), const),
        ],
        out_specs=[
            pl.BlockSpec((tm, N_P2), lambda i: (i, 0)),
            pl.BlockSpec((tm, LANES), lambda i: (i, 0)),
        ],
        out_shape=[
            jax.ShapeDtypeStruct((t, N_P2), BF16),
            jax.ShapeDtypeStruct((t, LANES), F32),
        ],
        scratch_shapes=[
            pltpu.VMEM((4, HALO + tm, D_DN), F32),
            pltpu.VMEM((3, tm, D_DN), F32),
            pltpu.VMEM((D_MODEL, N_MAIN), BF16),
            pltpu.VMEM((D_MODEL, LANES), BF16),
        ],
        compiler_params=pltpu.CompilerParams(dimension_semantics=("arbitrary",),
                                             vmem_limit_bytes=VMEM_LIMIT),
        name="in_proj",
    )(x2, mod3, nw, w_in, cw, scw, gp)


def _mix_kernel(proj_ref, gb_ref, x_ref, mod_ref, dnw_ref, wout_ref, pw_ref, out_ref,
                o_s, state_s, aq_s, bb_s, op_s, gam_s, bdm_s, lvl_s, xs_s, mt_s, at_s, gw_s, rv_s):
    tq = x_ref.shape[0]
    t = pl.program_id(1)

    @pl.when(t == 0)
    def _():
        state_s[...] = jnp.zeros_like(state_s)

    q_ref = proj_ref.at[:, 0:D_DN]
    k_ref = proj_ref.at[:, D_DN:2 * D_DN]
    v_ref = proj_ref.at[:, 2 * D_DN:3 * D_DN]

    n_chunks = tq // CHUNK
    quad = N_HEADS * CHUNK
    row_q = lax.broadcasted_iota(jnp.int32, (CHUNK, quad), 0)
    lane_q = lax.broadcasted_iota(jnp.int32, (CHUNK, quad), 1)
    col_q = lane_q & (CHUNK - 1)
    head_q = lane_q >> 6
    tri_incl = row_q >= col_q
    tri_strict = row_q > col_q
    eye_q = row_q == col_q
    eye_f = eye_q.astype(F32)
    row_l = lax.broadcasted_iota(jnp.int32, (CHUNK, LANES), 0)

    def level_mask(r, c, l):
        return (((r >> l) & 1) == 1) & (((c >> l) & 1) == 0) & ((r >> (l + 1)) == (c >> (l + 1)))

    @pl.when((pl.program_id(0) == 0) & (t == 0))
    def _():
        r2 = lax.broadcasted_iota(jnp.int32, (quad, quad), 0)
        c2 = lax.broadcasted_iota(jnp.int32, (quad, quad), 1)
        same_head = (r2 >> 6) == (c2 >> 6)
        bdm_s[...] = same_head.astype(F32).astype(BF16)
        for l in range(1, 6):
            lm = same_head & level_mask(r2 & (CHUNK - 1), c2 & (CHUNK - 1), l)
            lvl_s[l - 1] = lm.astype(F32).astype(BF16)

    lm0_f = level_mask(row_q, col_q, 0).astype(F32)
    zero_blk = jnp.zeros((CHUNK, HEAD_DIM), BF16)
    zero_sq = jnp.zeros((HEAD_DIM, HEAD_DIM), BF16)

    def head_bcast(cols, width):
        if width == HEAD_DIM:
            return jnp.concatenate([jnp.broadcast_to(c_, (CHUNK, HEAD_DIM)) for c_ in cols], axis=1)
        full = [jnp.broadcast_to(c_, (CHUNK, quad)) for c_ in cols]
        return jnp.where(head_q == 0, full[0],
                         jnp.where(head_q == 1, full[1], jnp.where(head_q == 2, full[2], full[3])))

    def bd_wide(a):
        rows = []
        for h in range(N_HEADS):
            rows.append(jnp.concatenate(
                [a[:, j * HEAD_DIM:(j + 1) * HEAD_DIM] if j == h else zero_blk for j in range(N_HEADS)],
                axis=1))
        return jnp.concatenate(rows, axis=0)

    def tile4(a):
        return jnp.concatenate([a] * N_HEADS, axis=0)

    carried = {"state": None}

    def group(gi):
        chunks = range(gi * GROUP_CHUNKS, (gi + 1) * GROUP_CHUNKS)
        rows = {c: slice(c * CHUNK, (c + 1) * CHUNK) for c in chunks}

        for n, c in enumerate(chunks):
            bt = gb_ref[rows[c], :]
            gcum = bt
            for s in (1, 2, 4, 8, 16, 32):
                gcum = gcum + jnp.where(row_l >= s, pltpu.roll(gcum, s, axis=0), 0.0)
            g_cols = [gcum[:, h:h + 1] for h in range(N_HEADS)]
            b_cols = [bt[:, N_HEADS + h:N_HEADS + h + 1] for h in range(N_HEADS)]
            g_i = head_bcast(g_cols, CHUNK)
            b_i = head_bcast(b_cols, CHUNK)
            gw_s[c] = head_bcast(g_cols, HEAD_DIM)
            g_j = jnp.sum(jnp.where(eye_q, g_i, 0.0), axis=0, keepdims=True)
            b_j = jnp.sum(jnp.where(eye_q, b_i, 0.0), axis=0, keepdims=True)
            rv_s[c, 0:1, :] = b_j
            rv_s[c, 1:2, :] = b_j * jnp.exp(g_j)
            decay = jnp.exp(jnp.where(tri_incl, g_i - g_j, -1e30))

            k16 = k_ref[rows[c], :]
            kq = jnp.concatenate([k16, q_ref[rows[c], :]], axis=0)
            kq_k = lax.dot_general(kq, bd_wide(k16), (((1,), (1,)), ((), ())),
                                   preferred_element_type=F32)
            m_st = jnp.where(tri_strict, kq_k[:CHUNK] * b_i * decay, 0.0)
            at_s[c] = (kq_k[CHUNK:] * decay).astype(BF16)
            xs_s[c] = eye_f - m_st * lm0_f
            mt_s[c] = m_st.astype(BF16)
            if n % 2 == 1:
                yield

        for l in range(1, 6):
            for c in chunks:
                x = xs_s[c]
                x16 = x.astype(BF16)
                y = jnp.dot(x16, tile4(mt_s[c]) * lvl_s[l - 1], preferred_element_type=F32)
                z = jnp.dot(y.astype(BF16), tile4(x16) * bdm_s[...], preferred_element_type=F32)
                xs_s[c] = x - z
            yield

        for n, c in enumerate(chunks):
            k16 = k_ref[rows[c], :]
            q = q_ref[rows[c], :].astype(F32)
            k = k16.astype(F32)
            x = xs_s[c]
            u = jnp.dot((x * rv_s[c, 0:1, :]).astype(BF16), bd_wide(v_ref[rows[c], :]),
                        preferred_element_type=F32)
            w = jnp.dot((x * rv_s[c, 1:2, :]).astype(BF16), bd_wide(k16), preferred_element_type=F32)
            u16 = u.astype(BF16)
            w16 = w.astype(BF16)
            a_wu = jnp.dot(at_s[c], jnp.concatenate([bd_wide(w16), bd_wide(u16)], axis=1),
                           preferred_element_type=F32)
            g_wide = gw_s[c]
            g_last = g_wide[CHUNK - 1:CHUNK, :]
            aq_s[c, 2 * CHUNK:3 * CHUNK, :] = (q * jnp.exp(g_wide) - a_wu[:, :D_DN]).astype(BF16)
            op_s[c] = a_wu[:, D_DN:]
            gam_s[c] = jnp.broadcast_to(jnp.exp(g_last), (8, D_DN))
            k_dec = (k * jnp.exp(g_last - g_wide)).astype(BF16)
            for h in range(N_HEADS):
                hs = slice(h * HEAD_DIM, (h + 1) * HEAD_DIM)
                ab_h = _mm_tn(k_dec[:, hs], jnp.concatenate([w16[:, hs], u16[:, hs]], axis=1))
                aq_s[c, 0:2 * CHUNK, hs] = ab_h[:, :HEAD_DIM].astype(BF16)
                bb_s[c, :, hs] = ab_h[:, HEAD_DIM:]
            if n % 2 == 1:
                yield

        state = state_s[...] if gi == 0 else carried["state"]
        for c in chunks:
            s16 = state.astype(BF16)
            new_parts = []
            o_parts_c = []
            for p in range(N_HEADS // 2):
                ps = slice(p * 2 * HEAD_DIM, (p + 1) * 2 * HEAD_DIM)
                s_a = s16[:, (2 * p) * HEAD_DIM:(2 * p + 1) * HEAD_DIM]
                s_b = s16[:, (2 * p + 1) * HEAD_DIM:(2 * p + 2) * HEAD_DIM]
                bd = jnp.concatenate([jnp.concatenate([s_a, zero_sq], axis=1),
                                      jnp.concatenate([zero_sq, s_b], axis=1)], axis=0)
                r = jnp.dot(aq_s[c, :, ps], bd, preferred_element_type=F32)
                new_parts.append(gam_s[c, 0:1, ps] * state[:, ps] + bb_s[c, :, ps] - r[:2 * CHUNK])
                o_parts_c.append(op_s[c, :, ps] + r[2 * CHUNK:])
            state = jnp.concatenate(new_parts, axis=1)
            o_s[rows[c], :] = jnp.concatenate(o_parts_c, axis=1)
        carried["state"] = state
        if gi == n_chunks // GROUP_CHUNKS - 1:
            state_s[...] = state
        yield

        rg = slice(gi * GROUP_CHUNKS * CHUNK, (gi + 1) * GROUP_CHUNKS * CHUNK)
        zg = proj_ref[rg, COL_ZG:COL_ZG + D_DN].astype(F32)
        o_parts = []
        for h in range(N_HEADS):
            hs = slice(h * HEAD_DIM, (h + 1) * HEAD_DIM)
            oh = o_s[rg, hs]
            o_parts.append(oh * _rms_scale(oh) * dnw_ref[...] * zg[:, hs])
        o_dn = jnp.concatenate(o_parts, axis=1)
        y = (jnp.dot(o_dn.astype(BF16), wout_ref[0:D_DN, :], preferred_element_type=F32)
             + jnp.dot(proj_ref[rg, COL_YSC:COL_YSC + D_SC], wout_ref[D_DN:D_MODEL, :],
                       preferred_element_type=F32))
        out_ref[rg, :] = x_ref[rg, :] + mod_ref[2:3, :] * (y * _rms_scale(y) * pw_ref[...])

    _interleave([group(gi) for gi in range(n_chunks // GROUP_CHUNKS)], lead=2)


def _mixer(proj, gb, x2, mod3, dnw, w_out, pw, bsz, seq):
    tq = TQ_MIX
    per_b = seq // tq
    t = x2.shape[0]

    def tile(b, s):
        return (b * per_b + s, 0)

    const = lambda b, s: (0, 0)
    return pl.pallas_call(
        _mix_kernel,
        grid=(bsz, per_b),
        in_specs=[
            pl.BlockSpec((tq, N_P2), tile),
            pl.BlockSpec((tq, LANES), tile),
            pl.BlockSpec((tq, D_MODEL), tile),
            pl.BlockSpec((None, 6, D_MODEL), lambda b, s: (b, 0, 0)),
            pl.BlockSpec((1, HEAD_DIM), const),
            pl.BlockSpec((D_MODEL, D_MODEL), const),
            pl.BlockSpec((1, D_MODEL), const),
        ],
        out_specs=pl.BlockSpec((tq, D_MODEL), tile),
        out_shape=jax.ShapeDtypeStruct((t, D_MODEL), F32),
        scratch_shapes=[
            pltpu.VMEM((tq, D_DN), F32),
            pltpu.VMEM((HEAD_DIM, D_DN), F32),
            pltpu.VMEM((tq // CHUNK, 3 * CHUNK, D_DN), BF16),
            pltpu.VMEM((tq // CHUNK, HEAD_DIM, D_DN), F32),
            pltpu.VMEM((tq // CHUNK, CHUNK, D_DN), F32),
            pltpu.VMEM((tq // CHUNK, 8, D_DN), F32),
            pltpu.VMEM((N_HEADS * CHUNK, N_HEADS * CHUNK), BF16),
            pltpu.VMEM((5, N_HEADS * CHUNK, N_HEADS * CHUNK), BF16),
            pltpu.VMEM((tq // CHUNK, CHUNK, N_HEADS * CHUNK), F32),
            pltpu.VMEM((tq // CHUNK, CHUNK, N_HEADS * CHUNK), BF16),
            pltpu.VMEM((tq // CHUNK, CHUNK, N_HEADS * CHUNK), BF16),
            pltpu.VMEM((tq // CHUNK, CHUNK, D_DN), F32),
            pltpu.VMEM((tq // CHUNK, 8, N_HEADS * CHUNK), F32),
        ],
        compiler_params=pltpu.CompilerParams(dimension_semantics=("arbitrary", "arbitrary"),
                                             vmem_limit_bytes=VMEM_LIMIT),
        name="token_mixer",
    )(proj, gb, x2, mod3, dnw, w_out, pw)


def _ffn_kernel(x_ref, mod_ref, nw_ref, w1_hbm, w2_hbm, pw_ref, out_ref, w1_ref, w2_ref, stage_s, sem):
    @pl.when(pl.program_id(0) == 0)
    def _():
        pieces = []
        for j in range(D_FF // FF_CHUNK):
            cs = slice(j * FF_CHUNK, (j + 1) * FF_CHUNK)
            pieces.append((w1_hbm.at[:, cs], w1_ref.at[:, cs]))
        for j in range(D_FF // FF_CHUNK):
            cs = slice(j * FF_CHUNK, (j + 1) * FF_CHUNK)
            pieces.append((w2_hbm.at[cs, :], w2_ref.at[cs, :]))

        def copy(n):
            return pltpu.make_async_copy(pieces[n][0], stage_s.at[n % 2], sem.at[n % 2])

        copy(0).start()
        for n in range(len(pieces)):
            if n + 1 < len(pieces):
                copy(n + 1).start()
            copy(n).wait()
            pieces[n][1][...] = stage_s[n % 2].astype(BF16)

    x = x_ref[...]
    h = x * _rms_scale(x) * nw_ref[...]
    h = h * (1.0 + mod_ref[4:5, :]) + mod_ref[3:4, :]
    hb = h.astype(BF16)
    y = jnp.zeros(x.shape, F32)
    for j in range(D_FF // FF_CHUNK):
        a = jnp.dot(hb, w1_ref[:, j * FF_CHUNK:(j + 1) * FF_CHUNK], preferred_element_type=F32)
        a = jnp.square(jnp.maximum(a, 0.0)).astype(BF16)
        y = y + jnp.dot(a, w2_ref[j * FF_CHUNK:(j + 1) * FF_CHUNK, :], preferred_element_type=F32)
    out_ref[...] = x + mod_ref[5:6, :] * (y * _rms_scale(y) * pw_ref[...])


def _ffn(x2, mod3, nw, w1, w2, pw, seq):
    t = x2.shape[0]
    tm = TM_FFN
    per_b = seq // tm
    return pl.pallas_call(
        _ffn_kernel,
        grid=(t // tm,),
        in_specs=[
            pl.BlockSpec((tm, D_MODEL), lambda i: (i, 0)),
            pl.BlockSpec((None, 6, D_MODEL), lambda i: (i // per_b, 0, 0)),
            pl.BlockSpec((1, D_MODEL), lambda i: (0, 0)),
            pl.BlockSpec(memory_space=pl.ANY),
            pl.BlockSpec(memory_space=pl.ANY),
            pl.BlockSpec((1, D_MODEL), lambda i: (0, 0)),
        ],
        out_specs=pl.BlockSpec((tm, D_MODEL), lambda i: (i, 0)),
        out_shape=jax.ShapeDtypeStruct((t, D_MODEL), F32),
        scratch_shapes=[
            pltpu.VMEM((D_MODEL, D_FF), BF16),
            pltpu.VMEM((D_FF, D_MODEL), BF16),
            pltpu.VMEM((2, FF_CHUNK, FF_CHUNK), F32),
            pltpu.SemaphoreType.DMA((2,)),
        ],
        compiler_params=pltpu.CompilerParams(dimension_semantics=("arbitrary",),
                                             vmem_limit_bytes=VMEM_LIMIT),
        name="ffn",
    )(x2, mod3, nw, w1, w2, pw)


def _layer(x2, c, w_ada, b_ada, pre_mix_w, post_mix_w, w_in, dn_conv_w, dn_a_log, dn_dt_bias,
           dn_norm_w, sc_conv_w, w_out, pre_ffn_w, post_ffn_w, w_ff1, w_ff2, bsz, seq):
    mod3 = _ada(c, w_ada, b_ada[None, :]).reshape(bsz, 6, D_MODEL)

    gp = jnp.zeros((8, LANES), F32)
    gp = gp.at[0, :N_HEADS].set(dn_a_log).at[1, :N_HEADS].set(dn_dt_bias)
    proj, gb = _in_proj(x2, mod3, pre_mix_w[None, :], w_in.T, dn_conv_w, sc_conv_w, gp, seq)

    x2 = _mixer(proj, gb, x2, mod3, dn_norm_w[None, :], w_out.astype(BF16), post_mix_w[None, :],
                bsz, seq)

    return _ffn(x2, mod3, pre_ffn_w[None, :], w_ff1, w_ff2, post_ffn_w[None, :], seq)


def kernel(x, c, w_ada, b_ada, pre_mix_norm_w, post_mix_norm_w, w_in, dn_conv_w, dn_a_log, dn_dt_bias, dn_norm_w, sc_conv_w, w_out, pre_ffn_norm_w, post_ffn_norm_w, w_ff1, w_ff2):
    bsz, seq, d = x.shape
    x2 = x.reshape(bsz * seq, d)
    for l in range(w_ada.shape[0]):
        x2 = _layer(x2, c, w_ada[l], b_ada[l], pre_mix_norm_w[l], post_mix_norm_w[l], w_in[l],
                    dn_conv_w[l], dn_a_log[l], dn_dt_bias[l], dn_norm_w[l], sc_conv_w[l], w_out[l],
                    pre_ffn_norm_w[l], post_ffn_norm_w[l], w_ff1[l], w_ff2[l], bsz, seq)
    return x2.reshape(bsz, seq, d)
```

```python
import functools

import jax
import jax.numpy as jnp
from jax import lax
from jax.experimental import pallas as pl
from jax.experimental.pallas import tpu as pltpu

F32 = jnp.float32
BF16 = jnp.bfloat16

D_MODEL = 1024
N_HEADS = 4
HEAD_DIM = 128
D_DN = N_HEADS * HEAD_DIM
D_SC = D_MODEL - D_DN
DN_CONV = 4
SC_CONV = 3
CHUNK = 64
D_FF = 4 * D_MODEL
EPS = 1e-6
N_QKV = 3 * D_DN
N_MAIN = 4 * D_DN + 3 * D_SC
COL_Z = N_QKV
COL_SB = 4 * D_DN
COL_SC = COL_SB + D_SC
COL_SH = COL_SC + D_SC
N_P2 = N_QKV + D_DN + D_SC
COL_ZG = N_QKV
COL_YSC = N_QKV + D_DN
LANES = 128
HALO = 8

TM_IN = 512
TQ_MIX = 1024
GROUP_CHUNKS = 4
TM_FFN = 1024
FFN_ROWS = 256
FF_CHUNK = 1024
VMEM_LIMIT = 56 * 1024 * 1024


def _sigmoid(x):
    return 0.5 + 0.5 * jnp.tanh(0.5 * x)


def _mm_tn(a, b):
    return lax.dot_general(a, b, (((0,), (0,)), ((), ())), preferred_element_type=F32)


def _rms_scale(x):
    return lax.rsqrt(jnp.mean(x * x, axis=-1, keepdims=True) + EPS)


def _interleave(gens, lead):
    live = []
    pending = list(gens)
    rnd = 0
    while live or pending:
        if pending and rnd % lead == 0:
            live.append(pending.pop(0))
        for g in list(live):
            try:
                next(g)
            except StopIteration:
                live.remove(g)
        rnd += 1


def _ada_kernel(c_ref, w_ref, b_ref, o_ref):
    c = c_ref[...]
    ca = (c * _sigmoid(c)).astype(BF16)
    o_ref[...] = jnp.dot(ca, w_ref[...].astype(BF16), preferred_element_type=F32) + b_ref[...]


def _ada(c, w_ada, b_ada):
    bsz = c.shape[0]
    n = w_ada.shape[1]
    bn = 1024
    return pl.pallas_call(
        _ada_kernel,
        grid=(n // bn,),
        in_specs=[
            pl.BlockSpec((bsz, D_MODEL), lambda j: (0, 0)),
            pl.BlockSpec((D_MODEL, bn), lambda j: (0, j)),
            pl.BlockSpec((1, bn), lambda j: (0, j)),
        ],
        out_specs=pl.BlockSpec((bsz, bn), lambda j: (0, j)),
        out_shape=jax.ShapeDtypeStruct((bsz, n), F32),
        compiler_params=pltpu.CompilerParams(dimension_semantics=("arbitrary",),
                                             vmem_limit_bytes=VMEM_LIMIT),
        name="ada_mod",
    )(c, w_ada, b_ada)


def _causal_conv(ext, w_ref, c0, width):
    r = ext.shape[0] - HALO
    n = ext.shape[1]
    acc = w_ref[width - 1:width, c0:c0 + n] * ext[HALO:HALO + r, :]
    for s in range(1, width):
        acc = acc + w_ref[width - 1 - s:width - s, c0:c0 + n] * pltpu.roll(ext, s, axis=0)[HALO:HALO + r, :]
    return acc


def _in_half(hf, x_ref, mod_ref, nw_ref, wm_ref, wab_ref, cw_ref, scw_ref, gp_ref,
             proj_ref, gb_ref, e_s, p_s):
    half = x_ref.shape[0] // 2
    r0 = hf * half
    rows = slice(r0, r0 + half)
    erows = slice(HALO + r0, HALO + r0 + half)
    wrows = slice(r0, r0 + half + HALO)

    x = x_ref[rows, :]
    h = x * _rms_scale(x) * nw_ref[...]
    h = h * (1.0 + mod_ref[1:2, :]) + mod_ref[0:1, :]
    hb = h.astype(BF16)
    yield

    def proj(c0, n=D_DN):
        return jnp.dot(hb, wm_ref[:, c0:c0 + n], preferred_element_type=F32)

    for part in range(3):
        e_s[part, erows, :] = proj(part * D_DN)
        yield
    e_s[3, erows, :] = proj(COL_SC)
    p_s[0, rows, :] = proj(COL_Z)
    yield
    p_s[1, rows, :] = proj(COL_SH)
    p_s[2, rows, :] = proj(COL_SB)
    ab = jnp.dot(hb, wab_ref[...], preferred_element_type=F32)
    yield

    for part in range(3):
        c0 = part * D_DN
        acc = _causal_conv(e_s[part, wrows, :], cw_ref, c0, DN_CONV)
        y = acc * _sigmoid(acc)
        if part < 2:
            parts = []
            for hd in range(N_HEADS):
                yh = y[:, hd * HEAD_DIM:(hd + 1) * HEAD_DIM]
                nrm = lax.rsqrt(jnp.sum(yh * yh, axis=-1, keepdims=True) + EPS)
                if part == 0:
                    nrm = nrm * (HEAD_DIM ** -0.5)
                parts.append(yh * nrm)
            y = jnp.concatenate(parts, axis=1)
        proj_ref[rows, c0:c0 + D_DN] = y.astype(BF16)
        yield

    p_z = p_s[0, rows, :]
    proj_ref[rows, COL_ZG:COL_ZG + D_DN] = (p_z * _sigmoid(p_z)).astype(BF16)
    e_s[3, erows, :] = e_s[3, erows, :] * p_s[1, rows, :]
    conv = _causal_conv(e_s[3, wrows, :], scw_ref, 0, SC_CONV)
    proj_ref[rows, COL_YSC:COL_YSC + D_SC] = (p_s[2, rows, :] * conv).astype(BF16)

    xg = ab + gp_ref[1:2, :]
    softplus = jnp.maximum(xg, 0.0) + jnp.log1p(jnp.exp(-jnp.abs(xg)))
    lane = lax.broadcasted_iota(jnp.int32, ab.shape, 1)
    gb_ref[rows, :] = jnp.where(lane < N_HEADS, -jnp.exp(gp_ref[0:1, :]) * softplus, _sigmoid(ab))


def _in_kernel(x_ref, mod_ref, nw_ref, win_ref, cw_ref, scw_ref, gp_ref,
               proj_ref, gb_ref, e_s, p_s, wm_ref, wab_ref, *, per_b):
    tm = x_ref.shape[0]

    @pl.when(pl.program_id(0) == 0)
    def _():
        n_gate = 2 * N_HEADS
        cb = D_DN // 2
        for r in range(N_MAIN // cb):
            src = r * cb + (n_gate if r * cb >= COL_SB else 0)
            wm_ref[:, r * cb:(r + 1) * cb] = win_ref[src:src + cb, :].T.astype(BF16)
        wab_ref[...] = win_ref[COL_SB:COL_SB + LANES, :].T.astype(BF16)

    @pl.when(pl.program_id(0) % per_b == 0)
    def _():
        e_s[:, 0:HALO, :] = jnp.zeros((e_s.shape[0], HALO, D_DN), F32)

    _interleave([_in_half(hf, x_ref, mod_ref, nw_ref, wm_ref, wab_ref, cw_ref, scw_ref, gp_ref,
                          proj_ref, gb_ref, e_s, p_s) for hf in range(2)], lead=5)
    e_s[:, 0:HALO, :] = e_s[:, tm:tm + HALO, :]


def _in_proj(x2, mod3, nw, w_in, cw, scw, gp, seq):
    t = x2.shape[0]
    tm = TM_IN
    per_b = seq // tm
    const = lambda i: (0, 0)
    return pl.pallas_call(
        functools.partial(_in_kernel, per_b=per_b),
        grid=(t // tm,),
        in_specs=[
            pl.BlockSpec((tm, D_MODEL), lambda i: (i, 0)),
            pl.BlockSpec((None, 6, D_MODEL), lambda i: (i // per_b, 0, 0)),
            pl.BlockSpec((1, D_MODEL), const),
            pl.BlockSpec(w_in.shape, const, pipeline_mode=pl.Buffered(1)),
            pl.BlockSpec((DN_CONV, N_QKV), const),
            pl.BlockSpec((SC_CONV, D_SC), const),
            pl.BlockSpec((8, LANES), const),
        ],
        out_specs=[
            pl.BlockSpec((tm, N_P2), lambda i: (i, 0)),
            pl.BlockSpec((tm, LANES), lambda i: (i, 0)),
        ],
        out_shape=[
            jax.ShapeDtypeStruct((t, N_P2), BF16),
            jax.ShapeDtypeStruct((t, LANES), F32),
        ],
        scratch_shapes=[
            pltpu.VMEM((4, HALO + tm, D_DN), F32),
            pltpu.VMEM((3, tm, D_DN), F32),
            pltpu.VMEM((D_MODEL, N_MAIN), BF16),
            pltpu.VMEM((D_MODEL, LANES), BF16),
        ],
        compiler_params=pltpu.CompilerParams(dimension_semantics=("arbitrary",),
                                             vmem_limit_bytes=VMEM_LIMIT),
        name="in_proj",
    )(x2, mod3, nw, w_in, cw, scw, gp)


def _mix_kernel(proj_ref, gb_ref, x_ref, mod_ref, dnw_ref, wout_ref, pw_ref, out_ref,
                o_s, state_s, aq_s, bb_s, op_s, gam_s, bdm_s, lvl_s, xs_s, mt_s, at_s, gw_s, rv_s):
    tq = x_ref.shape[0]
    t = pl.program_id(1)

    @pl.when(t == 0)
    def _():
        state_s[...] = jnp.zeros_like(state_s)

    q_ref = proj_ref.at[:, 0:D_DN]
    k_ref = proj_ref.at[:, D_DN:2 * D_DN]
    v_ref = proj_ref.at[:, 2 * D_DN:3 * D_DN]

    n_chunks = tq // CHUNK
    quad = N_HEADS * CHUNK
    row_q = lax.broadcasted_iota(jnp.int32, (CHUNK, quad), 0)
    lane_q = lax.broadcasted_iota(jnp.int32, (CHUNK, quad), 1)
    col_q = lane_q & (CHUNK - 1)
    head_q = lane_q >> 6
    tri_incl = row_q >= col_q
    tri_strict = row_q > col_q
    eye_q = row_q == col_q
    eye_f = eye_q.astype(F32)
    row_l = lax.broadcasted_iota(jnp.int32, (CHUNK, LANES), 0)

    def level_mask(r, c, l):
        return (((r >> l) & 1) == 1) & (((c >> l) & 1) == 0) & ((r >> (l + 1)) == (c >> (l + 1)))

    @pl.when((pl.program_id(0) == 0) & (t == 0))
    def _():
        r2 = lax.broadcasted_iota(jnp.int32, (quad, quad), 0)
        c2 = lax.broadcasted_iota(jnp.int32, (quad, quad), 1)
        same_head = (r2 >> 6) == (c2 >> 6)
        bdm_s[...] = same_head.astype(F32).astype(BF16)
        for l in range(1, 6):
            lm = same_head & level_mask(r2 & (CHUNK - 1), c2 & (CHUNK - 1), l)
            lvl_s[l - 1] = lm.astype(F32).astype(BF16)

    lm0_f = level_mask(row_q, col_q, 0).astype(F32)
    zero_blk = jnp.zeros((CHUNK, HEAD_DIM), BF16)
    zero_sq = jnp.zeros((HEAD_DIM, HEAD_DIM), BF16)

    def head_bcast(cols, width):
        if width == HEAD_DIM:
            return jnp.concatenate([jnp.broadcast_to(c_, (CHUNK, HEAD_DIM)) for c_ in cols], axis=1)
        full = [jnp.broadcast_to(c_, (CHUNK, quad)) for c_ in cols]
        return jnp.where(head_q == 0, full[0],
                         jnp.where(head_q == 1, full[1], jnp.where(head_q == 2, full[2], full[3])))

    def bd_wide(a):
        rows = []
        for h in range(N_HEADS):
            rows.append(jnp.concatenate(
                [a[:, j * HEAD_DIM:(j + 1) * HEAD_DIM] if j == h else zero_blk for j in range(N_HEADS)],
                axis=1))
        return jnp.concatenate(rows, axis=0)

    def tile4(a):
        return jnp.concatenate([a] * N_HEADS, axis=0)

    carried = {"state": None}

    def group(gi):
        chunks = range(gi * GROUP_CHUNKS, (gi + 1) * GROUP_CHUNKS)
        rows = {c: slice(c * CHUNK, (c + 1) * CHUNK) for c in chunks}

        for n, c in enumerate(chunks):
            bt = gb_ref[rows[c], :]
            gcum = bt
            for s in (1, 2, 4, 8, 16, 32):
                gcum = gcum + jnp.where(row_l >= s, pltpu.roll(gcum, s, axis=0), 0.0)
            g_cols = [gcum[:, h:h + 1] for h in range(N_HEADS)]
            b_cols = [bt[:, N_HEADS + h:N_HEADS + h + 1] for h in range(N_HEADS)]
            g_i = head_bcast(g_cols, CHUNK)
            b_i = head_bcast(b_cols, CHUNK)
            gw_s[c] = head_bcast(g_cols, HEAD_DIM)
            g_j = jnp.sum(jnp.where(eye_q, g_i, 0.0), axis=0, keepdims=True)
            b_j = jnp.sum(jnp.where(eye_q, b_i, 0.0), axis=0, keepdims=True)
            rv_s[c, 0:1, :] = b_j
            rv_s[c, 1:2, :] = b_j * jnp.exp(g_j)
            decay = jnp.exp(jnp.where(tri_incl, g_i - g_j, -1e30))

            k16 = k_ref[rows[c], :]
            kq = jnp.concatenate([k16, q_ref[rows[c], :]], axis=0)
            kq_k = lax.dot_general(kq, bd_wide(k16), (((1,), (1,)), ((), ())),
                                   preferred_element_type=F32)
            m_st = jnp.where(tri_strict, kq_k[:CHUNK] * b_i * decay, 0.0)
            at_s[c] = (kq_k[CHUNK:] * decay).astype(BF16)
            xs_s[c] = eye_f - m_st * lm0_f
            mt_s[c] = m_st.astype(BF16)
            if n % 2 == 1:
                yield

        for l in range(1, 6):
            for c in chunks:
                x = xs_s[c]
                x16 = x.astype(BF16)
                y = jnp.dot(x16, tile4(mt_s[c]) * lvl_s[l - 1], preferred_element_type=F32)
                z = jnp.dot(y.astype(BF16), tile4(x16) * bdm_s[...], preferred_element_type=F32)
                xs_s[c] = x - z
            yield

        for n, c in enumerate(chunks):
            k16 = k_ref[rows[c], :]
            q = q_ref[rows[c], :].astype(F32)
            k = k16.astype(F32)
            x = xs_s[c]
            u = jnp.dot((x * rv_s[c, 0:1, :]).astype(BF16), bd_wide(v_ref[rows[c], :]),
                        preferred_element_type=F32)
            w = jnp.dot((x * rv_s[c, 1:2, :]).astype(BF16), bd_wide(k16), preferred_element_type=F32)
            u16 = u.astype(BF16)
            w16 = w.astype(BF16)
            a_wu = jnp.dot(at_s[c], jnp.concatenate([bd_wide(w16), bd_wide(u16)], axis=1),
                           preferred_element_type=F32)
            g_wide = gw_s[c]
            g_last = g_wide[CHUNK - 1:CHUNK, :]
            aq_s[c, 2 * CHUNK:3 * CHUNK, :] = (q * jnp.exp(g_wide) - a_wu[:, :D_DN]).astype(BF16)
            op_s[c] = a_wu[:, D_DN:]
            gam_s[c] = jnp.broadcast_to(jnp.exp(g_last), (8, D_DN))
            k_dec = (k * jnp.exp(g_last - g_wide)).astype(BF16)
            for h in range(N_HEADS):
                hs = slice(h * HEAD_DIM, (h + 1) * HEAD_DIM)
                ab_h = _mm_tn(k_dec[:, hs], jnp.concatenate([w16[:, hs], u16[:, hs]], axis=1))
                aq_s[c, 0:2 * CHUNK, hs] = ab_h[:, :HEAD_DIM].astype(BF16)
                bb_s[c, :, hs] = ab_h[:, HEAD_DIM:]
            if n % 2 == 1:
                yield

        state = state_s[...] if gi == 0 else carried["state"]
        for c in chunks:
            s16 = state.astype(BF16)
            new_parts = []
            o_parts_c = []
            for p in range(N_HEADS // 2):
                ps = slice(p * 2 * HEAD_DIM, (p + 1) * 2 * HEAD_DIM)
                s_a = s16[:, (2 * p) * HEAD_DIM:(2 * p + 1) * HEAD_DIM]
                s_b = s16[:, (2 * p + 1) * HEAD_DIM:(2 * p + 2) * HEAD_DIM]
                bd = jnp.concatenate([jnp.concatenate([s_a, zero_sq], axis=1),
                                      jnp.concatenate([zero_sq, s_b], axis=1)], axis=0)
                r = jnp.dot(aq_s[c, :, ps], bd, preferred_element_type=F32)
                new_parts.append(gam_s[c, 0:1, ps] * state[:, ps] + bb_s[c, :, ps] - r[:2 * CHUNK])
                o_parts_c.append(op_s[c, :, ps] + r[2 * CHUNK:])
            state = jnp.concatenate(new_parts, axis=1)
            o_s[rows[c], :] = jnp.concatenate(o_parts_c, axis=1)
        carried["state"] = state
        if gi == n_chunks // GROUP_CHUNKS - 1:
            state_s[...] = state
        yield

        rg = slice(gi * GROUP_CHUNKS * CHUNK, (gi + 1) * GROUP_CHUNKS * CHUNK)
        zg = proj_ref[rg, COL_ZG:COL_ZG + D_DN].astype(F32)
        o_parts = []
        for h in range(N_HEADS):
            hs = slice(h * HEAD_DIM, (h + 1) * HEAD_DIM)
            oh = o_s[rg, hs]
            o_parts.append(oh * _rms_scale(oh) * dnw_ref[...] * zg[:, hs])
        o_dn = jnp.concatenate(o_parts, axis=1)
        y = (jnp.dot(o_dn.astype(BF16), wout_ref[0:D_DN, :], preferred_element_type=F32)
             + jnp.dot(proj_ref[rg, COL_YSC:COL_YSC + D_SC], wout_ref[D_DN:D_MODEL, :],
                       preferred_element_type=F32))
        out_ref[rg, :] = x_ref[rg, :] + mod_ref[2:3, :] * (y * _rms_scale(y) * pw_ref[...])

    _interleave([group(gi) for gi in range(n_chunks // GROUP_CHUNKS)], lead=2)


def _mixer(proj, gb, x2, mod3, dnw, w_out, pw, bsz, seq):
    tq = TQ_MIX
    per_b = seq // tq
    t = x2.shape[0]

    def tile(b, s):
        return (b * per_b + s, 0)

    const = lambda b, s: (0, 0)
    return pl.pallas_call(
        _mix_kernel,
        grid=(bsz, per_b),
        in_specs=[
            pl.BlockSpec((tq, N_P2), tile),
            pl.BlockSpec((tq, LANES), tile),
            pl.BlockSpec((tq, D_MODEL), tile),
            pl.BlockSpec((None, 6, D_MODEL), lambda b, s: (b, 0, 0)),
            pl.BlockSpec((1, HEAD_DIM), const),
            pl.BlockSpec((D_MODEL, D_MODEL), const),
            pl.BlockSpec((1, D_MODEL), const),
        ],
        out_specs=pl.BlockSpec((tq, D_MODEL), tile),
        out_shape=jax.ShapeDtypeStruct((t, D_MODEL), F32),
        scratch_shapes=[
            pltpu.VMEM((tq, D_DN), F32),
            pltpu.VMEM((HEAD_DIM, D_DN), F32),
            pltpu.VMEM((tq // CHUNK, 3 * CHUNK, D_DN), BF16),
            pltpu.VMEM((tq // CHUNK, HEAD_DIM, D_DN), F32),
            pltpu.VMEM((tq // CHUNK, CHUNK, D_DN), F32),
            pltpu.VMEM((tq // CHUNK, 8, D_DN), F32),
            pltpu.VMEM((N_HEADS * CHUNK, N_HEADS * CHUNK), BF16),
            pltpu.VMEM((5, N_HEADS * CHUNK, N_HEADS * CHUNK), BF16),
            pltpu.VMEM((tq // CHUNK, CHUNK, N_HEADS * CHUNK), F32),
            pltpu.VMEM((tq // CHUNK, CHUNK, N_HEADS * CHUNK), BF16),
            pltpu.VMEM((tq // CHUNK, CHUNK, N_HEADS * CHUNK), BF16),
            pltpu.VMEM((tq // CHUNK, CHUNK, D_DN), F32),
            pltpu.VMEM((tq // CHUNK, 8, N_HEADS * CHUNK), F32),
        ],
        compiler_params=pltpu.CompilerParams(dimension_semantics=("arbitrary", "arbitrary"),
                                             vmem_limit_bytes=VMEM_LIMIT),
        name="token_mixer",
    )(proj, gb, x2, mod3, dnw, w_out, pw)


def _ffn_kernel(x_ref, mod_ref, nw_ref, w1_hbm, w2_hbm, pw_ref, out_ref, w1_ref, w2_ref, stage_s, sem):
    @pl.when(pl.program_id(0) == 0)
    def _():
        pieces = []
        for j in range(D_FF // FF_CHUNK):
            cs = slice(j * FF_CHUNK, (j + 1) * FF_CHUNK)
            pieces.append((w1_hbm.at[:, cs], w1_ref.at[:, cs]))
        for j in range(D_FF // FF_CHUNK):
            cs = slice(j * FF_CHUNK, (j + 1) * FF_CHUNK)
            pieces.append((w2_hbm.at[cs, :], w2_ref.at[cs, :]))

        def copy(n):
            return pltpu.make_async_copy(pieces[n][0], stage_s.at[n % 2], sem.at[n % 2])

        copy(0).start()
        for n in range(len(pieces)):
            if n + 1 < len(pieces):
                copy(n + 1).start()
            copy(n).wait()
            pieces[n][1][...] = stage_s[n % 2].astype(BF16)

    n_parts = x_ref.shape[0] // FFN_ROWS

    def rows_gen(hf):
        rows = slice(hf * FFN_ROWS, (hf + 1) * FFN_ROWS)
        x = x_ref[rows, :]
        h = x * _rms_scale(x) * nw_ref[...]
        h = h * (1.0 + mod_ref[4:5, :]) + mod_ref[3:4, :]
        hb = h.astype(BF16)
        yield
        y = jnp.zeros(x.shape, F32)
        for j in range(D_FF // FF_CHUNK):
            a = jnp.dot(hb, w1_ref[:, j * FF_CHUNK:(j + 1) * FF_CHUNK], preferred_element_type=F32)
            a = jnp.square(jnp.maximum(a, 0.0)).astype(BF16)
            yield
            y = y + jnp.dot(a, w2_ref[j * FF_CHUNK:(j + 1) * FF_CHUNK, :], preferred_element_type=F32)
            yield
        out_ref[rows, :] = x_ref[rows, :] + mod_ref[5:6, :] * (y * _rms_scale(y) * pw_ref[...])

    _interleave([rows_gen(hf) for hf in range(n_parts)], lead=1)


def _ffn(x2, mod3, nw, w1, w2, pw, seq):
    t = x2.shape[0]
    tm = TM_FFN
    per_b = seq // tm
    return pl.pallas_call(
        _ffn_kernel,
        grid=(t // tm,),
        in_specs=[
            pl.BlockSpec((tm, D_MODEL), lambda i: (i, 0)),
            pl.BlockSpec((None, 6, D_MODEL), lambda i: (i // per_b, 0, 0)),
            pl.BlockSpec((1, D_MODEL), lambda i: (0, 0)),
            pl.BlockSpec(memory_space=pl.ANY),
            pl.BlockSpec(memory_space=pl.ANY),
            pl.BlockSpec((1, D_MODEL), lambda i: (0, 0)),
        ],
        out_specs=pl.BlockSpec((tm, D_MODEL), lambda i: (i, 0)),
        out_shape=jax.ShapeDtypeStruct((t, D_MODEL), F32),
        scratch_shapes=[
            pltpu.VMEM((D_MODEL, D_FF), BF16),
            pltpu.VMEM((D_FF, D_MODEL), BF16),
            pltpu.VMEM((2, FF_CHUNK, FF_CHUNK), F32),
            pltpu.SemaphoreType.DMA((2,)),
        ],
        compiler_params=pltpu.CompilerParams(dimension_semantics=("arbitrary",),
                                             vmem_limit_bytes=VMEM_LIMIT),
        name="ffn",
    )(x2, mod3, nw, w1, w2, pw)


def _layer(x2, c, w_ada, b_ada, pre_mix_w, post_mix_w, w_in, dn_conv_w, dn_a_log, dn_dt_bias,
           dn_norm_w, sc_conv_w, w_out, pre_ffn_w, post_ffn_w, w_ff1, w_ff2, bsz, seq):
    mod3 = _ada(c, w_ada, b_ada[None, :]).reshape(bsz, 6, D_MODEL)

    gp = jnp.zeros((8, LANES), F32)
    gp = gp.at[0, :N_HEADS].set(dn_a_log).at[1, :N_HEADS].set(dn_dt_bias)
    proj, gb = _in_proj(x2, mod3, pre_mix_w[None, :], w_in.T, dn_conv_w, sc_conv_w, gp, seq)

    x2 = _mixer(proj, gb, x2, mod3, dn_norm_w[None, :], w_out.astype(BF16), post_mix_w[None, :],
                bsz, seq)

    return _ffn(x2, mod3, pre_ffn_w[None, :], w_ff1, w_ff2, post_ffn_w[None, :], seq)


def kernel(x, c, w_ada, b_ada, pre_mix_norm_w, post_mix_norm_w, w_in, dn_conv_w, dn_a_log, dn_dt_bias, dn_norm_w, sc_conv_w, w_out, pre_ffn_norm_w, post_ffn_norm_w, w_ff1, w_ff2):
    bsz, seq, d = x.shape
    x2 = x.reshape(bsz * seq, d)
    for l in range(w_ada.shape[0]):
        x2 = _layer(x2, c, w_ada[l], b_ada[l], pre_mix_norm_w[l], post_mix_norm_w[l], w_in[l],
                    dn_conv_w[l], dn_a_log[l], dn_dt_bias[l], dn_norm_w[l], sc_conv_w[l], w_out[l],
                    pre_ffn_norm_w[l], post_ffn_norm_w[l], w_ff1[l], w_ff2[l], bsz, seq)
    return x2.reshape(bsz, seq, d)
```

```python
import functools

import jax
import jax.numpy as jnp
from jax import lax
from jax.experimental import pallas as pl
from jax.experimental.pallas import tpu as pltpu

F32 = jnp.float32
BF16 = jnp.bfloat16

D_MODEL = 1024
N_HEADS = 4
HEAD_DIM = 128
D_DN = N_HEADS * HEAD_DIM
D_SC = D_MODEL - D_DN
DN_CONV = 4
SC_CONV = 3
CHUNK = 64
D_FF = 4 * D_MODEL
EPS = 1e-6
N_QKV = 3 * D_DN
N_MAIN = 4 * D_DN + 3 * D_SC
COL_Z = N_QKV
COL_SB = 4 * D_DN
COL_SC = COL_SB + D_SC
COL_SH = COL_SC + D_SC
N_P2 = N_QKV + D_DN + D_SC
COL_ZG = N_QKV
COL_YSC = N_QKV + D_DN
LANES = 128
HALO = 8

TM_IN = 512
TQ_MIX = 1024
GROUP_CHUNKS = 4
TM_FFN = 512
FF_CHUNK = 1024
VMEM_LIMIT = 56 * 1024 * 1024


def _sigmoid(x):
    return 0.5 + 0.5 * jnp.tanh(0.5 * x)


def _mm_tn(a, b):
    return lax.dot_general(a, b, (((0,), (0,)), ((), ())), preferred_element_type=F32)


def _rms_scale(x):
    return lax.rsqrt(jnp.mean(x * x, axis=-1, keepdims=True) + EPS)


def _interleave(gens, lead):
    live = []
    pending = list(gens)
    rnd = 0
    while live or pending:
        if pending and rnd % lead == 0:
            live.append(pending.pop(0))
        for g in list(live):
            try:
                next(g)
            except StopIteration:
                live.remove(g)
        rnd += 1


def _ada_kernel(c_ref, w_ref, b_ref, o_ref):
    c = c_ref[...]
    ca = (c * _sigmoid(c)).astype(BF16)
    o_ref[...] = jnp.dot(ca, w_ref[...].astype(BF16), preferred_element_type=F32) + b_ref[...]


def _ada(c, w_ada, b_ada):
    bsz = c.shape[0]
    n = w_ada.shape[1]
    bn = 1024
    return pl.pallas_call(
        _ada_kernel,
        grid=(n // bn,),
        in_specs=[
            pl.BlockSpec((bsz, D_MODEL), lambda j: (0, 0)),
            pl.BlockSpec((D_MODEL, bn), lambda j: (0, j)),
            pl.BlockSpec((1, bn), lambda j: (0, j)),
        ],
        out_specs=pl.BlockSpec((bsz, bn), lambda j: (0, j)),
        out_shape=jax.ShapeDtypeStruct((bsz, n), F32),
        compiler_params=pltpu.CompilerParams(dimension_semantics=("arbitrary",),
                                             vmem_limit_bytes=VMEM_LIMIT),
        name="ada_mod",
    )(c, w_ada, b_ada)


def _causal_conv(ext, w_ref, c0, width):
    r = ext.shape[0] - HALO
    n = ext.shape[1]
    acc = w_ref[width - 1:width, c0:c0 + n] * ext[HALO:HALO + r, :]
    for s in range(1, width):
        acc = acc + w_ref[width - 1 - s:width - s, c0:c0 + n] * pltpu.roll(ext, s, axis=0)[HALO:HALO + r, :]
    return acc


def _in_half(hf, x_ref, mod_ref, nw_ref, wm_ref, wab_ref, cw_ref, scw_ref, gp_ref,
             proj_ref, gb_ref, e_s, p_s):
    half = x_ref.shape[0] // 2
    r0 = hf * half
    rows = slice(r0, r0 + half)
    erows = slice(HALO + r0, HALO + r0 + half)
    wrows = slice(r0, r0 + half + HALO)

    x = x_ref[rows, :]
    h = x * _rms_scale(x) * nw_ref[...]
    h = h * (1.0 + mod_ref[1:2, :]) + mod_ref[0:1, :]
    hb = h.astype(BF16)
    yield

    def proj(c0, n=D_DN):
        return jnp.dot(hb, wm_ref[:, c0:c0 + n], preferred_element_type=F32)

    for part in range(3):
        e_s[part, erows, :] = proj(part * D_DN)
        yield
    e_s[3, erows, :] = proj(COL_SC)
    p_s[0, rows, :] = proj(COL_Z)
    yield
    p_s[1, rows, :] = proj(COL_SH)
    p_s[2, rows, :] = proj(COL_SB)
    ab = jnp.dot(hb, wab_ref[...], preferred_element_type=F32)
    yield

    for part in range(3):
        c0 = part * D_DN
        acc = _causal_conv(e_s[part, wrows, :], cw_ref, c0, DN_CONV)
        y = acc * _sigmoid(acc)
        if part < 2:
            parts = []
            for hd in range(N_HEADS):
                yh = y[:, hd * HEAD_DIM:(hd + 1) * HEAD_DIM]
                nrm = lax.rsqrt(jnp.sum(yh * yh, axis=-1, keepdims=True) + EPS)
                if part == 0:
                    nrm = nrm * (HEAD_DIM ** -0.5)
                parts.append(yh * nrm)
            y = jnp.concatenate(parts, axis=1)
        proj_ref[rows, c0:c0 + D_DN] = y.astype(BF16)
        yield

    p_z = p_s[0, rows, :]
    proj_ref[rows, COL_ZG:COL_ZG + D_DN] = (p_z * _sigmoid(p_z)).astype(BF16)
    e_s[3, erows, :] = e_s[3, erows, :] * p_s[1, rows, :]
    conv = _causal_conv(e_s[3, wrows, :], scw_ref, 0, SC_CONV)
    proj_ref[rows, COL_YSC:COL_YSC + D_SC] = (p_s[2, rows, :] * conv).astype(BF16)

    xg = ab + gp_ref[1:2, :]
    softplus = jnp.maximum(xg, 0.0) + jnp.log1p(jnp.exp(-jnp.abs(xg)))
    lane = lax.broadcasted_iota(jnp.int32, ab.shape, 1)
    gb_ref[rows, :] = jnp.where(lane < N_HEADS, -jnp.exp(gp_ref[0:1, :]) * softplus, _sigmoid(ab))


def _in_kernel(x_ref, mod_ref, nw_ref, win_ref, cw_ref, scw_ref, gp_ref,
               proj_ref, gb_ref, e_s, p_s, wm_ref, wab_ref, *, per_b):
    tm = x_ref.shape[0]

    @pl.when(pl.program_id(0) == 0)
    def _():
        n_gate = 2 * N_HEADS
        cb = D_DN // 2
        for r in range(N_MAIN // cb):
            src = r * cb + (n_gate if r * cb >= COL_SB else 0)
            wm_ref[:, r * cb:(r + 1) * cb] = win_ref[src:src + cb, :].T.astype(BF16)
        wab_ref[...] = win_ref[COL_SB:COL_SB + LANES, :].T.astype(BF16)

    @pl.when(pl.program_id(0) % per_b == 0)
    def _():
        e_s[:, 0:HALO, :] = jnp.zeros((e_s.shape[0], HALO, D_DN), F32)

    _interleave([_in_half(hf, x_ref, mod_ref, nw_ref, wm_ref, wab_ref, cw_ref, scw_ref, gp_ref,
                          proj_ref, gb_ref, e_s, p_s) for hf in range(2)], lead=5)
    e_s[:, 0:HALO, :] = e_s[:, tm:tm + HALO, :]


def _in_proj(x2, mod3, nw, w_in, cw, scw, gp, seq):
    t = x2.shape[0]
    tm = TM_IN
    per_b = seq // tm
    const = lambda i: (0, 0)
    return pl.pallas_call(
        functools.partial(_in_kernel, per_b=per_b),
        grid=(t // tm,),
        in_specs=[
            pl.BlockSpec((tm, D_MODEL), lambda i: (i, 0)),
            pl.BlockSpec((None, 6, D_MODEL), lambda i: (i // per_b, 0, 0)),
            pl.BlockSpec((1, D_MODEL), const),
            pl.BlockSpec(w_in.shape, const, pipeline_mode=pl.Buffered(1)),
            pl.BlockSpec((DN_CONV, N_QKV), const),
            pl.BlockSpec((SC_CONV, D_SC), const),
            pl.BlockSpec((8, LANES), const),
        ],
        out_specs=[
            pl.BlockSpec((tm, N_P2), lambda i: (i, 0)),
            pl.BlockSpec((tm, LANES), lambda i: (i, 0)),
        ],
        out_shape=[
            jax.ShapeDtypeStruct((t, N_P2), BF16),
            jax.ShapeDtypeStruct((t, LANES), F32),
        ],
        scratch_shapes=[
            pltpu.VMEM((4, HALO + tm, D_DN), F32),
            pltpu.VMEM((3, tm, D_DN), F32),
            pltpu.VMEM((D_MODEL, N_MAIN), BF16),
            pltpu.VMEM((D_MODEL, LANES), BF16),
        ],
        compiler_params=pltpu.CompilerParams(dimension_semantics=("arbitrary",),
                                             vmem_limit_bytes=VMEM_LIMIT),
        name="in_proj",
    )(x2, mod3, nw, w_in, cw, scw, gp)


def _mix_kernel(proj_ref, gb_ref, x_ref, mod_ref, dnw_ref, wout_ref, pw_ref, out_ref,
                o_s, state_s, aq_s, bb_s, op_s, gam_s, bdm_s, bdx_s, lvl_s, xs_s, at_s, gw_s, rv_s):
    tq = x_ref.shape[0]
    t = pl.program_id(1)

    @pl.when(t == 0)
    def _():
        state_s[...] = jnp.zeros_like(state_s)

    q_ref = proj_ref.at[:, 0:D_DN]
    k_ref = proj_ref.at[:, D_DN:2 * D_DN]
    v_ref = proj_ref.at[:, 2 * D_DN:3 * D_DN]

    n_chunks = tq // CHUNK
    quad = N_HEADS * CHUNK
    row_q = lax.broadcasted_iota(jnp.int32, (CHUNK, quad), 0)
    lane_q = lax.broadcasted_iota(jnp.int32, (CHUNK, quad), 1)
    col_q = lane_q & (CHUNK - 1)
    head_q = lane_q >> 6
    tri_incl = row_q >= col_q
    tri_strict = row_q > col_q
    eye_q = row_q == col_q
    eye_f = eye_q.astype(F32)
    row_l = lax.broadcasted_iota(jnp.int32, (CHUNK, LANES), 0)

    def level_mask(r, c, l):
        return (((r >> l) & 1) == 1) & (((c >> l) & 1) == 0) & ((r >> (l + 1)) == (c >> (l + 1)))

    @pl.when((pl.program_id(0) == 0) & (t == 0))
    def _():
        bdm_s[...] = jnp.zeros_like(bdm_s)
        bdx_s[...] = jnp.zeros_like(bdx_s)
        for l in range(1, 6):
            lvl_s[l - 1] = level_mask(row_q, col_q, l).astype(F32)

    def put_diag(dst, a16):
        for h in range(N_HEADS):
            hq = slice(h * CHUNK, (h + 1) * CHUNK)
            dst[hq, hq] = a16[:, hq]

    lm0_f = level_mask(row_q, col_q, 0).astype(F32)
    zero_blk = jnp.zeros((CHUNK, HEAD_DIM), BF16)
    zero_sq = jnp.zeros((HEAD_DIM, HEAD_DIM), BF16)

    def head_bcast(cols, width):
        if width == HEAD_DIM:
            return jnp.concatenate([jnp.broadcast_to(c_, (CHUNK, HEAD_DIM)) for c_ in cols], axis=1)
        full = [jnp.broadcast_to(c_, (CHUNK, quad)) for c_ in cols]
        return jnp.where(head_q == 0, full[0],
                         jnp.where(head_q == 1, full[1], jnp.where(head_q == 2, full[2], full[3])))

    def bd_wide(a):
        rows = []
        for h in range(N_HEADS):
            rows.append(jnp.concatenate(
                [a[:, j * HEAD_DIM:(j + 1) * HEAD_DIM] if j == h else zero_blk for j in range(N_HEADS)],
                axis=1))
        return jnp.concatenate(rows, axis=0)

    def tile4(a):
        return jnp.concatenate([a] * N_HEADS, axis=0)

    carried = {}

    def group(gi):
        chunks = range(gi * GROUP_CHUNKS, (gi + 1) * GROUP_CHUNKS)
        rows = {c: slice(c * CHUNK, (c + 1) * CHUNK) for c in chunks}

        for n, c in enumerate(chunks):
            bt = gb_ref[rows[c], :]
            gcum = bt
            for s in (1, 2, 4, 8, 16, 32):
                gcum = gcum + jnp.where(row_l >= s, pltpu.roll(gcum, s, axis=0), 0.0)
            g_cols = [gcum[:, h:h + 1] for h in range(N_HEADS)]
            b_cols = [bt[:, N_HEADS + h:N_HEADS + h + 1] for h in range(N_HEADS)]
            g_i = head_bcast(g_cols, CHUNK)
            b_i = head_bcast(b_cols, CHUNK)
            gw_s[c] = head_bcast(g_cols, HEAD_DIM)
            g_j = jnp.sum(jnp.where(eye_q, g_i, 0.0), axis=0, keepdims=True)
            b_j = jnp.sum(jnp.where(eye_q, b_i, 0.0), axis=0, keepdims=True)
            rv_s[c, 0:1, :] = b_j
            rv_s[c, 1:2, :] = b_j * jnp.exp(g_j)
            decay = jnp.exp(jnp.where(tri_incl, g_i - g_j, -1e30))

            k16 = k_ref[rows[c], :]
            kq = jnp.concatenate([k16, q_ref[rows[c], :]], axis=0)
            kq_k = lax.dot_general(kq, bd_wide(k16), (((1,), (1,)), ((), ())),
                                   preferred_element_type=F32)
            m_st = jnp.where(tri_strict, kq_k[:CHUNK] * b_i * decay, 0.0)
            at_s[c] = (kq_k[CHUNK:] * decay).astype(BF16)
            xs_s[c] = eye_f - m_st * lm0_f
            put_diag(bdm_s.at[c], m_st.astype(BF16))
            if n % 2 == 1:
                yield

        for l in range(1, 6):
            for c in chunks:
                x = xs_s[c]
                x16 = x.astype(BF16)
                put_diag(bdx_s.at[c], x16)
                y = jnp.dot(x16, bdm_s[c], preferred_element_type=F32) * lvl_s[l - 1]
                z = jnp.dot(y.astype(BF16), bdx_s[c], preferred_element_type=F32)
                xs_s[c] = x - z
            yield

        for n, c in enumerate(chunks):
            k16 = k_ref[rows[c], :]
            q = q_ref[rows[c], :].astype(F32)
            k = k16.astype(F32)
            x = xs_s[c]
            u = jnp.dot((x * rv_s[c, 0:1, :]).astype(BF16), bd_wide(v_ref[rows[c], :]),
                        preferred_element_type=F32)
            w = jnp.dot((x * rv_s[c, 1:2, :]).astype(BF16), bd_wide(k16), preferred_element_type=F32)
            u16 = u.astype(BF16)
            w16 = w.astype(BF16)
            a_wu = jnp.dot(at_s[c], jnp.concatenate([bd_wide(w16), bd_wide(u16)], axis=1),
                           preferred_element_type=F32)
            g_wide = gw_s[c]
            g_last = g_wide[CHUNK - 1:CHUNK, :]
            aq_s[c, 2 * CHUNK:3 * CHUNK, :] = (q * jnp.exp(g_wide) - a_wu[:, :D_DN]).astype(BF16)
            op_s[c] = a_wu[:, D_DN:]
            gam_s[c] = jnp.broadcast_to(jnp.exp(g_last), (8, D_DN))
            k_dec = (k * jnp.exp(g_last - g_wide)).astype(BF16)
            for h in range(N_HEADS):
                hs = slice(h * HEAD_DIM, (h + 1) * HEAD_DIM)
                ab_h = _mm_tn(k_dec[:, hs], jnp.concatenate([w16[:, hs], u16[:, hs]], axis=1))
                aq_s[c, 0:2 * CHUNK, hs] = ab_h[:, :HEAD_DIM].astype(BF16)
                bb_s[c, :, hs] = ab_h[:, HEAD_DIM:]
            if n % 2 == 1:
                yield

        state = state_s[...] if gi == 0 else carried.pop("state")
        for c in chunks:
            s16 = state.astype(BF16)
            new_parts = []
            o_parts_c = []
            for p in range(N_HEADS // 2):
                ps = slice(p * 2 * HEAD_DIM, (p + 1) * 2 * HEAD_DIM)
                s_a = s16[:, (2 * p) * HEAD_DIM:(2 * p + 1) * HEAD_DIM]
                s_b = s16[:, (2 * p + 1) * HEAD_DIM:(2 * p + 2) * HEAD_DIM]
                bd = jnp.concatenate([jnp.concatenate([s_a, zero_sq], axis=1),
                                      jnp.concatenate([zero_sq, s_b], axis=1)], axis=0)
                r = jnp.dot(aq_s[c, :, ps], bd, preferred_element_type=F32)
                new_parts.append(gam_s[c, 0:1, ps] * state[:, ps] + bb_s[c, :, ps] - r[:2 * CHUNK])
                o_parts_c.append(op_s[c, :, ps] + r[2 * CHUNK:])
            state = jnp.concatenate(new_parts, axis=1)
            o_s[rows[c], :] = jnp.concatenate(o_parts_c, axis=1)
        carried["state"] = state
        if gi == n_chunks // GROUP_CHUNKS - 1:
            state_s[...] = state
        yield

        rg = slice(gi * GROUP_CHUNKS * CHUNK, (gi + 1) * GROUP_CHUNKS * CHUNK)
        zg = proj_ref[rg, COL_ZG:COL_ZG + D_DN].astype(F32)
        o_parts = []
        for h in range(N_HEADS):
            hs = slice(h * HEAD_DIM, (h + 1) * HEAD_DIM)
            oh = o_s[rg, hs]
            o_parts.append(oh * _rms_scale(oh) * dnw_ref[...] * zg[:, hs])
        o_dn = jnp.concatenate(o_parts, axis=1)
        y = (jnp.dot(o_dn.astype(BF16), wout_ref[0:D_DN, :], preferred_element_type=F32)
             + jnp.dot(proj_ref[rg, COL_YSC:COL_YSC + D_SC], wout_ref[D_DN:D_MODEL, :],
                       preferred_element_type=F32))
        out_ref[rg, :] = x_ref[rg, :] + mod_ref[2:3, :] * (y * _rms_scale(y) * pw_ref[...])

    _interleave([group(gi) for gi in range(n_chunks // GROUP_CHUNKS)], lead=2)


def _mixer(proj, gb, x2, mod3, dnw, w_out, pw, bsz, seq):
    tq = TQ_MIX
    per_b = seq // tq
    t = x2.shape[0]

    def tile(b, s):
        return (b * per_b + s, 0)

    const = lambda b, s: (0, 0)
    return pl.pallas_call(
        _mix_kernel,
        grid=(bsz, per_b),
        in_specs=[
            pl.BlockSpec((tq, N_P2), tile),
            pl.BlockSpec((tq, LANES), tile),
            pl.BlockSpec((tq, D_MODEL), tile),
            pl.BlockSpec((None, 6, D_MODEL), lambda b, s: (b, 0, 0)),
            pl.BlockSpec((1, HEAD_DIM), const),
            pl.BlockSpec((D_MODEL, D_MODEL), const),
            pl.BlockSpec((1, D_MODEL), const),
        ],
        out_specs=pl.BlockSpec((tq, D_MODEL), tile),
        out_shape=jax.ShapeDtypeStruct((t, D_MODEL), F32),
        scratch_shapes=[
            pltpu.VMEM((tq, D_DN), F32),
            pltpu.VMEM((HEAD_DIM, D_DN), F32),
            pltpu.VMEM((tq // CHUNK, 3 * CHUNK, D_DN), BF16),
            pltpu.VMEM((tq // CHUNK, HEAD_DIM, D_DN), F32),
            pltpu.VMEM((tq // CHUNK, CHUNK, D_DN), F32),
            pltpu.VMEM((tq // CHUNK, 8, D_DN), F32),
            pltpu.VMEM((tq // CHUNK, N_HEADS * CHUNK, N_HEADS * CHUNK), BF16),
            pltpu.VMEM((tq // CHUNK, N_HEADS * CHUNK, N_HEADS * CHUNK), BF16),
            pltpu.VMEM((5, CHUNK, N_HEADS * CHUNK), F32),
            pltpu.VMEM((tq // CHUNK, CHUNK, N_HEADS * CHUNK), F32),
            pltpu.VMEM((tq // CHUNK, CHUNK, N_HEADS * CHUNK), BF16),
            pltpu.VMEM((tq // CHUNK, CHUNK, D_DN), F32),
            pltpu.VMEM((tq // CHUNK, 8, N_HEADS * CHUNK), F32),
        ],
        compiler_params=pltpu.CompilerParams(dimension_semantics=("arbitrary", "arbitrary"),
                                             vmem_limit_bytes=VMEM_LIMIT),
        name="token_mixer",
    )(proj, gb, x2, mod3, dnw, w_out, pw)


def _ffn_kernel(x_ref, mod_ref, nw_ref, w1_hbm, w2_hbm, pw_ref, out_ref, w1_ref, w2_ref, stage_s, sem):
    @pl.when(pl.program_id(0) == 0)
    def _():
        pieces = []
        for j in range(D_FF // FF_CHUNK):
            cs = slice(j * FF_CHUNK, (j + 1) * FF_CHUNK)
            pieces.append((w1_hbm.at[:, cs], w1_ref.at[:, cs]))
        for j in range(D_FF // FF_CHUNK):
            cs = slice(j * FF_CHUNK, (j + 1) * FF_CHUNK)
            pieces.append((w2_hbm.at[cs, :], w2_ref.at[cs, :]))

        def copy(n):
            return pltpu.make_async_copy(pieces[n][0], stage_s.at[n % 2], sem.at[n % 2])

        copy(0).start()
        for n in range(len(pieces)):
            if n + 1 < len(pieces):
                copy(n + 1).start()
            copy(n).wait()
            pieces[n][1][...] = stage_s[n % 2].astype(BF16)

    x = x_ref[...]
    h = x * _rms_scale(x) * nw_ref[...]
    h = h * (1.0 + mod_ref[4:5, :]) + mod_ref[3:4, :]
    hb = h.astype(BF16)
    y = jnp.zeros(x.shape, F32)
    for j in range(D_FF // FF_CHUNK):
        a = jnp.dot(hb, w1_ref[:, j * FF_CHUNK:(j + 1) * FF_CHUNK], preferred_element_type=F32)
        a = jnp.square(jnp.maximum(a, 0.0)).astype(BF16)
        y = y + jnp.dot(a, w2_ref[j * FF_CHUNK:(j + 1) * FF_CHUNK, :], preferred_element_type=F32)
    out_ref[...] = x + mod_ref[5:6, :] * (y * _rms_scale(y) * pw_ref[...])


def _ffn(x2, mod3, nw, w1, w2, pw, seq):
    t = x2.shape[0]
    tm = TM_FFN
    per_b = seq // tm
    return pl.pallas_call(
        _ffn_kernel,
        grid=(t // tm,),
        in_specs=[
            pl.BlockSpec((tm, D_MODEL), lambda i: (i, 0)),
            pl.BlockSpec((None, 6, D_MODEL), lambda i: (i // per_b, 0, 0)),
            pl.BlockSpec((1, D_MODEL), lambda i: (0, 0)),
            pl.BlockSpec(memory_space=pl.ANY),
            pl.BlockSpec(memory_space=pl.ANY),
            pl.BlockSpec((1, D_MODEL), lambda i: (0, 0)),
        ],
        out_specs=pl.BlockSpec((tm, D_MODEL), lambda i: (i, 0)),
        out_shape=jax.ShapeDtypeStruct((t, D_MODEL), F32),
        scratch_shapes=[
            pltpu.VMEM((D_MODEL, D_FF), BF16),
            pltpu.VMEM((D_FF, D_MODEL), BF16),
            pltpu.VMEM((2, FF_CHUNK, FF_CHUNK), F32),
            pltpu.SemaphoreType.DMA((2,)),
        ],
        compiler_params=pltpu.CompilerParams(dimension_semantics=("arbitrary",),
                                             vmem_limit_bytes=VMEM_LIMIT),
        name="ffn",
    )(x2, mod3, nw, w1, w2, pw)


def _layer(x2, c, w_ada, b_ada, pre_mix_w, post_mix_w, w_in, dn_conv_w, dn_a_log, dn_dt_bias,
           dn_norm_w, sc_conv_w, w_out, pre_ffn_w, post_ffn_w, w_ff1, w_ff2, bsz, seq):
    mod3 = _ada(c, w_ada, b_ada[None, :]).reshape(bsz, 6, D_MODEL)

    gp = jnp.zeros((8, LANES), F32)
    gp = gp.at[0, :N_HEADS].set(dn_a_log).at[1, :N_HEADS].set(dn_dt_bias)
    proj, gb = _in_proj(x2, mod3, pre_mix_w[None, :], w_in.T, dn_conv_w, sc_conv_w, gp, seq)

    x2 = _mixer(proj, gb, x2, mod3, dn_norm_w[None, :], w_out.astype(BF16), post_mix_w[None, :],
                bsz, seq)

    return _ffn(x2, mod3, pre_ffn_w[None, :], w_ff1, w_ff2, post_ffn_w[None, :], seq)


def kernel(x, c, w_ada, b_ada, pre_mix_norm_w, post_mix_norm_w, w_in, dn_conv_w, dn_a_log, dn_dt_bias, dn_norm_w, sc_conv_w, w_out, pre_ffn_norm_w, post_ffn_norm_w, w_ff1, w_ff2):
    bsz, seq, d = x.shape
    x2 = x.reshape(bsz * seq, d)
    for l in range(w_ada.shape[0]):
        x2 = _layer(x2, c, w_ada[l], b_ada[l], pre_mix_norm_w[l], post_mix_norm_w[l], w_in[l],
                    dn_conv_w[l], dn_a_log[l], dn_dt_bias[l], dn_norm_w[l], sc_conv_w[l], w_out[l],
                    pre_ffn_norm_w[l], post_ffn_norm_w[l], w_ff1[l], w_ff2[l], bsz, seq)
    return x2.reshape(bsz, seq, d)
```

```python
import functools

import jax
import jax.numpy as jnp
from jax import lax
from jax.experimental import pallas as pl
from jax.experimental.pallas import tpu as pltpu

F32 = jnp.float32
BF16 = jnp.bfloat16

D_MODEL = 1024
N_HEADS = 4
HEAD_DIM = 128
D_DN = N_HEADS * HEAD_DIM
D_SC = D_MODEL - D_DN
DN_CONV = 4
SC_CONV = 3
CHUNK = 64
D_FF = 4 * D_MODEL
EPS = 1e-6
N_QKV = 3 * D_DN
N_MAIN = 4 * D_DN + 3 * D_SC
COL_Z = N_QKV
COL_SB = 4 * D_DN
COL_SC = COL_SB + D_SC
COL_SH = COL_SC + D_SC
N_P2 = N_QKV + D_DN + D_SC
COL_ZG = N_QKV
COL_YSC = N_QKV + D_DN
LANES = 128
HALO = 8

TM_IN = 1024
IN_ROWS = 256
TQ_MIX = 1024
GROUP_CHUNKS = 4
TM_FFN = 512
FF_CHUNK = 1024
VMEM_LIMIT = 56 * 1024 * 1024


def _sigmoid(x):
    return 0.5 + 0.5 * jnp.tanh(0.5 * x)


def _mm_tn(a, b):
    return lax.dot_general(a, b, (((0,), (0,)), ((), ())), preferred_element_type=F32)


def _rms_scale(x):
    return lax.rsqrt(jnp.mean(x * x, axis=-1, keepdims=True) + EPS)


def _interleave(gens, lead):
    live = []
    pending = list(gens)
    rnd = 0
    while live or pending:
        if pending and rnd % lead == 0:
            live.append(pending.pop(0))
        for g in list(live):
            try:
                next(g)
            except StopIteration:
                live.remove(g)
        rnd += 1


def _ada_kernel(c_ref, w_ref, b_ref, o_ref):
    c = c_ref[...]
    ca = (c * _sigmoid(c)).astype(BF16)
    o_ref[...] = jnp.dot(ca, w_ref[...].astype(BF16), preferred_element_type=F32) + b_ref[...]


def _ada(c, w_ada, b_ada):
    bsz = c.shape[0]
    n = w_ada.shape[1]
    bn = 1024
    return pl.pallas_call(
        _ada_kernel,
        grid=(n // bn,),
        in_specs=[
            pl.BlockSpec((bsz, D_MODEL), lambda j: (0, 0)),
            pl.BlockSpec((D_MODEL, bn), lambda j: (0, j)),
            pl.BlockSpec((1, bn), lambda j: (0, j)),
        ],
        out_specs=pl.BlockSpec((bsz, bn), lambda j: (0, j)),
        out_shape=jax.ShapeDtypeStruct((bsz, n), F32),
        compiler_params=pltpu.CompilerParams(dimension_semantics=("arbitrary",),
                                             vmem_limit_bytes=VMEM_LIMIT),
        name="ada_mod",
    )(c, w_ada, b_ada)


def _causal_conv(ext, w_ref, c0, width):
    r = ext.shape[0] - HALO
    n = ext.shape[1]
    acc = w_ref[width - 1:width, c0:c0 + n] * ext[HALO:HALO + r, :]
    for s in range(1, width):
        acc = acc + w_ref[width - 1 - s:width - s, c0:c0 + n] * pltpu.roll(ext, s, axis=0)[HALO:HALO + r, :]
    return acc


def _in_block(bi, x_ref, mod_ref, nw_ref, wm_ref, wab_ref, cw_ref, scw_ref, gp_ref,
              proj_ref, gb_ref, e_s, p_s):
    half = IN_ROWS
    r0 = bi * half
    rows = slice(r0, r0 + half)
    erows = slice(HALO + r0, HALO + r0 + half)
    wrows = slice(r0, r0 + half + HALO)

    x = x_ref[rows, :]
    h = x * _rms_scale(x) * nw_ref[...]
    h = h * (1.0 + mod_ref[1:2, :]) + mod_ref[0:1, :]
    hb = h.astype(BF16)
    yield

    def proj(c0, n=D_DN):
        return jnp.dot(hb, wm_ref[:, c0:c0 + n], preferred_element_type=F32)

    for part in range(3):
        e_s[part, erows, :] = proj(part * D_DN)
        yield
    e_s[3, erows, :] = proj(COL_SC)
    p_s[0, rows, :] = proj(COL_Z)
    yield
    p_s[1, rows, :] = proj(COL_SH)
    p_s[2, rows, :] = proj(COL_SB)
    ab = jnp.dot(hb, wab_ref[...], preferred_element_type=F32)
    yield

    for part in range(3):
        c0 = part * D_DN
        acc = _causal_conv(e_s[part, wrows, :], cw_ref, c0, DN_CONV)
        y = acc * _sigmoid(acc)
        if part < 2:
            parts = []
            for hd in range(N_HEADS):
                yh = y[:, hd * HEAD_DIM:(hd + 1) * HEAD_DIM]
                nrm = lax.rsqrt(jnp.sum(yh * yh, axis=-1, keepdims=True) + EPS)
                if part == 0:
                    nrm = nrm * (HEAD_DIM ** -0.5)
                parts.append(yh * nrm)
            y = jnp.concatenate(parts, axis=1)
        proj_ref[rows, c0:c0 + D_DN] = y.astype(BF16)
        yield

    p_z = p_s[0, rows, :]
    proj_ref[rows, COL_ZG:COL_ZG + D_DN] = (p_z * _sigmoid(p_z)).astype(BF16)
    e_s[3, erows, :] = e_s[3, erows, :] * p_s[1, rows, :]
    conv = _causal_conv(e_s[3, wrows, :], scw_ref, 0, SC_CONV)
    proj_ref[rows, COL_YSC:COL_YSC + D_SC] = (p_s[2, rows, :] * conv).astype(BF16)

    xg = ab + gp_ref[1:2, :]
    softplus = jnp.maximum(xg, 0.0) + jnp.log1p(jnp.exp(-jnp.abs(xg)))
    lane = lax.broadcasted_iota(jnp.int32, ab.shape, 1)
    gb_ref[rows, :] = jnp.where(lane < N_HEADS, -jnp.exp(gp_ref[0:1, :]) * softplus, _sigmoid(ab))


def _in_kernel(x_ref, mod_ref, nw_ref, win_hbm, cw_ref, scw_ref, gp_ref,
               proj_ref, gb_ref, e_s, p_s, wm_ref, wab_ref, stage_s, sem, *, per_b):
    tm = x_ref.shape[0]

    @pl.when(pl.program_id(0) == 0)
    def _():
        n_gate = 2 * N_HEADS
        cb = stage_s.shape[1]
        pieces = []
        for r in range(N_MAIN // cb):
            src = r * cb + (n_gate if r * cb >= COL_SB else 0)
            pieces.append((src, cb, wm_ref.at[:, r * cb:(r + 1) * cb]))
        pieces.append((COL_SB, LANES, wab_ref))

        def copy(n):
            src, rows, _ = pieces[n]
            return pltpu.make_async_copy(win_hbm.at[src:src + rows, :], stage_s.at[n % 2, 0:rows, :],
                                         sem.at[n % 2])

        copy(0).start()
        for n in range(len(pieces)):
            if n + 1 < len(pieces):
                copy(n + 1).start()
            copy(n).wait()
            rows = pieces[n][1]
            pieces[n][2][...] = stage_s[n % 2, 0:rows, :].T.astype(BF16)

    @pl.when(pl.program_id(0) % per_b == 0)
    def _():
        e_s[:, 0:HALO, :] = jnp.zeros((e_s.shape[0], HALO, D_DN), F32)

    _interleave([_in_block(bi, x_ref, mod_ref, nw_ref, wm_ref, wab_ref, cw_ref, scw_ref, gp_ref,
                           proj_ref, gb_ref, e_s, p_s) for bi in range(tm // IN_ROWS)], lead=5)
    e_s[:, 0:HALO, :] = e_s[:, tm:tm + HALO, :]


def _in_proj(x2, mod3, nw, w_in, cw, scw, gp, seq):
    t = x2.shape[0]
    tm = TM_IN
    per_b = seq // tm
    const = lambda i: (0, 0)
    return pl.pallas_call(
        functools.partial(_in_kernel, per_b=per_b),
        grid=(t // tm,),
        in_specs=[
            pl.BlockSpec((tm, D_MODEL), lambda i: (i, 0)),
            pl.BlockSpec((None, 6, D_MODEL), lambda i: (i // per_b, 0, 0)),
            pl.BlockSpec((1, D_MODEL), const),
            pl.BlockSpec(memory_space=pl.ANY),
            pl.BlockSpec((DN_CONV, N_QKV), const),
            pl.BlockSpec((SC_CONV, D_SC), const),
            pl.BlockSpec((8, LANES), const),
        ],
        out_specs=[
            pl.BlockSpec((tm, N_P2), lambda i: (i, 0)),
            pl.BlockSpec((tm, LANES), lambda i: (i, 0)),
        ],
        out_shape=[
            jax.ShapeDtypeStruct((t, N_P2), BF16),
            jax.ShapeDtypeStruct((t, LANES), F32),
        ],
        scratch_shapes=[
            pltpu.VMEM((4, HALO + tm, D_DN), F32),
            pltpu.VMEM((3, tm, D_DN), F32),
            pltpu.VMEM((D_MODEL, N_MAIN), BF16),
            pltpu.VMEM((D_MODEL, LANES), BF16),
            pltpu.VMEM((2, D_DN // 2, D_MODEL), F32),
            pltpu.SemaphoreType.DMA((2,)),
        ],
        compiler_params=pltpu.CompilerParams(dimension_semantics=("arbitrary",),
                                             vmem_limit_bytes=VMEM_LIMIT),
        name="in_proj",
    )(x2, mod3, nw, w_in, cw, scw, gp)


def _mix_kernel(proj_ref, gb_ref, x_ref, mod_ref, dnw_ref, wout_ref, pw_ref, out_ref,
                o_s, state_s, aq_s, bb_s, op_s, gam_s, bdm_s, bdx_s, lvl_s, xs_s, at_s, gw_s, rv_s):
    tq = x_ref.shape[0]
    t = pl.program_id(1)

    @pl.when(t == 0)
    def _():
        state_s[...] = jnp.zeros_like(state_s)

    q_ref = proj_ref.at[:, 0:D_DN]
    k_ref = proj_ref.at[:, D_DN:2 * D_DN]
    v_ref = proj_ref.at[:, 2 * D_DN:3 * D_DN]

    n_chunks = tq // CHUNK
    quad = N_HEADS * CHUNK
    row_q = lax.broadcasted_iota(jnp.int32, (CHUNK, quad), 0)
    lane_q = lax.broadcasted_iota(jnp.int32, (CHUNK, quad), 1)
    col_q = lane_q & (CHUNK - 1)
    head_q = lane_q >> 6
    tri_incl = row_q >= col_q
    tri_strict = row_q > col_q
    eye_q = row_q == col_q
    eye_f = eye_q.astype(F32)
    row_l = lax.broadcasted_iota(jnp.int32, (CHUNK, LANES), 0)

    def level_mask(r, c, l):
        return (((r >> l) & 1) == 1) & (((c >> l) & 1) == 0) & ((r >> (l + 1)) == (c >> (l + 1)))

    @pl.when((pl.program_id(0) == 0) & (t == 0))
    def _():
        bdm_s[...] = jnp.zeros_like(bdm_s)
        bdx_s[...] = jnp.zeros_like(bdx_s)
        for l in range(1, 6):
            lvl_s[l - 1] = level_mask(row_q, col_q, l).astype(F32)

    def put_diag(dst, a16):
        for h in range(N_HEADS):
            hq = slice(h * CHUNK, (h + 1) * CHUNK)
            dst[hq, hq] = a16[:, hq]

    lm0_f = level_mask(row_q, col_q, 0).astype(F32)
    zero_blk = jnp.zeros((CHUNK, HEAD_DIM), BF16)
    zero_sq = jnp.zeros((HEAD_DIM, HEAD_DIM), BF16)

    def head_bcast(cols, width):
        if width == HEAD_DIM:
            return jnp.concatenate([jnp.broadcast_to(c_, (CHUNK, HEAD_DIM)) for c_ in cols], axis=1)
        full = [jnp.broadcast_to(c_, (CHUNK, quad)) for c_ in cols]
        return jnp.where(head_q == 0, full[0],
                         jnp.where(head_q == 1, full[1], jnp.where(head_q == 2, full[2], full[3])))

    def bd_wide(a):
        rows = []
        for h in range(N_HEADS):
            rows.append(jnp.concatenate(
                [a[:, j * HEAD_DIM:(j + 1) * HEAD_DIM] if j == h else zero_blk for j in range(N_HEADS)],
                axis=1))
        return jnp.concatenate(rows, axis=0)

    def tile4(a):
        return jnp.concatenate([a] * N_HEADS, axis=0)

    carried = {}

    def group(gi):
        chunks = range(gi * GROUP_CHUNKS, (gi + 1) * GROUP_CHUNKS)
        rows = {c: slice(c * CHUNK, (c + 1) * CHUNK) for c in chunks}

        for n, c in enumerate(chunks):
            bt = gb_ref[rows[c], :]
            gcum = bt
            for s in (1, 2, 4, 8, 16, 32):
                gcum = gcum + jnp.where(row_l >= s, pltpu.roll(gcum, s, axis=0), 0.0)
            g_cols = [gcum[:, h:h + 1] for h in range(N_HEADS)]
            b_cols = [bt[:, N_HEADS + h:N_HEADS + h + 1] for h in range(N_HEADS)]
            g_i = head_bcast(g_cols, CHUNK)
            b_i = head_bcast(b_cols, CHUNK)
            gw_s[c] = head_bcast(g_cols, HEAD_DIM)
            g_j = jnp.sum(jnp.where(eye_q, g_i, 0.0), axis=0, keepdims=True)
            b_j = jnp.sum(jnp.where(eye_q, b_i, 0.0), axis=0, keepdims=True)
            rv_s[c, 0:1, :] = b_j
            rv_s[c, 1:2, :] = b_j * jnp.exp(g_j)
            decay = jnp.exp(jnp.where(tri_incl, g_i - g_j, -1e30))

            k16 = k_ref[rows[c], :]
            kq = jnp.concatenate([k16, q_ref[rows[c], :]], axis=0)
            kq_k = lax.dot_general(kq, bd_wide(k16), (((1,), (1,)), ((), ())),
                                   preferred_element_type=F32)
            m_st = jnp.where(tri_strict, kq_k[:CHUNK] * b_i * decay, 0.0)
            at_s[c] = (kq_k[CHUNK:] * decay).astype(BF16)
            xs_s[c] = eye_f - m_st * lm0_f
            put_diag(bdm_s.at[c], m_st.astype(BF16))
            if n % 2 == 1:
                yield

        for l in range(1, 6):
            for c in chunks:
                x = xs_s[c]
                x16 = x.astype(BF16)
                put_diag(bdx_s.at[c], x16)
                y = jnp.dot(x16, bdm_s[c], preferred_element_type=F32) * lvl_s[l - 1]
                z = jnp.dot(y.astype(BF16), bdx_s[c], preferred_element_type=F32)
                xs_s[c] = x - z
            yield

        for n, c in enumerate(chunks):
            k16 = k_ref[rows[c], :]
            q = q_ref[rows[c], :].astype(F32)
            k = k16.astype(F32)
            x = xs_s[c]
            u = jnp.dot((x * rv_s[c, 0:1, :]).astype(BF16), bd_wide(v_ref[rows[c], :]),
                        preferred_element_type=F32)
            w = jnp.dot((x * rv_s[c, 1:2, :]).astype(BF16), bd_wide(k16), preferred_element_type=F32)
            u16 = u.astype(BF16)
            w16 = w.astype(BF16)
            a_wu = jnp.dot(at_s[c], jnp.concatenate([bd_wide(w16), bd_wide(u16)], axis=1),
                           preferred_element_type=F32)
            g_wide = gw_s[c]
            g_last = g_wide[CHUNK - 1:CHUNK, :]
            aq_s[c, 2 * CHUNK:3 * CHUNK, :] = (q * jnp.exp(g_wide) - a_wu[:, :D_DN]).astype(BF16)
            op_s[c] = a_wu[:, D_DN:]
            gam_s[c] = jnp.broadcast_to(jnp.exp(g_last), (8, D_DN))
            k_dec = (k * jnp.exp(g_last - g_wide)).astype(BF16)
            for h in range(N_HEADS):
                hs = slice(h * HEAD_DIM, (h + 1) * HEAD_DIM)
                ab_h = _mm_tn(k_dec[:, hs], jnp.concatenate([w16[:, hs], u16[:, hs]], axis=1))
                aq_s[c, 0:2 * CHUNK, hs] = ab_h[:, :HEAD_DIM].astype(BF16)
                bb_s[c, :, hs] = ab_h[:, HEAD_DIM:]
            if n % 2 == 1:
                yield

        state = state_s[...] if gi == 0 else carried.pop("state")
        for c in chunks:
            s16 = state.astype(BF16)
            new_parts = []
            o_parts_c = []
            for p in range(N_HEADS // 2):
                ps = slice(p * 2 * HEAD_DIM, (p + 1) * 2 * HEAD_DIM)
                s_a = s16[:, (2 * p) * HEAD_DIM:(2 * p + 1) * HEAD_DIM]
                s_b = s16[:, (2 * p + 1) * HEAD_DIM:(2 * p + 2) * HEAD_DIM]
                bd = jnp.concatenate([jnp.concatenate([s_a, zero_sq], axis=1),
                                      jnp.concatenate([zero_sq, s_b], axis=1)], axis=0)
                r = jnp.dot(aq_s[c, :, ps], bd, preferred_element_type=F32)
                new_parts.append(gam_s[c, 0:1, ps] * state[:, ps] + bb_s[c, :, ps] - r[:2 * CHUNK])
                o_parts_c.append(op_s[c, :, ps] + r[2 * CHUNK:])
            state = jnp.concatenate(new_parts, axis=1)
            o_s[rows[c], :] = jnp.concatenate(o_parts_c, axis=1)
        carried["state"] = state
        if gi == n_chunks // GROUP_CHUNKS - 1:
            state_s[...] = state
        yield

        rg = slice(gi * GROUP_CHUNKS * CHUNK, (gi + 1) * GROUP_CHUNKS * CHUNK)
        zg = proj_ref[rg, COL_ZG:COL_ZG + D_DN].astype(F32)
        o_parts = []
        for h in range(N_HEADS):
            hs = slice(h * HEAD_DIM, (h + 1) * HEAD_DIM)
            oh = o_s[rg, hs]
            o_parts.append(oh * _rms_scale(oh) * dnw_ref[...] * zg[:, hs])
        o_dn = jnp.concatenate(o_parts, axis=1)
        y = (jnp.dot(o_dn.astype(BF16), wout_ref[0:D_DN, :], preferred_element_type=F32)
             + jnp.dot(proj_ref[rg, COL_YSC:COL_YSC + D_SC], wout_ref[D_DN:D_MODEL, :],
                       preferred_element_type=F32))
        out_ref[rg, :] = x_ref[rg, :] + mod_ref[2:3, :] * (y * _rms_scale(y) * pw_ref[...])

    _interleave([group(gi) for gi in range(n_chunks // GROUP_CHUNKS)], lead=2)


def _mixer(proj, gb, x2, mod3, dnw, w_out, pw, bsz, seq):
    tq = TQ_MIX
    per_b = seq // tq
    t = x2.shape[0]

    def tile(b, s):
        return (b * per_b + s, 0)

    const = lambda b, s: (0, 0)
    return pl.pallas_call(
        _mix_kernel,
        grid=(bsz, per_b),
        in_specs=[
            pl.BlockSpec((tq, N_P2), tile),
            pl.BlockSpec((tq, LANES), tile),
            pl.BlockSpec((tq, D_MODEL), tile),
            pl.BlockSpec((None, 6, D_MODEL), lambda b, s: (b, 0, 0)),
            pl.BlockSpec((1, HEAD_DIM), const),
            pl.BlockSpec((D_MODEL, D_MODEL), const),
            pl.BlockSpec((1, D_MODEL), const),
        ],
        out_specs=pl.BlockSpec((tq, D_MODEL), tile),
        out_shape=jax.ShapeDtypeStruct((t, D_MODEL), F32),
        scratch_shapes=[
            pltpu.VMEM((tq, D_DN), F32),
            pltpu.VMEM((HEAD_DIM, D_DN), F32),
            pltpu.VMEM((tq // CHUNK, 3 * CHUNK, D_DN), BF16),
            pltpu.VMEM((tq // CHUNK, HEAD_DIM, D_DN), F32),
            pltpu.VMEM((tq // CHUNK, CHUNK, D_DN), F32),
            pltpu.VMEM((tq // CHUNK, 8, D_DN), F32),
            pltpu.VMEM((tq // CHUNK, N_HEADS * CHUNK, N_HEADS * CHUNK), BF16),
            pltpu.VMEM((tq // CHUNK, N_HEADS * CHUNK, N_HEADS * CHUNK), BF16),
            pltpu.VMEM((5, CHUNK, N_HEADS * CHUNK), F32),
            pltpu.VMEM((tq // CHUNK, CHUNK, N_HEADS * CHUNK), F32),
            pltpu.VMEM((tq // CHUNK, CHUNK, N_HEADS * CHUNK), BF16),
            pltpu.VMEM((tq // CHUNK, CHUNK, D_DN), F32),
            pltpu.VMEM((tq // CHUNK, 8, N_HEADS * CHUNK), F32),
        ],
        compiler_params=pltpu.CompilerParams(dimension_semantics=("arbitrary", "arbitrary"),
                                             vmem_limit_bytes=VMEM_LIMIT),
        name="token_mixer",
    )(proj, gb, x2, mod3, dnw, w_out, pw)


def _ffn_kernel(x_ref, mod_ref, nw_ref, w1_hbm, w2_hbm, pw_ref, out_ref, w1_ref, w2_ref, stage_s, sem):
    n_ff = D_FF // FF_CHUNK
    pieces = []
    for j in range(n_ff):
        cs = slice(j * FF_CHUNK, (j + 1) * FF_CHUNK)
        pieces.append((w1_hbm.at[:, cs], w1_ref.at[:, cs]))
        pieces.append((w2_hbm.at[cs, :], w2_ref.at[cs, :]))

    def copy(n):
        return pltpu.make_async_copy(pieces[n][0], stage_s.at[n % 2], sem.at[n % 2])

    def fetch(n):
        if n + 1 < len(pieces):
            copy(n + 1).start()
        copy(n).wait()
        pieces[n][1][...] = stage_s[n % 2].astype(BF16)

    def body(loading):
        x = x_ref[...]
        h = x * _rms_scale(x) * nw_ref[...]
        h = h * (1.0 + mod_ref[4:5, :]) + mod_ref[3:4, :]
        hb = h.astype(BF16)
        y = jnp.zeros(x.shape, F32)
        for j in range(n_ff):
            if loading:
                fetch(2 * j)
            a = jnp.dot(hb, w1_ref[:, j * FF_CHUNK:(j + 1) * FF_CHUNK], preferred_element_type=F32)
            a = jnp.square(jnp.maximum(a, 0.0)).astype(BF16)
            if loading:
                fetch(2 * j + 1)
            y = y + jnp.dot(a, w2_ref[j * FF_CHUNK:(j + 1) * FF_CHUNK, :], preferred_element_type=F32)
        out_ref[...] = x + mod_ref[5:6, :] * (y * _rms_scale(y) * pw_ref[...])

    @pl.when(pl.program_id(0) == 0)
    def _():
        copy(0).start()
        body(loading=True)

    @pl.when(pl.program_id(0) != 0)
    def _():
        body(loading=False)


def _ffn(x2, mod3, nw, w1, w2, pw, seq):
    t = x2.shape[0]
    tm = TM_FFN
    per_b = seq // tm
    return pl.pallas_call(
        _ffn_kernel,
        grid=(t // tm,),
        in_specs=[
            pl.BlockSpec((tm, D_MODEL), lambda i: (i, 0)),
            pl.BlockSpec((None, 6, D_MODEL), lambda i: (i // per_b, 0, 0)),
            pl.BlockSpec((1, D_MODEL), lambda i: (0, 0)),
            pl.BlockSpec(memory_space=pl.ANY),
            pl.BlockSpec(memory_space=pl.ANY),
            pl.BlockSpec((1, D_MODEL), lambda i: (0, 0)),
        ],
        out_specs=pl.BlockSpec((tm, D_MODEL), lambda i: (i, 0)),
        out_shape=jax.ShapeDtypeStruct((t, D_MODEL), F32),
        scratch_shapes=[
            pltpu.VMEM((D_MODEL, D_FF), BF16),
            pltpu.VMEM((D_FF, D_MODEL), BF16),
            pltpu.VMEM((2, FF_CHUNK, FF_CHUNK), F32),
            pltpu.SemaphoreType.DMA((2,)),
        ],
        compiler_params=pltpu.CompilerParams(dimension_semantics=("arbitrary",),
                                             vmem_limit_bytes=VMEM_LIMIT),
        name="ffn",
    )(x2, mod3, nw, w1, w2, pw)


def _layer(x2, c, w_ada, b_ada, pre_mix_w, post_mix_w, w_in, dn_conv_w, dn_a_log, dn_dt_bias,
           dn_norm_w, sc_conv_w, w_out, pre_ffn_w, post_ffn_w, w_ff1, w_ff2, bsz, seq):
    mod3 = _ada(c, w_ada, b_ada[None, :]).reshape(bsz, 6, D_MODEL)

    gp = jnp.zeros((8, LANES), F32)
    gp = gp.at[0, :N_HEADS].set(dn_a_log).at[1, :N_HEADS].set(dn_dt_bias)
    proj, gb = _in_proj(x2, mod3, pre_mix_w[None, :], w_in.T, dn_conv_w, sc_conv_w, gp, seq)

    x2 = _mixer(proj, gb, x2, mod3, dn_norm_w[None, :], w_out.astype(BF16), post_mix_w[None, :],
                bsz, seq)

    return _ffn(x2, mod3, pre_ffn_w[None, :], w_ff1, w_ff2, post_ffn_w[None, :], seq)


def kernel(x, c, w_ada, b_ada, pre_mix_norm_w, post_mix_norm_w, w_in, dn_conv_w, dn_a_log, dn_dt_bias, dn_norm_w, sc_conv_w, w_out, pre_ffn_norm_w, post_ffn_norm_w, w_ff1, w_ff2):
    bsz, seq, d = x.shape
    x2 = x.reshape(bsz * seq, d)
    for l in range(w_ada.shape[0]):
        x2 = _layer(x2, c, w_ada[l], b_ada[l], pre_mix_norm_w[l], post_mix_norm_w[l], w_in[l],
                    dn_conv_w[l], dn_a_log[l], dn_dt_bias[l], dn_norm_w[l], sc_conv_w[l], w_out[l],
                    pre_ffn_norm_w[l], post_ffn_norm_w[l], w_ff1[l], w_ff2[l], bsz, seq)
    return x2.reshape(bsz, seq, d)
```

```python
import functools

import jax
import jax.numpy as jnp
from jax import lax
from jax.experimental import pallas as pl
from jax.experimental.pallas import tpu as pltpu

F32 = jnp.float32
BF16 = jnp.bfloat16

D_MODEL = 1024
N_HEADS = 4
HEAD_DIM = 128
D_DN = N_HEADS * HEAD_DIM
D_SC = D_MODEL - D_DN
DN_CONV = 4
SC_CONV = 3
CHUNK = 64
D_FF = 4 * D_MODEL
EPS = 1e-6
N_QKV = 3 * D_DN
N_MAIN = 4 * D_DN + 3 * D_SC
COL_Z = N_QKV
COL_SB = 4 * D_DN
COL_SC = COL_SB + D_SC
COL_SH = COL_SC + D_SC
N_P2 = N_QKV + D_DN + D_SC
COL_ZG = N_QKV
COL_YSC = N_QKV + D_DN
LANES = 128
HALO = 8

TM_IN = 512
TQ_MIX = 1024
GROUP_CHUNKS = 4
TM_FFN = 512
FF_CHUNK = 1024
VMEM_LIMIT = 56 * 1024 * 1024


def _sigmoid(x):
    return 0.5 + 0.5 * jnp.tanh(0.5 * x)


def _mm_tn(a, b):
    return lax.dot_general(a, b, (((0,), (0,)), ((), ())), preferred_element_type=F32)


def _rms_scale(x):
    return lax.rsqrt(jnp.mean(x * x, axis=-1, keepdims=True) + EPS)


def _interleave(gens, lead):
    live = []
    pending = list(gens)
    rnd = 0
    while live or pending:
        if pending and rnd % lead == 0:
            live.append(pending.pop(0))
        for g in list(live):
            try:
                next(g)
            except StopIteration:
                live.remove(g)
        rnd += 1


def _ada_kernel(c_ref, w_ref, b_ref, o_ref):
    c = c_ref[...]
    ca = (c * _sigmoid(c)).astype(BF16)
    o_ref[...] = jnp.dot(ca, w_ref[...].astype(BF16), preferred_element_type=F32) + b_ref[...]


def _ada(c, w_ada, b_ada):
    bsz = c.shape[0]
    n = w_ada.shape[1]
    bn = 1024
    return pl.pallas_call(
        _ada_kernel,
        grid=(n // bn,),
        in_specs=[
            pl.BlockSpec((bsz, D_MODEL), lambda j: (0, 0)),
            pl.BlockSpec((D_MODEL, bn), lambda j: (0, j)),
            pl.BlockSpec((1, bn), lambda j: (0, j)),
        ],
        out_specs=pl.BlockSpec((bsz, bn), lambda j: (0, j)),
        out_shape=jax.ShapeDtypeStruct((bsz, n), F32),
        compiler_params=pltpu.CompilerParams(dimension_semantics=("arbitrary",),
                                             vmem_limit_bytes=VMEM_LIMIT),
        name="ada_mod",
    )(c, w_ada, b_ada)


def _causal_conv(ext, w, c0):
    width = w.shape[0]
    r = ext.shape[0] - HALO
    n = ext.shape[1]
    acc = w[width - 1:width, c0:c0 + n] * ext[HALO:HALO + r, :]
    for s in range(1, width):
        acc = acc + w[width - 1 - s:width - s, c0:c0 + n] * pltpu.roll(ext, s, axis=0)[HALO:HALO + r, :]
    return acc


def _silu_from_half(hx):
    return hx + hx * jnp.tanh(hx)


def _in_half(hf, x_ref, mod_ref, nw_ref, wm_ref, wab_ref, cw_ref, scw_ref, gp_ref,
             proj_ref, gb_ref, e_s, p_s):
    half = x_ref.shape[0] // 2
    r0 = hf * half
    rows = slice(r0, r0 + half)
    erows = slice(HALO + r0, HALO + r0 + half)
    wrows = slice(r0, r0 + half + HALO)

    x = x_ref[rows, :]
    h = x * _rms_scale(x) * (nw_ref[...] * (1.0 + mod_ref[1:2, :])) + mod_ref[0:1, :]
    hb = h.astype(BF16)
    half_cw = 0.5 * cw_ref[...]
    yield

    def proj(c0, n=D_DN):
        return jnp.dot(hb, wm_ref[:, c0:c0 + n], preferred_element_type=F32)

    for part in range(3):
        e_s[part, erows, :] = proj(part * D_DN)
        yield
    e_s[3, erows, :] = proj(COL_SC)
    p_s[0, rows, :] = proj(COL_Z)
    yield
    p_s[1, rows, :] = proj(COL_SH)
    p_s[2, rows, :] = proj(COL_SB)
    ab = jnp.dot(hb, wab_ref[...], preferred_element_type=F32)
    yield

    for part in range(3):
        c0 = part * D_DN
        y = _silu_from_half(_causal_conv(e_s[part, wrows, :], half_cw, c0))
        if part < 2:
            parts = []
            for hd in range(N_HEADS):
                yh = y[:, hd * HEAD_DIM:(hd + 1) * HEAD_DIM]
                nrm = lax.rsqrt(jnp.sum(yh * yh, axis=-1, keepdims=True) + EPS)
                if part == 0:
                    nrm = nrm * (HEAD_DIM ** -0.5)
                parts.append(yh * nrm)
            y = jnp.concatenate(parts, axis=1)
        proj_ref[rows, c0:c0 + D_DN] = y.astype(BF16)
        yield

    proj_ref[rows, COL_ZG:COL_ZG + D_DN] = _silu_from_half(0.5 * p_s[0, rows, :]).astype(BF16)
    e_s[3, erows, :] = e_s[3, erows, :] * p_s[1, rows, :]
    conv = _causal_conv(e_s[3, wrows, :], scw_ref[...], 0)
    proj_ref[rows, COL_YSC:COL_YSC + D_SC] = (p_s[2, rows, :] * conv).astype(BF16)

    xg = ab + gp_ref[1:2, :]
    softplus = jnp.maximum(xg, 0.0) + jnp.log1p(jnp.exp(-jnp.abs(xg)))
    lane = lax.broadcasted_iota(jnp.int32, ab.shape, 1)
    gb_ref[rows, :] = jnp.where(lane < N_HEADS, -jnp.exp(gp_ref[0:1, :]) * softplus, _sigmoid(ab))


def _in_kernel(x_ref, mod_ref, nw_ref, win_ref, cw_ref, scw_ref, gp_ref,
               proj_ref, gb_ref, e_s, p_s, wm_ref, wab_ref, *, per_b):
    tm = x_ref.shape[0]

    @pl.when(pl.program_id(0) == 0)
    def _():
        n_gate = 2 * N_HEADS
        cb = D_DN // 2
        for r in range(N_MAIN // cb):
            src = r * cb + (n_gate if r * cb >= COL_SB else 0)
            wm_ref[:, r * cb:(r + 1) * cb] = win_ref[src:src + cb, :].T.astype(BF16)
        wab_ref[...] = win_ref[COL_SB:COL_SB + LANES, :].T.astype(BF16)

    @pl.when(pl.program_id(0) % per_b == 0)
    def _():
        e_s[:, 0:HALO, :] = jnp.zeros((e_s.shape[0], HALO, D_DN), F32)

    _interleave([_in_half(hf, x_ref, mod_ref, nw_ref, wm_ref, wab_ref, cw_ref, scw_ref, gp_ref,
                          proj_ref, gb_ref, e_s, p_s) for hf in range(2)], lead=5)
    e_s[:, 0:HALO, :] = e_s[:, tm:tm + HALO, :]


def _in_proj(x2, mod3, nw, w_in, cw, scw, gp, seq):
    t = x2.shape[0]
    tm = TM_IN
    per_b = seq // tm
    const = lambda i: (0, 0)
    return pl.pallas_call(
        functools.partial(_in_kernel, per_b=per_b),
        grid=(t // tm,),
        in_specs=[
            pl.BlockSpec((tm, D_MODEL), lambda i: (i, 0)),
            pl.BlockSpec((None, 6, D_MODEL), lambda i: (i // per_b, 0, 0)),
            pl.BlockSpec((1, D_MODEL), const),
            pl.BlockSpec(w_in.shape, const, pipeline_mode=pl.Buffered(1)),
            pl.BlockSpec((DN_CONV, N_QKV), const),
            pl.BlockSpec((SC_CONV, D_SC), const),
            pl.BlockSpec((8, LANES), const),
        ],
        out_specs=[
            pl.BlockSpec((tm, N_P2), lambda i: (i, 0)),
            pl.BlockSpec((tm, LANES), lambda i: (i, 0)),
        ],
        out_shape=[
            jax.ShapeDtypeStruct((t, N_P2), BF16),
            jax.ShapeDtypeStruct((t, LANES), F32),
        ],
        scratch_shapes=[
            pltpu.VMEM((4, HALO + tm, D_DN), F32),
            pltpu.VMEM((3, tm, D_DN), F32),
            pltpu.VMEM((D_MODEL, N_MAIN), BF16),
            pltpu.VMEM((D_MODEL, LANES), BF16),
        ],
        compiler_params=pltpu.CompilerParams(dimension_semantics=("arbitrary",),
                                             vmem_limit_bytes=VMEM_LIMIT),
        name="in_proj",
    )(x2, mod3, nw, w_in, cw, scw, gp)


def _mix_kernel(proj_ref, gb_ref, x_ref, mod_ref, dnw_ref, wout_ref, pw_ref, out_ref,
                o_s, state_s, aq_s, bb_s, op_s, gam_s, bdm_s, lvl_s, xs_s, mt_s, at_s, gw_s, rv_s):
    tq = x_ref.shape[0]
    t = pl.program_id(1)

    @pl.when(t == 0)
    def _():
        state_s[...] = jnp.zeros_like(state_s)

    q_ref = proj_ref.at[:, 0:D_DN]
    k_ref = proj_ref.at[:, D_DN:2 * D_DN]
    v_ref = proj_ref.at[:, 2 * D_DN:3 * D_DN]

    n_chunks = tq // CHUNK
    quad = N_HEADS * CHUNK
    row_q = lax.broadcasted_iota(jnp.int32, (CHUNK, quad), 0)
    lane_q = lax.broadcasted_iota(jnp.int32, (CHUNK, quad), 1)
    col_q = lane_q & (CHUNK - 1)
    head_q = lane_q >> 6
    tri_incl = row_q >= col_q
    tri_strict = row_q > col_q
    eye_q = row_q == col_q
    eye_f = eye_q.astype(F32)
    row_l = lax.broadcasted_iota(jnp.int32, (CHUNK, LANES), 0)

    def level_mask(r, c, l):
        return (((r >> l) & 1) == 1) & (((c >> l) & 1) == 0) & ((r >> (l + 1)) == (c >> (l + 1)))

    @pl.when((pl.program_id(0) == 0) & (t == 0))
    def _():
        r2 = lax.broadcasted_iota(jnp.int32, (quad, quad), 0)
        c2 = lax.broadcasted_iota(jnp.int32, (quad, quad), 1)
        same_head = (r2 >> 6) == (c2 >> 6)
        bdm_s[...] = same_head.astype(F32).astype(BF16)
        for l in range(1, 6):
            lm = same_head & level_mask(r2 & (CHUNK - 1), c2 & (CHUNK - 1), l)
            lvl_s[l - 1] = lm.astype(F32).astype(BF16)

    lm0_f = level_mask(row_q, col_q, 0).astype(F32)
    zero_blk = jnp.zeros((CHUNK, HEAD_DIM), BF16)
    zero_sq = jnp.zeros((HEAD_DIM, HEAD_DIM), BF16)

    def head_bcast(cols, width):
        if width == HEAD_DIM:
            return jnp.concatenate([jnp.broadcast_to(c_, (CHUNK, HEAD_DIM)) for c_ in cols], axis=1)
        full = [jnp.broadcast_to(c_, (CHUNK, quad)) for c_ in cols]
        return jnp.where(head_q == 0, full[0],
                         jnp.where(head_q == 1, full[1], jnp.where(head_q == 2, full[2], full[3])))

    def bd_wide(a):
        rows = []
        for h in range(N_HEADS):
            rows.append(jnp.concatenate(
                [a[:, j * HEAD_DIM:(j + 1) * HEAD_DIM] if j == h else zero_blk for j in range(N_HEADS)],
                axis=1))
        return jnp.concatenate(rows, axis=0)

    def tile4(a):
        return jnp.concatenate([a] * N_HEADS, axis=0)

    carried = {"state": None}

    def group(gi):
        chunks = range(gi * GROUP_CHUNKS, (gi + 1) * GROUP_CHUNKS)
        rows = {c: slice(c * CHUNK, (c + 1) * CHUNK) for c in chunks}

        for n, c in enumerate(chunks):
            bt = gb_ref[rows[c], :]
            gcum = bt
            for s in (1, 2, 4, 8, 16, 32):
                gcum = gcum + jnp.where(row_l >= s, pltpu.roll(gcum, s, axis=0), 0.0)
            g_cols = [gcum[:, h:h + 1] for h in range(N_HEADS)]
            b_cols = [bt[:, N_HEADS + h:N_HEADS + h + 1] for h in range(N_HEADS)]
            g_i = head_bcast(g_cols, CHUNK)
            b_i = head_bcast(b_cols, CHUNK)
            gw_s[c] = head_bcast(g_cols, HEAD_DIM)
            g_j = jnp.sum(jnp.where(eye_q, g_i, 0.0), axis=0, keepdims=True)
            b_j = jnp.sum(jnp.where(eye_q, b_i, 0.0), axis=0, keepdims=True)
            rv_s[c, 0:1, :] = b_j
            rv_s[c, 1:2, :] = b_j * jnp.exp(g_j)
            decay = jnp.exp(jnp.where(tri_incl, g_i - g_j, -1e30))

            k16 = k_ref[rows[c], :]
            kq = jnp.concatenate([k16, q_ref[rows[c], :]], axis=0)
            kq_k = lax.dot_general(kq, bd_wide(k16), (((1,), (1,)), ((), ())),
                                   preferred_element_type=F32)
            m_st = jnp.where(tri_strict, kq_k[:CHUNK] * b_i * decay, 0.0)
            at_s[c] = (kq_k[CHUNK:] * decay).astype(BF16)
            xs_s[c] = eye_f - m_st * lm0_f
            mt_s[c] = m_st.astype(BF16)
            if n % 2 == 1:
                yield

        for l in range(1, 6):
            for c in chunks:
                x = xs_s[c]
                x16 = x.astype(BF16)
                y = jnp.dot(x16, tile4(mt_s[c]) * lvl_s[l - 1], preferred_element_type=F32)
                z = jnp.dot(y.astype(BF16), tile4(x16) * bdm_s[...], preferred_element_type=F32)
                xs_s[c] = x - z
            yield

        for n, c in enumerate(chunks):
            k16 = k_ref[rows[c], :]
            q = q_ref[rows[c], :].astype(F32)
            k = k16.astype(F32)
            x = xs_s[c]
            u = jnp.dot((x * rv_s[c, 0:1, :]).astype(BF16), bd_wide(v_ref[rows[c], :]),
                        preferred_element_type=F32)
            w = jnp.dot((x * rv_s[c, 1:2, :]).astype(BF16), bd_wide(k16), preferred_element_type=F32)
            u16 = u.astype(BF16)
            w16 = w.astype(BF16)
            a_wu = jnp.dot(at_s[c], jnp.concatenate([bd_wide(w16), bd_wide(u16)], axis=1),
                           preferred_element_type=F32)
            g_wide = gw_s[c]
            g_last = g_wide[CHUNK - 1:CHUNK, :]
            aq_s[c, 2 * CHUNK:3 * CHUNK, :] = (q * jnp.exp(g_wide) - a_wu[:, :D_DN]).astype(BF16)
            op_s[c] = a_wu[:, D_DN:]
            gam_s[c] = jnp.broadcast_to(jnp.exp(g_last), (8, D_DN))
            k_dec = (k * jnp.exp(g_last - g_wide)).astype(BF16)
            for h in range(N_HEADS):
                hs = slice(h * HEAD_DIM, (h + 1) * HEAD_DIM)
                ab_h = _mm_tn(k_dec[:, hs], jnp.concatenate([w16[:, hs], u16[:, hs]], axis=1))
                aq_s[c, 0:2 * CHUNK, hs] = ab_h[:, :HEAD_DIM].astype(BF16)
                bb_s[c, :, hs] = ab_h[:, HEAD_DIM:]
            if n % 2 == 1:
                yield

        state = state_s[...] if gi == 0 else carried["state"]
        for c in chunks:
            s16 = state.astype(BF16)
            new_parts = []
            o_parts_c = []
            for p in range(N_HEADS // 2):
                ps = slice(p * 2 * HEAD_DIM, (p + 1) * 2 * HEAD_DIM)
                s_a = s16[:, (2 * p) * HEAD_DIM:(2 * p + 1) * HEAD_DIM]
                s_b = s16[:, (2 * p + 1) * HEAD_DIM:(2 * p + 2) * HEAD_DIM]
                bd = jnp.concatenate([jnp.concatenate([s_a, zero_sq], axis=1),
                                      jnp.concatenate([zero_sq, s_b], axis=1)], axis=0)
                r = jnp.dot(aq_s[c, :, ps], bd, preferred_element_type=F32)
                new_parts.append(gam_s[c, 0:1, ps] * state[:, ps] + bb_s[c, :, ps] - r[:2 * CHUNK])
                o_parts_c.append(op_s[c, :, ps] + r[2 * CHUNK:])
            state = jnp.concatenate(new_parts, axis=1)
            o_s[rows[c], :] = jnp.concatenate(o_parts_c, axis=1)
        carried["state"] = state
        if gi == n_chunks // GROUP_CHUNKS - 1:
            state_s[...] = state
        yield

        rg = slice(gi * GROUP_CHUNKS * CHUNK, (gi + 1) * GROUP_CHUNKS * CHUNK)
        zg = proj_ref[rg, COL_ZG:COL_ZG + D_DN].astype(F32)
        o_parts = []
        for h in range(N_HEADS):
            hs = slice(h * HEAD_DIM, (h + 1) * HEAD_DIM)
            oh = o_s[rg, hs]
            o_parts.append(oh * _rms_scale(oh) * dnw_ref[...] * zg[:, hs])
        o_dn = jnp.concatenate(o_parts, axis=1)
        y = (jnp.dot(o_dn.astype(BF16), wout_ref[0:D_DN, :], preferred_element_type=F32)
             + jnp.dot(proj_ref[rg, COL_YSC:COL_YSC + D_SC], wout_ref[D_DN:D_MODEL, :],
                       preferred_element_type=F32))
        out_ref[rg, :] = x_ref[rg, :] + mod_ref[2:3, :] * (y * _rms_scale(y) * pw_ref[...])

    _interleave([group(gi) for gi in range(n_chunks // GROUP_CHUNKS)], lead=2)


def _mixer(proj, gb, x2, mod3, dnw, w_out, pw, bsz, seq):
    tq = TQ_MIX
    per_b = seq // tq
    t = x2.shape[0]

    def tile(b, s):
        return (b * per_b + s, 0)

    const = lambda b, s: (0, 0)
    return pl.pallas_call(
        _mix_kernel,
        grid=(bsz, per_b),
        in_specs=[
            pl.BlockSpec((tq, N_P2), tile),
            pl.BlockSpec((tq, LANES), tile),
            pl.BlockSpec((tq, D_MODEL), tile),
            pl.BlockSpec((None, 6, D_MODEL), lambda b, s: (b, 0, 0)),
            pl.BlockSpec((1, HEAD_DIM), const),
            pl.BlockSpec((D_MODEL, D_MODEL), const),
            pl.BlockSpec((1, D_MODEL), const),
        ],
        out_specs=pl.BlockSpec((tq, D_MODEL), tile),
        out_shape=jax.ShapeDtypeStruct((t, D_MODEL), F32),
        scratch_shapes=[
            pltpu.VMEM((tq, D_DN), F32),
            pltpu.VMEM((HEAD_DIM, D_DN), F32),
            pltpu.VMEM((tq // CHUNK, 3 * CHUNK, D_DN), BF16),
            pltpu.VMEM((tq // CHUNK, HEAD_DIM, D_DN), F32),
            pltpu.VMEM((tq // CHUNK, CHUNK, D_DN), F32),
            pltpu.VMEM((tq // CHUNK, 8, D_DN), F32),
            pltpu.VMEM((N_HEADS * CHUNK, N_HEADS * CHUNK), BF16),
            pltpu.VMEM((5, N_HEADS * CHUNK, N_HEADS * CHUNK), BF16),
            pltpu.VMEM((tq // CHUNK, CHUNK, N_HEADS * CHUNK), F32),
            pltpu.VMEM((tq // CHUNK, CHUNK, N_HEADS * CHUNK), BF16),
            pltpu.VMEM((tq // CHUNK, CHUNK, N_HEADS * CHUNK), BF16),
            pltpu.VMEM((tq // CHUNK, CHUNK, D_DN), F32),
            pltpu.VMEM((tq // CHUNK, 8, N_HEADS * CHUNK), F32),
        ],
        compiler_params=pltpu.CompilerParams(dimension_semantics=("arbitrary", "arbitrary"),
                                             vmem_limit_bytes=VMEM_LIMIT),
        name="token_mixer",
    )(proj, gb, x2, mod3, dnw, w_out, pw)


def _ffn_kernel(x_ref, mod_ref, nw_ref, w1_hbm, w2_hbm, pw_ref, out_ref, w1_ref, w2_ref, stage_s, sem):
    @pl.when(pl.program_id(0) == 0)
    def _():
        pieces = []
        for j in range(D_FF // FF_CHUNK):
            cs = slice(j * FF_CHUNK, (j + 1) * FF_CHUNK)
            pieces.append((w1_hbm.at[:, cs], w1_ref.at[:, cs]))
        for j in range(D_FF // FF_CHUNK):
            cs = slice(j * FF_CHUNK, (j + 1) * FF_CHUNK)
            pieces.append((w2_hbm.at[cs, :], w2_ref.at[cs, :]))

        def copy(n):
            return pltpu.make_async_copy(pieces[n][0], stage_s.at[n % 2], sem.at[n % 2])

        copy(0).start()
        for n in range(len(pieces)):
            if n + 1 < len(pieces):
                copy(n + 1).start()
            copy(n).wait()
            pieces[n][1][...] = stage_s[n % 2].astype(BF16)

    x = x_ref[...]
    h = x * _rms_scale(x) * nw_ref[...]
    h = h * (1.0 + mod_ref[4:5, :]) + mod_ref[3:4, :]
    hb = h.astype(BF16)
    y = jnp.zeros(x.shape, F32)
    for j in range(D_FF // FF_CHUNK):
        a = jnp.dot(hb, w1_ref[:, j * FF_CHUNK:(j + 1) * FF_CHUNK], preferred_element_type=F32)
        a = jnp.square(jnp.maximum(a, 0.0)).astype(BF16)
        y = y + jnp.dot(a, w2_ref[j * FF_CHUNK:(j + 1) * FF_CHUNK, :], preferred_element_type=F32)
    out_ref[...] = x + mod_ref[5:6, :] * (y * _rms_scale(y) * pw_ref[...])


def _ffn(x2, mod3, nw, w1, w2, pw, seq):
    t = x2.shape[0]
    tm = TM_FFN
    per_b = seq // tm
    return pl.pallas_call(
        _ffn_kernel,
        grid=(t // tm,),
        in_specs=[
            pl.BlockSpec((tm, D_MODEL), lambda i: (i, 0)),
            pl.BlockSpec((None, 6, D_MODEL), lambda i: (i // per_b, 0, 0)),
            pl.BlockSpec((1, D_MODEL), lambda i: (0, 0)),
            pl.BlockSpec(memory_space=pl.ANY),
            pl.BlockSpec(memory_space=pl.ANY),
            pl.BlockSpec((1, D_MODEL), lambda i: (0, 0)),
        ],
        out_specs=pl.BlockSpec((tm, D_MODEL), lambda i: (i, 0)),
        out_shape=jax.ShapeDtypeStruct((t, D_MODEL), F32),
        scratch_shapes=[
            pltpu.VMEM((D_MODEL, D_FF), BF16),
            pltpu.VMEM((D_FF, D_MODEL), BF16),
            pltpu.VMEM((2, FF_CHUNK, FF_CHUNK), F32),
            pltpu.SemaphoreType.DMA((2,)),
        ],
        compiler_params=pltpu.CompilerParams(dimension_semantics=("arbitrary",),
                                             vmem_limit_bytes=VMEM_LIMIT),
        name="ffn",
    )(x2, mod3, nw, w1, w2, pw)


def _layer(x2, c, w_ada, b_ada, pre_mix_w, post_mix_w, w_in, dn_conv_w, dn_a_log, dn_dt_bias,
           dn_norm_w, sc_conv_w, w_out, pre_ffn_w, post_ffn_w, w_ff1, w_ff2, bsz, seq):
    mod3 = _ada(c, w_ada, b_ada[None, :]).reshape(bsz, 6, D_MODEL)

    gp = jnp.zeros((8, LANES), F32)
    gp = gp.at[0, :N_HEADS].set(dn_a_log).at[1, :N_HEADS].set(dn_dt_bias)
    proj, gb = _in_proj(x2, mod3, pre_mix_w[None, :], w_in.T, dn_conv_w, sc_conv_w, gp, seq)

    x2 = _mixer(proj, gb, x2, mod3, dn_norm_w[None, :], w_out.astype(BF16), post_mix_w[None, :],
                bsz, seq)

    return _ffn(x2, mod3, pre_ffn_w[None, :], w_ff1, w_ff2, post_ffn_w[None, :], seq)


def kernel(x, c, w_ada, b_ada, pre_mix_norm_w, post_mix_norm_w, w_in, dn_conv_w, dn_a_log, dn_dt_bias, dn_norm_w, sc_conv_w, w_out, pre_ffn_norm_w, post_ffn_norm_w, w_ff1, w_ff2):
    bsz, seq, d = x.shape
    x2 = x.reshape(bsz * seq, d)
    for l in range(w_ada.shape[0]):
        x2 = _layer(x2, c, w_ada[l], b_ada[l], pre_mix_norm_w[l], post_mix_norm_w[l], w_in[l],
                    dn_conv_w[l], dn_a_log[l], dn_dt_bias[l], dn_norm_w[l], sc_conv_w[l], w_out[l],
                    pre_ffn_norm_w[l], post_ffn_norm_w[l], w_ff1[l], w_ff2[l], bsz, seq)
    return x2.reshape(bsz, seq, d)
```

```python
import functools

import jax
import jax.numpy as jnp
from jax import lax
from jax.experimental import pallas as pl
from jax.experimental.pallas import tpu as pltpu

F32 = jnp.float32
BF16 = jnp.bfloat16

D_MODEL = 1024
N_HEADS = 4
HEAD_DIM = 128
D_DN = N_HEADS * HEAD_DIM
D_SC = D_MODEL - D_DN
DN_CONV = 4
SC_CONV = 3
CHUNK = 64
D_FF = 4 * D_MODEL
EPS = 1e-6
N_QKV = 3 * D_DN
N_MAIN = 4 * D_DN + 3 * D_SC
COL_Z = N_QKV
COL_SB = 4 * D_DN
COL_SC = COL_SB + D_SC
COL_SH = COL_SC + D_SC
N_P2 = N_QKV + D_DN + D_SC
COL_ZG = N_QKV
COL_YSC = N_QKV + D_DN
LANES = 128
HALO = 8

TM_IN = 512
TQ_MIX = 1024
GROUP_CHUNKS = 4
TM_FFN = 512
FF_CHUNK = 1024
VMEM_LIMIT = 56 * 1024 * 1024


def _sigmoid(x):
    return 0.5 + 0.5 * jnp.tanh(0.5 * x)


def _mm_tn(a, b):
    return lax.dot_general(a, b, (((0,), (0,)), ((), ())), preferred_element_type=F32)


def _rms_scale(x):
    return lax.rsqrt(jnp.mean(x * x, axis=-1, keepdims=True) + EPS)


def _interleave(gens, lead):
    live = []
    pending = list(gens)
    rnd = 0
    while live or pending:
        if pending and rnd % lead == 0:
            live.append(pending.pop(0))
        for g in list(live):
            try:
                next(g)
            except StopIteration:
                live.remove(g)
        rnd += 1


def _ada_kernel(c_ref, w_ref, b_ref, o_ref):
    c = c_ref[...]
    ca = (c * _sigmoid(c)).astype(BF16)
    o_ref[...] = jnp.dot(ca, w_ref[...].astype(BF16), preferred_element_type=F32) + b_ref[...]


def _ada(c, w_ada, b_ada):
    bsz = c.shape[0]
    n = w_ada.shape[1]
    bn = 1024
    return pl.pallas_call(
        _ada_kernel,
        grid=(n // bn,),
        in_specs=[
            pl.BlockSpec((bsz, D_MODEL), lambda j: (0, 0)),
            pl.BlockSpec((D_MODEL, bn), lambda j: (0, j)),
            pl.BlockSpec((1, bn), lambda j: (0, j)),
        ],
        out_specs=pl.BlockSpec((bsz, bn), lambda j: (0, j)),
        out_shape=jax.ShapeDtypeStruct((bsz, n), F32),
        compiler_params=pltpu.CompilerParams(dimension_semantics=("arbitrary",),
                                             vmem_limit_bytes=VMEM_LIMIT),
        name="ada_mod",
    )(c, w_ada, b_ada)


def _causal_conv(ext, w, c0):
    width = w.shape[0]
    r = ext.shape[0] - HALO
    n = ext.shape[1]
    cur = ext[HALO:HALO + r, :]
    tiles = (r // HALO, HALO, n)
    cur3 = cur.reshape(tiles)
    prev3 = ext[0:r, :].reshape(tiles)
    sub = lax.broadcasted_iota(jnp.int32, (1, HALO, n), 1)
    acc = w[width - 1:width, c0:c0 + n] * cur
    for s in range(1, width):
        mixed = jnp.where(sub >= HALO - s, prev3, cur3)
        shifted = pltpu.roll(mixed, s, axis=1).reshape(r, n)
        acc = acc + w[width - 1 - s:width - s, c0:c0 + n] * shifted
    return acc


def _silu_from_half(hx):
    return hx + hx * jnp.tanh(hx)


def _in_half(hf, x_ref, mod_ref, nw_ref, wm_ref, wab_ref, cw_ref, scw_ref, gp_ref,
             proj_ref, gb_ref, e_s, p_s):
    half = x_ref.shape[0] // 2
    r0 = hf * half
    rows = slice(r0, r0 + half)
    erows = slice(HALO + r0, HALO + r0 + half)
    wrows = slice(r0, r0 + half + HALO)

    x = x_ref[rows, :]
    h = x * _rms_scale(x) * (nw_ref[...] * (1.0 + mod_ref[1:2, :])) + mod_ref[0:1, :]
    hb = h.astype(BF16)
    half_cw = 0.5 * cw_ref[...]
    yield

    def proj(c0, n=D_DN):
        return jnp.dot(hb, wm_ref[:, c0:c0 + n], preferred_element_type=F32)

    for part in range(3):
        e_s[part, erows, :] = proj(part * D_DN)
        yield
    e_s[3, erows, :] = proj(COL_SC)
    p_s[0, rows, :] = proj(COL_Z)
    yield
    p_s[1, rows, :] = proj(COL_SH)
    p_s[2, rows, :] = proj(COL_SB)
    ab = jnp.dot(hb, wab_ref[...], preferred_element_type=F32)
    yield

    for part in range(3):
        c0 = part * D_DN
        y = _silu_from_half(_causal_conv(e_s[part, wrows, :], half_cw, c0))
        if part < 2:
            parts = []
            for hd in range(N_HEADS):
                yh = y[:, hd * HEAD_DIM:(hd + 1) * HEAD_DIM]
                nrm = lax.rsqrt(jnp.sum(yh * yh, axis=-1, keepdims=True) + EPS)
                if part == 0:
                    nrm = nrm * (HEAD_DIM ** -0.5)
                parts.append(yh * nrm)
            y = jnp.concatenate(parts, axis=1)
        proj_ref[rows, c0:c0 + D_DN] = y.astype(BF16)
        yield

    proj_ref[rows, COL_ZG:COL_ZG + D_DN] = _silu_from_half(0.5 * p_s[0, rows, :]).astype(BF16)
    e_s[3, erows, :] = e_s[3, erows, :] * p_s[1, rows, :]
    conv = _causal_conv(e_s[3, wrows, :], scw_ref[...], 0)
    proj_ref[rows, COL_YSC:COL_YSC + D_SC] = (p_s[2, rows, :] * conv).astype(BF16)

    xg = ab + gp_ref[1:2, :]
    softplus = jnp.maximum(xg, 0.0) + jnp.log1p(jnp.exp(-jnp.abs(xg)))
    lane = lax.broadcasted_iota(jnp.int32, ab.shape, 1)
    gb_ref[rows, :] = jnp.where(lane < N_HEADS, -jnp.exp(gp_ref[0:1, :]) * softplus, _sigmoid(ab))


def _in_kernel(x_ref, mod_ref, nw_ref, win_ref, cw_ref, scw_ref, gp_ref,
               proj_ref, gb_ref, e_s, p_s, wm_ref, wab_ref, *, per_b):
    tm = x_ref.shape[0]

    @pl.when(pl.program_id(0) == 0)
    def _():
        n_gate = 2 * N_HEADS
        cb = D_DN // 2
        for r in range(N_MAIN // cb):
            src = r * cb + (n_gate if r * cb >= COL_SB else 0)
            wm_ref[:, r * cb:(r + 1) * cb] = win_ref[src:src + cb, :].T.astype(BF16)
        wab_ref[...] = win_ref[COL_SB:COL_SB + LANES, :].T.astype(BF16)

    @pl.when(pl.program_id(0) % per_b == 0)
    def _():
        e_s[:, 0:HALO, :] = jnp.zeros((e_s.shape[0], HALO, D_DN), F32)

    _interleave([_in_half(hf, x_ref, mod_ref, nw_ref, wm_ref, wab_ref, cw_ref, scw_ref, gp_ref,
                          proj_ref, gb_ref, e_s, p_s) for hf in range(2)], lead=5)
    e_s[:, 0:HALO, :] = e_s[:, tm:tm + HALO, :]


def _in_proj(x2, mod3, nw, w_in, cw, scw, gp, seq):
    t = x2.shape[0]
    tm = TM_IN
    per_b = seq // tm
    const = lambda i: (0, 0)
    return pl.pallas_call(
        functools.partial(_in_kernel, per_b=per_b),
        grid=(t // tm,),
        in_specs=[
            pl.BlockSpec((tm, D_MODEL), lambda i: (i, 0)),
            pl.BlockSpec((None, 6, D_MODEL), lambda i: (i // per_b, 0, 0)),
            pl.BlockSpec((1, D_MODEL), const),
            pl.BlockSpec(w_in.shape, const, pipeline_mode=pl.Buffered(1)),
            pl.BlockSpec((DN_CONV, N_QKV), const),
            pl.BlockSpec((SC_CONV, D_SC), const),
            pl.BlockSpec((8, LANES), const),
        ],
        out_specs=[
            pl.BlockSpec((tm, N_P2), lambda i: (i, 0)),
            pl.BlockSpec((tm, LANES), lambda i: (i, 0)),
        ],
        out_shape=[
            jax.ShapeDtypeStruct((t, N_P2), BF16),
            jax.ShapeDtypeStruct((t, LANES), F32),
        ],
        scratch_shapes=[
            pltpu.VMEM((4, HALO + tm, D_DN), F32),
            pltpu.VMEM((3, tm, D_DN), F32),
            pltpu.VMEM((D_MODEL, N_MAIN), BF16),
            pltpu.VMEM((D_MODEL, LANES), BF16),
        ],
        compiler_params=pltpu.CompilerParams(dimension_semantics=("arbitrary",),
                                             vmem_limit_bytes=VMEM_LIMIT),
        name="in_proj",
    )(x2, mod3, nw, w_in, cw, scw, gp)


def _mix_kernel(proj_ref, gb_ref, x_ref, mod_ref, dnw_ref, wout_ref, pw_ref, out_ref,
                o_s, state_s, aq_s, bb_s, op_s, gam_s, bdm_s, lvl_s, xs_s, mt_s, at_s, gw_s, rv_s):
    tq = x_ref.shape[0]
    t = pl.program_id(1)

    @pl.when(t == 0)
    def _():
        state_s[...] = jnp.zeros_like(state_s)

    q_ref = proj_ref.at[:, 0:D_DN]
    k_ref = proj_ref.at[:, D_DN:2 * D_DN]
    v_ref = proj_ref.at[:, 2 * D_DN:3 * D_DN]

    n_chunks = tq // CHUNK
    quad = N_HEADS * CHUNK
    row_q = lax.broadcasted_iota(jnp.int32, (CHUNK, quad), 0)
    lane_q = lax.broadcasted_iota(jnp.int32, (CHUNK, quad), 1)
    col_q = lane_q & (CHUNK - 1)
    head_q = lane_q >> 6
    tri_incl = row_q >= col_q
    tri_strict = row_q > col_q
    eye_q = row_q == col_q
    eye_f = eye_q.astype(F32)
    row_l = lax.broadcasted_iota(jnp.int32, (CHUNK, LANES), 0)

    def level_mask(r, c, l):
        return (((r >> l) & 1) == 1) & (((c >> l) & 1) == 0) & ((r >> (l + 1)) == (c >> (l + 1)))

    @pl.when((pl.program_id(0) == 0) & (t == 0))
    def _():
        r2 = lax.broadcasted_iota(jnp.int32, (quad, quad), 0)
        c2 = lax.broadcasted_iota(jnp.int32, (quad, quad), 1)
        same_head = (r2 >> 6) == (c2 >> 6)
        bdm_s[...] = same_head.astype(F32).astype(BF16)
        for l in range(1, 6):
            lm = same_head & level_mask(r2 & (CHUNK - 1), c2 & (CHUNK - 1), l)
            lvl_s[l - 1] = lm.astype(F32).astype(BF16)

    lm0_f = level_mask(row_q, col_q, 0).astype(F32)
    zero_blk = jnp.zeros((CHUNK, HEAD_DIM), BF16)
    zero_sq = jnp.zeros((HEAD_DIM, HEAD_DIM), BF16)

    def head_bcast(cols, width):
        if width == HEAD_DIM:
            return jnp.concatenate([jnp.broadcast_to(c_, (CHUNK, HEAD_DIM)) for c_ in cols], axis=1)
        full = [jnp.broadcast_to(c_, (CHUNK, quad)) for c_ in cols]
        return jnp.where(head_q == 0, full[0],
                         jnp.where(head_q == 1, full[1], jnp.where(head_q == 2, full[2], full[3])))

    def bd_wide(a):
        rows = []
        for h in range(N_HEADS):
            rows.append(jnp.concatenate(
                [a[:, j * HEAD_DIM:(j + 1) * HEAD_DIM] if j == h else zero_blk for j in range(N_HEADS)],
                axis=1))
        return jnp.concatenate(rows, axis=0)

    def tile4(a):
        return jnp.concatenate([a] * N_HEADS, axis=0)

    carried = {"state": None}

    def group(gi):
        chunks = range(gi * GROUP_CHUNKS, (gi + 1) * GROUP_CHUNKS)
        rows = {c: slice(c * CHUNK, (c + 1) * CHUNK) for c in chunks}

        for n, c in enumerate(chunks):
            bt = gb_ref[rows[c], :]
            gcum = bt
            for s in (1, 2, 4, 8, 16, 32):
                gcum = gcum + jnp.where(row_l >= s, pltpu.roll(gcum, s, axis=0), 0.0)
            g_cols = [gcum[:, h:h + 1] for h in range(N_HEADS)]
            b_cols = [bt[:, N_HEADS + h:N_HEADS + h + 1] for h in range(N_HEADS)]
            g_i = head_bcast(g_cols, CHUNK)
            b_i = head_bcast(b_cols, CHUNK)
            gw_s[c] = head_bcast(g_cols, HEAD_DIM)
            g_j = jnp.sum(jnp.where(eye_q, g_i, 0.0), axis=0, keepdims=True)
            b_j = jnp.sum(jnp.where(eye_q, b_i, 0.0), axis=0, keepdims=True)
            rv_s[c, 0:1, :] = b_j
            rv_s[c, 1:2, :] = b_j * jnp.exp(g_j)
            decay = jnp.exp(jnp.where(tri_incl, g_i - g_j, -1e30))

            k16 = k_ref[rows[c], :]
            kq = jnp.concatenate([k16, q_ref[rows[c], :]], axis=0)
            kq_k = lax.dot_general(kq, bd_wide(k16), (((1,), (1,)), ((), ())),
                                   preferred_element_type=F32)
            m_st = jnp.where(tri_strict, kq_k[:CHUNK] * b_i * decay, 0.0)
            at_s[c] = (kq_k[CHUNK:] * decay).astype(BF16)
            xs_s[c] = eye_f - m_st * lm0_f
            mt_s[c] = m_st.astype(BF16)
            if n % 2 == 1:
                yield

        for l in range(1, 6):
            for c in chunks:
                x = xs_s[c]
                x16 = x.astype(BF16)
                y = jnp.dot(x16, tile4(mt_s[c]) * lvl_s[l - 1], preferred_element_type=F32)
                z = jnp.dot(y.astype(BF16), tile4(x16) * bdm_s[...], preferred_element_type=F32)
                xs_s[c] = x - z
            yield

        for n, c in enumerate(chunks):
            k16 = k_ref[rows[c], :]
            q = q_ref[rows[c], :].astype(F32)
            k = k16.astype(F32)
            x = xs_s[c]
            u = jnp.dot((x * rv_s[c, 0:1, :]).astype(BF16), bd_wide(v_ref[rows[c], :]),
                        preferred_element_type=F32)
            w = jnp.dot((x * rv_s[c, 1:2, :]).astype(BF16), bd_wide(k16), preferred_element_type=F32)
            u16 = u.astype(BF16)
            w16 = w.astype(BF16)
            a_wu = jnp.dot(at_s[c], jnp.concatenate([bd_wide(w16), bd_wide(u16)], axis=1),
                           preferred_element_type=F32)
            g_wide = gw_s[c]
            g_last = g_wide[CHUNK - 1:CHUNK, :]
            aq_s[c, 2 * CHUNK:3 * CHUNK, :] = (q * jnp.exp(g_wide) - a_wu[:, :D_DN]).astype(BF16)
            op_s[c] = a_wu[:, D_DN:]
            gam_s[c] = jnp.broadcast_to(jnp.exp(g_last), (8, D_DN))
            k_dec = (k * jnp.exp(g_last - g_wide)).astype(BF16)
            for h in range(N_HEADS):
                hs = slice(h * HEAD_DIM, (h + 1) * HEAD_DIM)
                ab_h = _mm_tn(k_dec[:, hs], jnp.concatenate([w16[:, hs], u16[:, hs]], axis=1))
                aq_s[c, 0:2 * CHUNK, hs] = ab_h[:, :HEAD_DIM].astype(BF16)
                bb_s[c, :, hs] = ab_h[:, HEAD_DIM:]
            if n % 2 == 1:
                yield

        state = state_s[...] if gi == 0 else carried["state"]
        for c in chunks:
            s16 = state.astype(BF16)
            new_parts = []
            o_parts_c = []
            for p in range(N_HEADS // 2):
                ps = slice(p * 2 * HEAD_DIM, (p + 1) * 2 * HEAD_DIM)
                s_a = s16[:, (2 * p) * HEAD_DIM:(2 * p + 1) * HEAD_DIM]
                s_b = s16[:, (2 * p + 1) * HEAD_DIM:(2 * p + 2) * HEAD_DIM]
                bd = jnp.concatenate([jnp.concatenate([s_a, zero_sq], axis=1),
                                      jnp.concatenate([zero_sq, s_b], axis=1)], axis=0)
                r = jnp.dot(aq_s[c, :, ps], bd, preferred_element_type=F32)
                new_parts.append(gam_s[c, 0:1, ps] * state[:, ps] + bb_s[c, :, ps] - r[:2 * CHUNK])
                o_parts_c.append(op_s[c, :, ps] + r[2 * CHUNK:])
            state = jnp.concatenate(new_parts, axis=1)
            o_s[rows[c], :] = jnp.concatenate(o_parts_c, axis=1)
        carried["state"] = state
        if gi == n_chunks // GROUP_CHUNKS - 1:
            state_s[...] = state
        yield

        rg = slice(gi * GROUP_CHUNKS * CHUNK, (gi + 1) * GROUP_CHUNKS * CHUNK)
        zg = proj_ref[rg, COL_ZG:COL_ZG + D_DN].astype(F32)
        o_parts = []
        for h in range(N_HEADS):
            hs = slice(h * HEAD_DIM, (h + 1) * HEAD_DIM)
            oh = o_s[rg, hs]
            o_parts.append(oh * _rms_scale(oh) * dnw_ref[...] * zg[:, hs])
        o_dn = jnp.concatenate(o_parts, axis=1)
        y = (jnp.dot(o_dn.astype(BF16), wout_ref[0:D_DN, :], preferred_element_type=F32)
             + jnp.dot(proj_ref[rg, COL_YSC:COL_YSC + D_SC], wout_ref[D_DN:D_MODEL, :],
                       preferred_element_type=F32))
        out_ref[rg, :] = x_ref[rg, :] + mod_ref[2:3, :] * (y * _rms_scale(y) * pw_ref[...])

    _interleave([group(gi) for gi in range(n_chunks // GROUP_CHUNKS)], lead=2)


def _mixer(proj, gb, x2, mod3, dnw, w_out, pw, bsz, seq):
    tq = TQ_MIX
    per_b = seq // tq
    t = x2.shape[0]

    def tile(b, s):
        return (b * per_b + s, 0)

    const = lambda b, s: (0, 0)
    return pl.pallas_call(
        _mix_kernel,
        grid=(bsz, per_b),
        in_specs=[
            pl.BlockSpec((tq, N_P2), tile),
            pl.BlockSpec((tq, LANES), tile),
            pl.BlockSpec((tq, D_MODEL), tile),
            pl.BlockSpec((None, 6, D_MODEL), lambda b, s: (b, 0, 0)),
            pl.BlockSpec((1, HEAD_DIM), const),
            pl.BlockSpec((D_MODEL, D_MODEL), const),
            pl.BlockSpec((1, D_MODEL), const),
        ],
        out_specs=pl.BlockSpec((tq, D_MODEL), tile),
        out_shape=jax.ShapeDtypeStruct((t, D_MODEL), F32),
        scratch_shapes=[
            pltpu.VMEM((tq, D_DN), F32),
            pltpu.VMEM((HEAD_DIM, D_DN), F32),
            pltpu.VMEM((tq // CHUNK, 3 * CHUNK, D_DN), BF16),
            pltpu.VMEM((tq // CHUNK, HEAD_DIM, D_DN), F32),
            pltpu.VMEM((tq // CHUNK, CHUNK, D_DN), F32),
            pltpu.VMEM((tq // CHUNK, 8, D_DN), F32),
            pltpu.VMEM((N_HEADS * CHUNK, N_HEADS * CHUNK), BF16),
            pltpu.VMEM((5, N_HEADS * CHUNK, N_HEADS * CHUNK), BF16),
            pltpu.VMEM((tq // CHUNK, CHUNK, N_HEADS * CHUNK), F32),
            pltpu.VMEM((tq // CHUNK, CHUNK, N_HEADS * CHUNK), BF16),
            pltpu.VMEM((tq // CHUNK, CHUNK, N_HEADS * CHUNK), BF16),
            pltpu.VMEM((tq // CHUNK, CHUNK, D_DN), F32),
            pltpu.VMEM((tq // CHUNK, 8, N_HEADS * CHUNK), F32),
        ],
        compiler_params=pltpu.CompilerParams(dimension_semantics=("arbitrary", "arbitrary"),
                                             vmem_limit_bytes=VMEM_LIMIT),
        name="token_mixer",
    )(proj, gb, x2, mod3, dnw, w_out, pw)


def _ffn_kernel(x_ref, mod_ref, nw_ref, w1_hbm, w2_hbm, pw_ref, out_ref, w1_ref, w2_ref, stage_s, sem):
    @pl.when(pl.program_id(0) == 0)
    def _():
        pieces = []
        for j in range(D_FF // FF_CHUNK):
            cs = slice(j * FF_CHUNK, (j + 1) * FF_CHUNK)
            pieces.append((w1_hbm.at[:, cs], w1_ref.at[:, cs]))
        for j in range(D_FF // FF_CHUNK):
            cs = slice(j * FF_CHUNK, (j + 1) * FF_CHUNK)
            pieces.append((w2_hbm.at[cs, :], w2_ref.at[cs, :]))

        def copy(n):
            return pltpu.make_async_copy(pieces[n][0], stage_s.at[n % 2], sem.at[n % 2])

        copy(0).start()
        for n in range(len(pieces)):
            if n + 1 < len(pieces):
                copy(n + 1).start()
            copy(n).wait()
            pieces[n][1][...] = stage_s[n % 2].astype(BF16)

    x = x_ref[...]
    h = x * _rms_scale(x) * nw_ref[...]
    h = h * (1.0 + mod_ref[4:5, :]) + mod_ref[3:4, :]
    hb = h.astype(BF16)
    y = jnp.zeros(x.shape, F32)
    for j in range(D_FF // FF_CHUNK):
        a = jnp.dot(hb, w1_ref[:, j * FF_CHUNK:(j + 1) * FF_CHUNK], preferred_element_type=F32)
        a = jnp.square(jnp.maximum(a, 0.0)).astype(BF16)
        y = y + jnp.dot(a, w2_ref[j * FF_CHUNK:(j + 1) * FF_CHUNK, :], preferred_element_type=F32)
    out_ref[...] = x + mod_ref[5:6, :] * (y * _rms_scale(y) * pw_ref[...])


def _ffn(x2, mod3, nw, w1, w2, pw, seq):
    t = x2.shape[0]
    tm = TM_FFN
    per_b = seq // tm
    return pl.pallas_call(
        _ffn_kernel,
        grid=(t // tm,),
        in_specs=[
            pl.BlockSpec((tm, D_MODEL), lambda i: (i, 0)),
            pl.BlockSpec((None, 6, D_MODEL), lambda i: (i // per_b, 0, 0)),
            pl.BlockSpec((1, D_MODEL), lambda i: (0, 0)),
            pl.BlockSpec(memory_space=pl.ANY),
            pl.BlockSpec(memory_space=pl.ANY),
            pl.BlockSpec((1, D_MODEL), lambda i: (0, 0)),
        ],
        out_specs=pl.BlockSpec((tm, D_MODEL), lambda i: (i, 0)),
        out_shape=jax.ShapeDtypeStruct((t, D_MODEL), F32),
        scratch_shapes=[
            pltpu.VMEM((D_MODEL, D_FF), BF16),
            pltpu.VMEM((D_FF, D_MODEL), BF16),
            pltpu.VMEM((2, FF_CHUNK, FF_CHUNK), F32),
            pltpu.SemaphoreType.DMA((2,)),
        ],
        compiler_params=pltpu.CompilerParams(dimension_semantics=("arbitrary",),
                                             vmem_limit_bytes=VMEM_LIMIT),
        name="ffn",
    )(x2, mod3, nw, w1, w2, pw)


def _layer(x2, c, w_ada, b_ada, pre_mix_w, post_mix_w, w_in, dn_conv_w, dn_a_log, dn_dt_bias,
           dn_norm_w, sc_conv_w, w_out, pre_ffn_w, post_ffn_w, w_ff1, w_ff2, bsz, seq):
    mod3 = _ada(c, w_ada, b_ada[None, :]).reshape(bsz, 6, D_MODEL)

    gp = jnp.zeros((8, LANES), F32)
    gp = gp.at[0, :N_HEADS].set(dn_a_log).at[1, :N_HEADS].set(dn_dt_bias)
    proj, gb = _in_proj(x2, mod3, pre_mix_w[None, :], w_in.T, dn_conv_w, sc_conv_w, gp, seq)

    x2 = _mixer(proj, gb, x2, mod3, dn_norm_w[None, :], w_out.astype(BF16), post_mix_w[None, :],
                bsz, seq)

    return _ffn(x2, mod3, pre_ffn_w[None, :], w_ff1, w_ff2, post_ffn_w[None, :], seq)


def kernel(x, c, w_ada, b_ada, pre_mix_norm_w, post_mix_norm_w, w_in, dn_conv_w, dn_a_log, dn_dt_bias, dn_norm_w, sc_conv_w, w_out, pre_ffn_norm_w, post_ffn_norm_w, w_ff1, w_ff2):
    bsz, seq, d = x.shape
    x2 = x.reshape(bsz * seq, d)
    for l in range(w_ada.shape[0]):
        x2 = _layer(x2, c, w_ada[l], b_ada[l], pre_mix_norm_w[l], post_mix_norm_w[l], w_in[l],
                    dn_conv_w[l], dn_a_log[l], dn_dt_bias[l], dn_norm_w[l], sc_conv_w[l], w_out[l],
                    pre_ffn_norm_w[l], post_ffn_norm_w[l], w_ff1[l], w_ff2[l], bsz, seq)
    return x2.reshape(bsz, seq, d)
```

```python
import functools

import jax
import jax.numpy as jnp
from jax import lax
from jax.experimental import pallas as pl
from jax.experimental.pallas import tpu as pltpu

F32 = jnp.float32
BF16 = jnp.bfloat16

D_MODEL = 1024
N_HEADS = 4
HEAD_DIM = 128
D_DN = N_HEADS * HEAD_DIM
D_SC = D_MODEL - D_DN
DN_CONV = 4
SC_CONV = 3
CHUNK = 64
D_FF = 4 * D_MODEL
EPS = 1e-6
N_QKV = 3 * D_DN
N_MAIN = 4 * D_DN + 3 * D_SC
COL_Z = N_QKV
COL_SB = 4 * D_DN
COL_SC = COL_SB + D_SC
COL_SH = COL_SC + D_SC
N_P2 = N_QKV + D_DN + D_SC
COL_ZG = N_QKV
COL_YSC = N_QKV + D_DN
LANES = 128
HALO = 8

TM_IN = 512
TQ_MIX = 512
MIX_SEQS = 2
GROUP_CHUNKS = 4
TM_FFN = 512
FF_CHUNK = 1024
VMEM_LIMIT = 56 * 1024 * 1024


def _sigmoid(x):
    return 0.5 + 0.5 * jnp.tanh(0.5 * x)


def _mm_tn(a, b):
    return lax.dot_general(a, b, (((0,), (0,)), ((), ())), preferred_element_type=F32)


def _rms_scale(x):
    return lax.rsqrt(jnp.mean(x * x, axis=-1, keepdims=True) + EPS)


def _interleave(gens, lead):
    live = []
    pending = list(gens)
    rnd = 0
    while live or pending:
        if pending and rnd % lead == 0:
            live.append(pending.pop(0))
        for g in list(live):
            try:
                next(g)
            except StopIteration:
                live.remove(g)
        rnd += 1


def _ada_kernel(c_ref, w_ref, b_ref, o_ref):
    c = c_ref[...]
    ca = (c * _sigmoid(c)).astype(BF16)
    o_ref[...] = jnp.dot(ca, w_ref[...].astype(BF16), preferred_element_type=F32) + b_ref[...]


def _ada(c, w_ada, b_ada):
    bsz = c.shape[0]
    n = w_ada.shape[1]
    bn = 1024
    return pl.pallas_call(
        _ada_kernel,
        grid=(n // bn,),
        in_specs=[
            pl.BlockSpec((bsz, D_MODEL), lambda j: (0, 0)),
            pl.BlockSpec((D_MODEL, bn), lambda j: (0, j)),
            pl.BlockSpec((1, bn), lambda j: (0, j)),
        ],
        out_specs=pl.BlockSpec((bsz, bn), lambda j: (0, j)),
        out_shape=jax.ShapeDtypeStruct((bsz, n), F32),
        compiler_params=pltpu.CompilerParams(dimension_semantics=("arbitrary",),
                                             vmem_limit_bytes=VMEM_LIMIT),
        name="ada_mod",
    )(c, w_ada, b_ada)


def _causal_conv(ext, w, c0):
    width = w.shape[0]
    r = ext.shape[0] - HALO
    n = ext.shape[1]
    cur = ext[HALO:HALO + r, :]
    tiles = (r // HALO, HALO, n)
    cur3 = cur.reshape(tiles)
    prev3 = ext[0:r, :].reshape(tiles)
    sub = lax.broadcasted_iota(jnp.int32, (1, HALO, n), 1)
    acc = w[width - 1:width, c0:c0 + n] * cur
    for s in range(1, width):
        mixed = jnp.where(sub >= HALO - s, prev3, cur3)
        shifted = pltpu.roll(mixed, s, axis=1).reshape(r, n)
        acc = acc + w[width - 1 - s:width - s, c0:c0 + n] * shifted
    return acc


def _silu_from_half(hx):
    return hx + hx * jnp.tanh(hx)


def _in_half(hf, x_ref, mod_ref, nw_ref, wm_ref, wab_ref, cw_ref, scw_ref, gp_ref,
             proj_ref, gb_ref, e_s, p_s):
    half = x_ref.shape[0] // 2
    r0 = hf * half
    rows = slice(r0, r0 + half)
    erows = slice(HALO + r0, HALO + r0 + half)
    wrows = slice(r0, r0 + half + HALO)

    x = x_ref[rows, :]
    h = x * _rms_scale(x) * (nw_ref[...] * (1.0 + mod_ref[1:2, :])) + mod_ref[0:1, :]
    hb = h.astype(BF16)
    half_cw = 0.5 * cw_ref[...]
    yield

    def proj(c0, n=D_DN):
        return jnp.dot(hb, wm_ref[:, c0:c0 + n], preferred_element_type=F32)

    for part in range(3):
        e_s[part, erows, :] = proj(part * D_DN)
        yield
    e_s[3, erows, :] = proj(COL_SC)
    p_s[0, rows, :] = proj(COL_Z)
    yield
    p_s[1, rows, :] = proj(COL_SH)
    p_s[2, rows, :] = proj(COL_SB)
    ab = jnp.dot(hb, wab_ref[...], preferred_element_type=F32)
    yield

    for part in range(3):
        c0 = part * D_DN
        y = _silu_from_half(_causal_conv(e_s[part, wrows, :], half_cw, c0))
        if part < 2:
            parts = []
            for hd in range(N_HEADS):
                yh = y[:, hd * HEAD_DIM:(hd + 1) * HEAD_DIM]
                nrm = lax.rsqrt(jnp.sum(yh * yh, axis=-1, keepdims=True) + EPS)
                if part == 0:
                    nrm = nrm * (HEAD_DIM ** -0.5)
                parts.append(yh * nrm)
            y = jnp.concatenate(parts, axis=1)
        proj_ref[rows, c0:c0 + D_DN] = y.astype(BF16)
        yield

    proj_ref[rows, COL_ZG:COL_ZG + D_DN] = _silu_from_half(0.5 * p_s[0, rows, :]).astype(BF16)
    e_s[3, erows, :] = e_s[3, erows, :] * p_s[1, rows, :]
    conv = _causal_conv(e_s[3, wrows, :], scw_ref[...], 0)
    proj_ref[rows, COL_YSC:COL_YSC + D_SC] = (p_s[2, rows, :] * conv).astype(BF16)

    xg = ab + gp_ref[1:2, :]
    softplus = jnp.maximum(xg, 0.0) + jnp.log1p(jnp.exp(-jnp.abs(xg)))
    lane = lax.broadcasted_iota(jnp.int32, ab.shape, 1)
    gb_ref[rows, :] = jnp.where(lane < N_HEADS, -jnp.exp(gp_ref[0:1, :]) * softplus, _sigmoid(ab))


def _in_kernel(x_ref, mod_ref, nw_ref, win_ref, cw_ref, scw_ref, gp_ref,
               proj_ref, gb_ref, e_s, p_s, wm_ref, wab_ref, *, per_b):
    tm = x_ref.shape[0]

    @pl.when(pl.program_id(0) == 0)
    def _():
        n_gate = 2 * N_HEADS
        cb = D_DN // 2
        for r in range(N_MAIN // cb):
            src = r * cb + (n_gate if r * cb >= COL_SB else 0)
            wm_ref[:, r * cb:(r + 1) * cb] = win_ref[src:src + cb, :].T.astype(BF16)
        wab_ref[...] = win_ref[COL_SB:COL_SB + LANES, :].T.astype(BF16)

    @pl.when(pl.program_id(0) % per_b == 0)
    def _():
        e_s[:, 0:HALO, :] = jnp.zeros((e_s.shape[0], HALO, D_DN), F32)

    _interleave([_in_half(hf, x_ref, mod_ref, nw_ref, wm_ref, wab_ref, cw_ref, scw_ref, gp_ref,
                          proj_ref, gb_ref, e_s, p_s) for hf in range(2)], lead=5)
    e_s[:, 0:HALO, :] = e_s[:, tm:tm + HALO, :]


def _in_proj(x2, mod3, nw, w_in, cw, scw, gp, seq):
    t = x2.shape[0]
    tm = TM_IN
    per_b = seq // tm
    const = lambda i: (0, 0)
    return pl.pallas_call(
        functools.partial(_in_kernel, per_b=per_b),
        grid=(t // tm,),
        in_specs=[
            pl.BlockSpec((tm, D_MODEL), lambda i: (i, 0)),
            pl.BlockSpec((None, 6, D_MODEL), lambda i: (i // per_b, 0, 0)),
            pl.BlockSpec((1, D_MODEL), const),
            pl.BlockSpec(w_in.shape, const, pipeline_mode=pl.Buffered(1)),
            pl.BlockSpec((DN_CONV, N_QKV), const),
            pl.BlockSpec((SC_CONV, D_SC), const),
            pl.BlockSpec((8, LANES), const),
        ],
        out_specs=[
            pl.BlockSpec((tm, N_P2), lambda i: (i, 0)),
            pl.BlockSpec((tm, LANES), lambda i: (i, 0)),
        ],
        out_shape=[
            jax.ShapeDtypeStruct((t, N_P2), BF16),
            jax.ShapeDtypeStruct((t, LANES), F32),
        ],
        scratch_shapes=[
            pltpu.VMEM((4, HALO + tm, D_DN), F32),
            pltpu.VMEM((3, tm, D_DN), F32),
            pltpu.VMEM((D_MODEL, N_MAIN), BF16),
            pltpu.VMEM((D_MODEL, LANES), BF16),
        ],
        compiler_params=pltpu.CompilerParams(dimension_semantics=("arbitrary",),
                                             vmem_limit_bytes=VMEM_LIMIT),
        name="in_proj",
    )(x2, mod3, nw, w_in, cw, scw, gp)


def _mix_kernel(proj_ref, gb_ref, x_ref, mod_ref, dnw_ref, wout_ref, pw_ref, out_ref,
                o_s, state_s, aq_s, bb_s, op_s, gam_s, bdm_s, lvl_s, xs_s, mt_s, at_s, gw_s, rv_s):
    n_seq, tq = x_ref.shape[0], x_ref.shape[1]
    t = pl.program_id(1)

    @pl.when(t == 0)
    def _():
        state_s[...] = jnp.zeros_like(state_s)

    cps = tq // CHUNK
    n_chunks = n_seq * cps

    def chunk_rows(c):
        return slice((c % cps) * CHUNK, (c % cps + 1) * CHUNK)

    q_ref = {c: proj_ref.at[c // cps, chunk_rows(c), 0:D_DN] for c in range(n_chunks)}
    k_ref = {c: proj_ref.at[c // cps, chunk_rows(c), D_DN:2 * D_DN] for c in range(n_chunks)}
    v_ref = {c: proj_ref.at[c // cps, chunk_rows(c), 2 * D_DN:3 * D_DN] for c in range(n_chunks)}
    quad = N_HEADS * CHUNK
    row_q = lax.broadcasted_iota(jnp.int32, (CHUNK, quad), 0)
    lane_q = lax.broadcasted_iota(jnp.int32, (CHUNK, quad), 1)
    col_q = lane_q & (CHUNK - 1)
    head_q = lane_q >> 6
    tri_incl = row_q >= col_q
    tri_strict = row_q > col_q
    eye_q = row_q == col_q
    eye_f = eye_q.astype(F32)
    row_l = lax.broadcasted_iota(jnp.int32, (CHUNK, LANES), 0)

    def level_mask(r, c, l):
        return (((r >> l) & 1) == 1) & (((c >> l) & 1) == 0) & ((r >> (l + 1)) == (c >> (l + 1)))

    @pl.when((pl.program_id(0) == 0) & (t == 0))
    def _():
        r2 = lax.broadcasted_iota(jnp.int32, (quad, quad), 0)
        c2 = lax.broadcasted_iota(jnp.int32, (quad, quad), 1)
        same_head = (r2 >> 6) == (c2 >> 6)
        bdm_s[...] = same_head.astype(F32).astype(BF16)
        for l in range(1, 6):
            lm = same_head & level_mask(r2 & (CHUNK - 1), c2 & (CHUNK - 1), l)
            lvl_s[l - 1] = lm.astype(F32).astype(BF16)

    lm0_f = level_mask(row_q, col_q, 0).astype(F32)
    zero_blk = jnp.zeros((CHUNK, HEAD_DIM), BF16)
    zero_sq = jnp.zeros((HEAD_DIM, HEAD_DIM), BF16)

    def head_bcast(cols, width):
        if width == HEAD_DIM:
            return jnp.concatenate([jnp.broadcast_to(c_, (CHUNK, HEAD_DIM)) for c_ in cols], axis=1)
        full = [jnp.broadcast_to(c_, (CHUNK, quad)) for c_ in cols]
        return jnp.where(head_q == 0, full[0],
                         jnp.where(head_q == 1, full[1], jnp.where(head_q == 2, full[2], full[3])))

    def bd_wide(a):
        rows = []
        for h in range(N_HEADS):
            rows.append(jnp.concatenate(
                [a[:, j * HEAD_DIM:(j + 1) * HEAD_DIM] if j == h else zero_blk for j in range(N_HEADS)],
                axis=1))
        return jnp.concatenate(rows, axis=0)

    def tile4(a):
        return jnp.concatenate([a] * N_HEADS, axis=0)

    carried = {}

    def group(gi):
        chunks = range(gi * GROUP_CHUNKS, (gi + 1) * GROUP_CHUNKS)
        sq = chunks[0] // cps
        first_of_seq = chunks[0] % cps == 0
        last_of_seq = chunks[-1] % cps == cps - 1
        rows = {c: slice(c * CHUNK, (c + 1) * CHUNK) for c in chunks}

        for n, c in enumerate(chunks):
            bt = gb_ref[sq, chunk_rows(c), :]
            gcum = bt
            for s in (1, 2, 4, 8, 16, 32):
                gcum = gcum + jnp.where(row_l >= s, pltpu.roll(gcum, s, axis=0), 0.0)
            g_cols = [gcum[:, h:h + 1] for h in range(N_HEADS)]
            b_cols = [bt[:, N_HEADS + h:N_HEADS + h + 1] for h in range(N_HEADS)]
            g_i = head_bcast(g_cols, CHUNK)
            b_i = head_bcast(b_cols, CHUNK)
            gw_s[c] = head_bcast(g_cols, HEAD_DIM)
            g_j = jnp.sum(jnp.where(eye_q, g_i, 0.0), axis=0, keepdims=True)
            b_j = jnp.sum(jnp.where(eye_q, b_i, 0.0), axis=0, keepdims=True)
            rv_s[c, 0:1, :] = b_j
            rv_s[c, 1:2, :] = b_j * jnp.exp(g_j)
            decay = jnp.exp(jnp.where(tri_incl, g_i - g_j, -1e30))

            k16 = k_ref[c][...]
            kq = jnp.concatenate([k16, q_ref[c][...]], axis=0)
            kq_k = lax.dot_general(kq, bd_wide(k16), (((1,), (1,)), ((), ())),
                                   preferred_element_type=F32)
            m_st = jnp.where(tri_strict, kq_k[:CHUNK] * b_i * decay, 0.0)
            at_s[c] = (kq_k[CHUNK:] * decay).astype(BF16)
            xs_s[c] = eye_f - m_st * lm0_f
            mt_s[c] = m_st.astype(BF16)
            if n % 2 == 1:
                yield

        for l in range(1, 6):
            for c in chunks:
                x = xs_s[c]
                x16 = x.astype(BF16)
                y = jnp.dot(x16, tile4(mt_s[c]) * lvl_s[l - 1], preferred_element_type=F32)
                z = jnp.dot(y.astype(BF16), tile4(x16) * bdm_s[...], preferred_element_type=F32)
                xs_s[c] = x - z
            yield

        for n, c in enumerate(chunks):
            k16 = k_ref[c][...]
            q = q_ref[c][...].astype(F32)
            k = k16.astype(F32)
            x = xs_s[c]
            u = jnp.dot((x * rv_s[c, 0:1, :]).astype(BF16), bd_wide(v_ref[c][...]),
                        preferred_element_type=F32)
            w = jnp.dot((x * rv_s[c, 1:2, :]).astype(BF16), bd_wide(k16), preferred_element_type=F32)
            u16 = u.astype(BF16)
            w16 = w.astype(BF16)
            a_wu = jnp.dot(at_s[c], jnp.concatenate([bd_wide(w16), bd_wide(u16)], axis=1),
                           preferred_element_type=F32)
            g_wide = gw_s[c]
            g_last = g_wide[CHUNK - 1:CHUNK, :]
            aq_s[c, 2 * CHUNK:3 * CHUNK, :] = (q * jnp.exp(g_wide) - a_wu[:, :D_DN]).astype(BF16)
            op_s[c] = a_wu[:, D_DN:]
            gam_s[c] = jnp.broadcast_to(jnp.exp(g_last), (8, D_DN))
            k_dec = (k * jnp.exp(g_last - g_wide)).astype(BF16)
            for h in range(N_HEADS):
                hs = slice(h * HEAD_DIM, (h + 1) * HEAD_DIM)
                ab_h = _mm_tn(k_dec[:, hs], jnp.concatenate([w16[:, hs], u16[:, hs]], axis=1))
                aq_s[c, 0:2 * CHUNK, hs] = ab_h[:, :HEAD_DIM].astype(BF16)
                bb_s[c, :, hs] = ab_h[:, HEAD_DIM:]
            if n % 2 == 1:
                yield

        state = state_s[sq] if first_of_seq else carried[sq]
        for c in chunks:
            s16 = state.astype(BF16)
            new_parts = []
            o_parts_c = []
            for p in range(N_HEADS // 2):
                ps = slice(p * 2 * HEAD_DIM, (p + 1) * 2 * HEAD_DIM)
                s_a = s16[:, (2 * p) * HEAD_DIM:(2 * p + 1) * HEAD_DIM]
                s_b = s16[:, (2 * p + 1) * HEAD_DIM:(2 * p + 2) * HEAD_DIM]
                bd = jnp.concatenate([jnp.concatenate([s_a, zero_sq], axis=1),
                                      jnp.concatenate([zero_sq, s_b], axis=1)], axis=0)
                r = jnp.dot(aq_s[c, :, ps], bd, preferred_element_type=F32)
                new_parts.append(gam_s[c, 0:1, ps] * state[:, ps] + bb_s[c, :, ps] - r[:2 * CHUNK])
                o_parts_c.append(op_s[c, :, ps] + r[2 * CHUNK:])
            state = jnp.concatenate(new_parts, axis=1)
            o_s[rows[c], :] = jnp.concatenate(o_parts_c, axis=1)
        carried[sq] = state
        if last_of_seq:
            state_s[sq] = state
        yield

        rg = slice(chunks[0] * CHUNK, (chunks[-1] + 1) * CHUNK)
        rl = slice((chunks[0] % cps) * CHUNK, (chunks[-1] % cps + 1) * CHUNK)
        zg = proj_ref[sq, rl, COL_ZG:COL_ZG + D_DN].astype(F32)
        o_parts = []
        for h in range(N_HEADS):
            hs = slice(h * HEAD_DIM, (h + 1) * HEAD_DIM)
            oh = o_s[rg, hs]
            o_parts.append(oh * _rms_scale(oh) * dnw_ref[...] * zg[:, hs])
        o_dn = jnp.concatenate(o_parts, axis=1)
        y = (jnp.dot(o_dn.astype(BF16), wout_ref[0:D_DN, :], preferred_element_type=F32)
             + jnp.dot(proj_ref[sq, rl, COL_YSC:COL_YSC + D_SC], wout_ref[D_DN:D_MODEL, :],
                       preferred_element_type=F32))
        out_ref[sq, rl, :] = x_ref[sq, rl, :] + mod_ref[sq, 2:3, :] * (y * _rms_scale(y) * pw_ref[...])

    gps = cps // GROUP_CHUNKS
    order = [sq * gps + g for g in range(gps) for sq in range(n_seq)]
    _interleave([group(gi) for gi in order], lead=2)


def _mixer(proj, gb, x2, mod3, dnw, w_out, pw, bsz, seq):
    tq = TQ_MIX
    ns = MIX_SEQS
    nc = ns * tq // CHUNK

    def tile(b, s):
        return (b, s, 0)

    const = lambda b, s: (0, 0)
    out = pl.pallas_call(
        _mix_kernel,
        grid=(bsz // ns, seq // tq),
        in_specs=[
            pl.BlockSpec((ns, tq, N_P2), tile),
            pl.BlockSpec((ns, tq, LANES), tile),
            pl.BlockSpec((ns, tq, D_MODEL), tile),
            pl.BlockSpec((ns, 6, D_MODEL), lambda b, s: (b, 0, 0)),
            pl.BlockSpec((1, HEAD_DIM), const),
            pl.BlockSpec((D_MODEL, D_MODEL), const),
            pl.BlockSpec((1, D_MODEL), const),
        ],
        out_specs=pl.BlockSpec((ns, tq, D_MODEL), tile),
        out_shape=jax.ShapeDtypeStruct((bsz, seq, D_MODEL), F32),
        scratch_shapes=[
            pltpu.VMEM((nc * CHUNK, D_DN), F32),
            pltpu.VMEM((ns, HEAD_DIM, D_DN), F32),
            pltpu.VMEM((nc, 3 * CHUNK, D_DN), BF16),
            pltpu.VMEM((nc, HEAD_DIM, D_DN), F32),
            pltpu.VMEM((nc, CHUNK, D_DN), F32),
            pltpu.VMEM((nc, 8, D_DN), F32),
            pltpu.VMEM((N_HEADS * CHUNK, N_HEADS * CHUNK), BF16),
            pltpu.VMEM((5, N_HEADS * CHUNK, N_HEADS * CHUNK), BF16),
            pltpu.VMEM((nc, CHUNK, N_HEADS * CHUNK), F32),
            pltpu.VMEM((nc, CHUNK, N_HEADS * CHUNK), BF16),
            pltpu.VMEM((nc, CHUNK, N_HEADS * CHUNK), BF16),
            pltpu.VMEM((nc, CHUNK, D_DN), F32),
            pltpu.VMEM((nc, 8, N_HEADS * CHUNK), F32),
        ],
        compiler_params=pltpu.CompilerParams(dimension_semantics=("arbitrary", "arbitrary"),
                                             vmem_limit_bytes=VMEM_LIMIT),
        name="token_mixer",
    )(proj.reshape(bsz, seq, N_P2), gb.reshape(bsz, seq, LANES), x2.reshape(bsz, seq, D_MODEL),
      mod3, dnw, w_out, pw)
    return out.reshape(bsz * seq, D_MODEL)


def _ffn_kernel(x_ref, mod_ref, nw_ref, w1_hbm, w2_hbm, pw_ref, out_ref, w1_ref, w2_ref, stage_s, sem):
    @pl.when(pl.program_id(0) == 0)
    def _():
        pieces = []
        for j in range(D_FF // FF_CHUNK):
            cs = slice(j * FF_CHUNK, (j + 1) * FF_CHUNK)
            pieces.append((w1_hbm.at[:, cs], w1_ref.at[:, cs]))
        for j in range(D_FF // FF_CHUNK):
            cs = slice(j * FF_CHUNK, (j + 1) * FF_CHUNK)
            pieces.append((w2_hbm.at[cs, :], w2_ref.at[cs, :]))

        def copy(n):
            return pltpu.make_async_copy(pieces[n][0], stage_s.at[n % 2], sem.at[n % 2])

        copy(0).start()
        for n in range(len(pieces)):
            if n + 1 < len(pieces):
                copy(n + 1).start()
            copy(n).wait()
            pieces[n][1][...] = stage_s[n % 2].astype(BF16)

    x = x_ref[...]
    h = x * _rms_scale(x) * nw_ref[...]
    h = h * (1.0 + mod_ref[4:5, :]) + mod_ref[3:4, :]
    hb = h.astype(BF16)
    y = jnp.zeros(x.shape, F32)
    for j in range(D_FF // FF_CHUNK):
        a = jnp.dot(hb, w1_ref[:, j * FF_CHUNK:(j + 1) * FF_CHUNK], preferred_element_type=F32)
        a = jnp.square(jnp.maximum(a, 0.0)).astype(BF16)
        y = y + jnp.dot(a, w2_ref[j * FF_CHUNK:(j + 1) * FF_CHUNK, :], preferred_element_type=F32)
    out_ref[...] = x + mod_ref[5:6, :] * (y * _rms_scale(y) * pw_ref[...])


def _ffn(x2, mod3, nw, w1, w2, pw, seq):
    t = x2.shape[0]
    tm = TM_FFN
    per_b = seq // tm
    return pl.pallas_call(
        _ffn_kernel,
        grid=(t // tm,),
        in_specs=[
            pl.BlockSpec((tm, D_MODEL), lambda i: (i, 0)),
            pl.BlockSpec((None, 6, D_MODEL), lambda i: (i // per_b, 0, 0)),
            pl.BlockSpec((1, D_MODEL), lambda i: (0, 0)),
            pl.BlockSpec(memory_space=pl.ANY),
            pl.BlockSpec(memory_space=pl.ANY),
            pl.BlockSpec((1, D_MODEL), lambda i: (0, 0)),
        ],
        out_specs=pl.BlockSpec((tm, D_MODEL), lambda i: (i, 0)),
        out_shape=jax.ShapeDtypeStruct((t, D_MODEL), F32),
        scratch_shapes=[
            pltpu.VMEM((D_MODEL, D_FF), BF16),
            pltpu.VMEM((D_FF, D_MODEL), BF16),
            pltpu.VMEM((2, FF_CHUNK, FF_CHUNK), F32),
            pltpu.SemaphoreType.DMA((2,)),
        ],
        compiler_params=pltpu.CompilerParams(dimension_semantics=("arbitrary",),
                                             vmem_limit_bytes=VMEM_LIMIT),
        name="ffn",
    )(x2, mod3, nw, w1, w2, pw)


def _layer(x2, c, w_ada, b_ada, pre_mix_w, post_mix_w, w_in, dn_conv_w, dn_a_log, dn_dt_bias,
           dn_norm_w, sc_conv_w, w_out, pre_ffn_w, post_ffn_w, w_ff1, w_ff2, bsz, seq):
    mod3 = _ada(c, w_ada, b_ada[None, :]).reshape(bsz, 6, D_MODEL)

    gp = jnp.zeros((8, LANES), F32)
    gp = gp.at[0, :N_HEADS].set(dn_a_log).at[1, :N_HEADS].set(dn_dt_bias)
    proj, gb = _in_proj(x2, mod3, pre_mix_w[None, :], w_in.T, dn_conv_w, sc_conv_w, gp, seq)

    x2 = _mixer(proj, gb, x2, mod3, dn_norm_w[None, :], w_out.astype(BF16), post_mix_w[None, :],
                bsz, seq)

    return _ffn(x2, mod3, pre_ffn_w[None, :], w_ff1, w_ff2, post_ffn_w[None, :], seq)


def kernel(x, c, w_ada, b_ada, pre_mix_norm_w, post_mix_norm_w, w_in, dn_conv_w, dn_a_log, dn_dt_bias, dn_norm_w, sc_conv_w, w_out, pre_ffn_norm_w, post_ffn_norm_w, w_ff1, w_ff2):
    bsz, seq, d = x.shape
    x2 = x.reshape(bsz * seq, d)
    for l in range(w_ada.shape[0]):
        x2 = _layer(x2, c, w_ada[l], b_ada[l], pre_mix_norm_w[l], post_mix_norm_w[l], w_in[l],
                    dn_conv_w[l], dn_a_log[l], dn_dt_bias[l], dn_norm_w[l], sc_conv_w[l], w_out[l],
                    pre_ffn_norm_w[l], post_ffn_norm_w[l], w_ff1[l], w_ff2[l], bsz, seq)
    return x2.reshape(bsz, seq, d)
```

```python
import functools

import jax
import jax.numpy as jnp
from jax import lax
from jax.experimental import pallas as pl
from jax.experimental.pallas import tpu as pltpu

F32 = jnp.float32
BF16 = jnp.bfloat16

D_MODEL = 1024
N_HEADS = 4
HEAD_DIM = 128
D_DN = N_HEADS * HEAD_DIM
D_SC = D_MODEL - D_DN
DN_CONV = 4
SC_CONV = 3
CHUNK = 64
D_FF = 4 * D_MODEL
EPS = 1e-6
N_QKV = 3 * D_DN
N_MAIN = 4 * D_DN + 3 * D_SC
COL_Z = N_QKV
COL_SB = 4 * D_DN
COL_SC = COL_SB + D_SC
COL_SH = COL_SC + D_SC
N_P2 = N_QKV + D_DN + D_SC
COL_ZG = N_QKV
COL_YSC = N_QKV + D_DN
LANES = 128
HALO = 8

TM_IN = 512
TQ_MIX = 1024
GROUP_CHUNKS = 4
TM_FFN = 512
FF_CHUNK = 1024
VMEM_LIMIT = 56 * 1024 * 1024


def _sigmoid(x):
    return 0.5 + 0.5 * jnp.tanh(0.5 * x)


def _mm_tn(a, b):
    return lax.dot_general(a, b, (((0,), (0,)), ((), ())), preferred_element_type=F32)


def _rms_scale(x):
    return lax.rsqrt(jnp.mean(x * x, axis=-1, keepdims=True) + EPS)


def _interleave(gens, lead):
    live = []
    pending = list(gens)
    rnd = 0
    while live or pending:
        if pending and rnd % lead == 0:
            live.append(pending.pop(0))
        for g in list(live):
            try:
                next(g)
            except StopIteration:
                live.remove(g)
        rnd += 1


def _ada_kernel(c_ref, w_ref, b_ref, o_ref):
    c = c_ref[...]
    ca = (c * _sigmoid(c)).astype(BF16)
    o_ref[...] = jnp.dot(ca, w_ref[...].astype(BF16), preferred_element_type=F32) + b_ref[...]


def _ada(c, w_ada, b_ada):
    bsz = c.shape[0]
    n = w_ada.shape[1]
    bn = 1024
    return pl.pallas_call(
        _ada_kernel,
        grid=(n // bn,),
        in_specs=[
            pl.BlockSpec((bsz, D_MODEL), lambda j: (0, 0)),
            pl.BlockSpec((D_MODEL, bn), lambda j: (0, j)),
            pl.BlockSpec((1, bn), lambda j: (0, j)),
        ],
        out_specs=pl.BlockSpec((bsz, bn), lambda j: (0, j)),
        out_shape=jax.ShapeDtypeStruct((bsz, n), F32),
        compiler_params=pltpu.CompilerParams(dimension_semantics=("arbitrary",),
                                             vmem_limit_bytes=VMEM_LIMIT),
        name="ada_mod",
    )(c, w_ada, b_ada)


def _causal_conv(ext, w, c0):
    width = w.shape[0]
    r = ext.shape[0] - HALO
    n = ext.shape[1]
    e3 = ext.reshape(r // HALO + 1, HALO, n)
    sub = lax.broadcasted_iota(jnp.int32, (1, HALO, n), 1)

    def delayed(a3, s):
        prev3 = jnp.concatenate([a3[-1:], a3[:-1]], axis=0)
        return pltpu.roll(jnp.where(sub >= HALO - s, prev3, a3), s, axis=1)

    def tap(j):
        return w[j:j + 1, c0:c0 + n]

    assert width in (3, 4), width
    x1 = delayed(e3, 1)
    if width == 4:
        out = tap(3) * e3 + tap(2) * x1 + delayed(tap(1) * e3 + tap(0) * x1, 2)
    else:
        out = tap(2) * e3 + tap(1) * x1 + delayed(tap(0) * e3, 2)
    return out[1:].reshape(r, n)


def _silu_from_half(hx):
    return hx + hx * jnp.tanh(hx)


def _in_half(hf, x_ref, mod_ref, nw_ref, wm_ref, wab_ref, cw_ref, scw_ref, gp_ref,
             proj_ref, gb_ref, e_s, p_s):
    half = x_ref.shape[0] // 2
    r0 = hf * half
    rows = slice(r0, r0 + half)
    erows = slice(HALO + r0, HALO + r0 + half)
    wrows = slice(r0, r0 + half + HALO)

    x = x_ref[rows, :]
    h = x * _rms_scale(x) * (nw_ref[...] * (1.0 + mod_ref[1:2, :])) + mod_ref[0:1, :]
    hb = h.astype(BF16)
    half_cw = 0.5 * cw_ref[...]
    yield

    def proj(c0, n=D_DN):
        return jnp.dot(hb, wm_ref[:, c0:c0 + n], preferred_element_type=F32)

    for part in range(3):
        e_s[part, erows, :] = proj(part * D_DN)
        yield
    e_s[3, erows, :] = proj(COL_SC)
    p_s[0, rows, :] = proj(COL_Z)
    yield
    p_s[1, rows, :] = proj(COL_SH)
    p_s[2, rows, :] = proj(COL_SB)
    ab = jnp.dot(hb, wab_ref[...], preferred_element_type=F32)
    yield

    for part in range(3):
        c0 = part * D_DN
        y = _silu_from_half(_causal_conv(e_s[part, wrows, :], half_cw, c0))
        if part < 2:
            parts = []
            for hd in range(N_HEADS):
                yh = y[:, hd * HEAD_DIM:(hd + 1) * HEAD_DIM]
                nrm = lax.rsqrt(jnp.sum(yh * yh, axis=-1, keepdims=True) + EPS)
                if part == 0:
                    nrm = nrm * (HEAD_DIM ** -0.5)
                parts.append(yh * nrm)
            y = jnp.concatenate(parts, axis=1)
        proj_ref[rows, c0:c0 + D_DN] = y.astype(BF16)
        yield

    proj_ref[rows, COL_ZG:COL_ZG + D_DN] = _silu_from_half(0.5 * p_s[0, rows, :]).astype(BF16)
    e_s[3, erows, :] = e_s[3, erows, :] * p_s[1, rows, :]
    conv = _causal_conv(e_s[3, wrows, :], scw_ref[...], 0)
    proj_ref[rows, COL_YSC:COL_YSC + D_SC] = (p_s[2, rows, :] * conv).astype(BF16)

    xg = ab + gp_ref[1:2, :]
    softplus = jnp.maximum(xg, 0.0) + jnp.log1p(jnp.exp(-jnp.abs(xg)))
    lane = lax.broadcasted_iota(jnp.int32, ab.shape, 1)
    gb_ref[rows, :] = jnp.where(lane < N_HEADS, -jnp.exp(gp_ref[0:1, :]) * softplus, _sigmoid(ab))


def _in_kernel(x_ref, mod_ref, nw_ref, win_ref, cw_ref, scw_ref, gp_ref,
               proj_ref, gb_ref, e_s, p_s, wm_ref, wab_ref, *, per_b):
    tm = x_ref.shape[0]

    @pl.when(pl.program_id(0) == 0)
    def _():
        n_gate = 2 * N_HEADS
        cb = D_DN // 2
        for r in range(N_MAIN // cb):
            src = r * cb + (n_gate if r * cb >= COL_SB else 0)
            wm_ref[:, r * cb:(r + 1) * cb] = win_ref[src:src + cb, :].T.astype(BF16)
        wab_ref[...] = win_ref[COL_SB:COL_SB + LANES, :].T.astype(BF16)

    @pl.when(pl.program_id(0) % per_b == 0)
    def _():
        e_s[:, 0:HALO, :] = jnp.zeros((e_s.shape[0], HALO, D_DN), F32)

    _interleave([_in_half(hf, x_ref, mod_ref, nw_ref, wm_ref, wab_ref, cw_ref, scw_ref, gp_ref,
                          proj_ref, gb_ref, e_s, p_s) for hf in range(2)], lead=5)
    e_s[:, 0:HALO, :] = e_s[:, tm:tm + HALO, :]


def _in_proj(x2, mod3, nw, w_in, cw, scw, gp, seq):
    t = x2.shape[0]
    tm = TM_IN
    per_b = seq // tm
    const = lambda i: (0, 0)
    return pl.pallas_call(
        functools.partial(_in_kernel, per_b=per_b),
        grid=(t // tm,),
        in_specs=[
            pl.BlockSpec((tm, D_MODEL), lambda i: (i, 0)),
            pl.BlockSpec((None, 6, D_MODEL), lambda i: (i // per_b, 0, 0)),
            pl.BlockSpec((1, D_MODEL), const),
            pl.BlockSpec(w_in.shape, const, pipeline_mode=pl.Buffered(1)),
            pl.BlockSpec((DN_CONV, N_QKV), const),
            pl.BlockSpec((SC_CONV, D_SC), const),
            pl.BlockSpec((8, LANES), const),
        ],
        out_specs=[
            pl.BlockSpec((tm, N_P2), lambda i: (i, 0)),
            pl.BlockSpec((tm, LANES), lambda i: (i, 0)),
        ],
        out_shape=[
            jax.ShapeDtypeStruct((t, N_P2), BF16),
            jax.ShapeDtypeStruct((t, LANES), F32),
        ],
        scratch_shapes=[
            pltpu.VMEM((4, HALO + tm, D_DN), F32),
            pltpu.VMEM((3, tm, D_DN), F32),
            pltpu.VMEM((D_MODEL, N_MAIN), BF16),
            pltpu.VMEM((D_MODEL, LANES), BF16),
        ],
        compiler_params=pltpu.CompilerParams(dimension_semantics=("arbitrary",),
                                             vmem_limit_bytes=VMEM_LIMIT),
        name="in_proj",
    )(x2, mod3, nw, w_in, cw, scw, gp)


def _mix_kernel(proj_ref, gb_ref, x_ref, mod_ref, dnw_ref, wout_ref, pw_ref, out_ref,
                o_s, state_s, aq_s, bb_s, op_s, gam_s, bdm_s, lvl_s, xs_s, mt_s, at_s, gw_s, rv_s):
    tq = x_ref.shape[0]
    t = pl.program_id(1)

    @pl.when(t == 0)
    def _():
        state_s[...] = jnp.zeros_like(state_s)

    q_ref = proj_ref.at[:, 0:D_DN]
    k_ref = proj_ref.at[:, D_DN:2 * D_DN]
    v_ref = proj_ref.at[:, 2 * D_DN:3 * D_DN]

    n_chunks = tq // CHUNK
    quad = N_HEADS * CHUNK
    row_q = lax.broadcasted_iota(jnp.int32, (CHUNK, quad), 0)
    lane_q = lax.broadcasted_iota(jnp.int32, (CHUNK, quad), 1)
    col_q = lane_q & (CHUNK - 1)
    head_q = lane_q >> 6
    tri_incl = row_q >= col_q
    tri_strict = row_q > col_q
    eye_q = row_q == col_q
    eye_f = eye_q.astype(F32)
    row_l = lax.broadcasted_iota(jnp.int32, (CHUNK, LANES), 0)

    def level_mask(r, c, l):
        return (((r >> l) & 1) == 1) & (((c >> l) & 1) == 0) & ((r >> (l + 1)) == (c >> (l + 1)))

    @pl.when((pl.program_id(0) == 0) & (t == 0))
    def _():
        r2 = lax.broadcasted_iota(jnp.int32, (quad, quad), 0)
        c2 = lax.broadcasted_iota(jnp.int32, (quad, quad), 1)
        same_head = (r2 >> 6) == (c2 >> 6)
        bdm_s[...] = same_head.astype(F32).astype(BF16)
        for l in range(1, 6):
            lm = same_head & level_mask(r2 & (CHUNK - 1), c2 & (CHUNK - 1), l)
            lvl_s[l - 1] = lm.astype(F32).astype(BF16)

    lm0_f = level_mask(row_q, col_q, 0).astype(F32)
    zero_blk = jnp.zeros((CHUNK, HEAD_DIM), BF16)
    zero_sq = jnp.zeros((HEAD_DIM, HEAD_DIM), BF16)

    def head_bcast(cols, width):
        if width == HEAD_DIM:
            return jnp.concatenate([jnp.broadcast_to(c_, (CHUNK, HEAD_DIM)) for c_ in cols], axis=1)
        full = [jnp.broadcast_to(c_, (CHUNK, quad)) for c_ in cols]
        return jnp.where(head_q == 0, full[0],
                         jnp.where(head_q == 1, full[1], jnp.where(head_q == 2, full[2], full[3])))

    def bd_wide(a):
        rows = []
        for h in range(N_HEADS):
            rows.append(jnp.concatenate(
                [a[:, j * HEAD_DIM:(j + 1) * HEAD_DIM] if j == h else zero_blk for j in range(N_HEADS)],
                axis=1))
        return jnp.concatenate(rows, axis=0)

    def tile4(a):
        return jnp.concatenate([a] * N_HEADS, axis=0)

    carried = {"state": None}

    def group(gi):
        chunks = range(gi * GROUP_CHUNKS, (gi + 1) * GROUP_CHUNKS)
        rows = {c: slice(c * CHUNK, (c + 1) * CHUNK) for c in chunks}

        for n, c in enumerate(chunks):
            bt = gb_ref[rows[c], :]
            gcum = bt
            for s in (1, 2, 4, 8, 16, 32):
                gcum = gcum + jnp.where(row_l >= s, pltpu.roll(gcum, s, axis=0), 0.0)
            g_cols = [gcum[:, h:h + 1] for h in range(N_HEADS)]
            b_cols = [bt[:, N_HEADS + h:N_HEADS + h + 1] for h in range(N_HEADS)]
            g_i = head_bcast(g_cols, CHUNK)
            b_i = head_bcast(b_cols, CHUNK)
            gw_s[c] = head_bcast(g_cols, HEAD_DIM)
            g_j = jnp.sum(jnp.where(eye_q, g_i, 0.0), axis=0, keepdims=True)
            b_j = jnp.sum(jnp.where(eye_q, b_i, 0.0), axis=0, keepdims=True)
            rv_s[c, 0:1, :] = b_j
            rv_s[c, 1:2, :] = b_j * jnp.exp(g_j)
            decay = jnp.exp(jnp.where(tri_incl, g_i - g_j, -1e30))

            k16 = k_ref[rows[c], :]
            kq = jnp.concatenate([k16, q_ref[rows[c], :]], axis=0)
            kq_k = lax.dot_general(kq, bd_wide(k16), (((1,), (1,)), ((), ())),
                                   preferred_element_type=F32)
            m_st = jnp.where(tri_strict, kq_k[:CHUNK] * b_i * decay, 0.0)
            at_s[c] = (kq_k[CHUNK:] * decay).astype(BF16)
            xs_s[c] = eye_f - m_st * lm0_f
            mt_s[c] = m_st.astype(BF16)
            if n % 2 == 1:
                yield

        for l in range(1, 6):
            for c in chunks:
                x = xs_s[c]
                x16 = x.astype(BF16)
                y = jnp.dot(x16, tile4(mt_s[c]) * lvl_s[l - 1], preferred_element_type=F32)
                z = jnp.dot(y.astype(BF16), tile4(x16) * bdm_s[...], preferred_element_type=F32)
                xs_s[c] = x - z
            yield

        for n, c in enumerate(chunks):
            k16 = k_ref[rows[c], :]
            q = q_ref[rows[c], :].astype(F32)
            k = k16.astype(F32)
            x = xs_s[c]
            u = jnp.dot((x * rv_s[c, 0:1, :]).astype(BF16), bd_wide(v_ref[rows[c], :]),
                        preferred_element_type=F32)
            w = jnp.dot((x * rv_s[c, 1:2, :]).astype(BF16), bd_wide(k16), preferred_element_type=F32)
            u16 = u.astype(BF16)
            w16 = w.astype(BF16)
            a_wu = jnp.dot(at_s[c], jnp.concatenate([bd_wide(w16), bd_wide(u16)], axis=1),
                           preferred_element_type=F32)
            g_wide = gw_s[c]
            g_last = g_wide[CHUNK - 1:CHUNK, :]
            aq_s[c, 2 * CHUNK:3 * CHUNK, :] = (q * jnp.exp(g_wide) - a_wu[:, :D_DN]).astype(BF16)
            op_s[c] = a_wu[:, D_DN:]
            gam_s[c] = jnp.broadcast_to(jnp.exp(g_last), (8, D_DN))
            k_dec = (k * jnp.exp(g_last - g_wide)).astype(BF16)
            for h in range(N_HEADS):
                hs = slice(h * HEAD_DIM, (h + 1) * HEAD_DIM)
                ab_h = _mm_tn(k_dec[:, hs], jnp.concatenate([w16[:, hs], u16[:, hs]], axis=1))
                aq_s[c, 0:2 * CHUNK, hs] = ab_h[:, :HEAD_DIM].astype(BF16)
                bb_s[c, :, hs] = ab_h[:, HEAD_DIM:]
            if n % 2 == 1:
                yield

        state = state_s[...] if gi == 0 else carried["state"]
        for c in chunks:
            s16 = state.astype(BF16)
            new_parts = []
            o_parts_c = []
            for p in range(N_HEADS // 2):
                ps = slice(p * 2 * HEAD_DIM, (p + 1) * 2 * HEAD_DIM)
                s_a = s16[:, (2 * p) * HEAD_DIM:(2 * p + 1) * HEAD_DIM]
                s_b = s16[:, (2 * p + 1) * HEAD_DIM:(2 * p + 2) * HEAD_DIM]
                bd = jnp.concatenate([jnp.concatenate([s_a, zero_sq], axis=1),
                                      jnp.concatenate([zero_sq, s_b], axis=1)], axis=0)
                r = jnp.dot(aq_s[c, :, ps], bd, preferred_element_type=F32)
                new_parts.append(gam_s[c, 0:1, ps] * state[:, ps] + bb_s[c, :, ps] - r[:2 * CHUNK])
                o_parts_c.append(op_s[c, :, ps] + r[2 * CHUNK:])
            state = jnp.concatenate(new_parts, axis=1)
            o_s[rows[c], :] = jnp.concatenate(o_parts_c, axis=1)
        carried["state"] = state
        if gi == n_chunks // GROUP_CHUNKS - 1:
            state_s[...] = state
        yield

        rg = slice(gi * GROUP_CHUNKS * CHUNK, (gi + 1) * GROUP_CHUNKS * CHUNK)
        zg = proj_ref[rg, COL_ZG:COL_ZG + D_DN].astype(F32)
        o_parts = []
        for h in range(N_HEADS):
            hs = slice(h * HEAD_DIM, (h + 1) * HEAD_DIM)
            oh = o_s[rg, hs]
            o_parts.append(oh * _rms_scale(oh) * dnw_ref[...] * zg[:, hs])
        o_dn = jnp.concatenate(o_parts, axis=1)
        y = (jnp.dot(o_dn.astype(BF16), wout_ref[0:D_DN, :], preferred_element_type=F32)
             + jnp.dot(proj_ref[rg, COL_YSC:COL_YSC + D_SC], wout_ref[D_DN:D_MODEL, :],
                       preferred_element_type=F32))
        out_ref[rg, :] = x_ref[rg, :] + mod_ref[2:3, :] * (y * _rms_scale(y) * pw_ref[...])

    _interleave([group(gi) for gi in range(n_chunks // GROUP_CHUNKS)], lead=2)


def _mixer(proj, gb, x2, mod3, dnw, w_out, pw, bsz, seq):
    tq = TQ_MIX
    per_b = seq // tq
    t = x2.shape[0]

    def tile(b, s):
        return (b * per_b + s, 0)

    const = lambda b, s: (0, 0)
    return pl.pallas_call(
        _mix_kernel,
        grid=(bsz, per_b),
        in_specs=[
            pl.BlockSpec((tq, N_P2), tile),
            pl.BlockSpec((tq, LANES), tile),
            pl.BlockSpec((tq, D_MODEL), tile),
            pl.BlockSpec((None, 6, D_MODEL), lambda b, s: (b, 0, 0)),
            pl.BlockSpec((1, HEAD_DIM), const),
            pl.BlockSpec((D_MODEL, D_MODEL), const),
            pl.BlockSpec((1, D_MODEL), const),
        ],
        out_specs=pl.BlockSpec((tq, D_MODEL), tile),
        out_shape=jax.ShapeDtypeStruct((t, D_MODEL), F32),
        scratch_shapes=[
            pltpu.VMEM((tq, D_DN), F32),
            pltpu.VMEM((HEAD_DIM, D_DN), F32),
            pltpu.VMEM((tq // CHUNK, 3 * CHUNK, D_DN), BF16),
            pltpu.VMEM((tq // CHUNK, HEAD_DIM, D_DN), F32),
            pltpu.VMEM((tq // CHUNK, CHUNK, D_DN), F32),
            pltpu.VMEM((tq // CHUNK, 8, D_DN), F32),
            pltpu.VMEM((N_HEADS * CHUNK, N_HEADS * CHUNK), BF16),
            pltpu.VMEM((5, N_HEADS * CHUNK, N_HEADS * CHUNK), BF16),
            pltpu.VMEM((tq // CHUNK, CHUNK, N_HEADS * CHUNK), F32),
            pltpu.VMEM((tq // CHUNK, CHUNK, N_HEADS * CHUNK), BF16),
            pltpu.VMEM((tq // CHUNK, CHUNK, N_HEADS * CHUNK), BF16),
            pltpu.VMEM((tq // CHUNK, CHUNK, D_DN), F32),
            pltpu.VMEM((tq // CHUNK, 8, N_HEADS * CHUNK), F32),
        ],
        compiler_params=pltpu.CompilerParams(dimension_semantics=("arbitrary", "arbitrary"),
                                             vmem_limit_bytes=VMEM_LIMIT),
        name="token_mixer",
    )(proj, gb, x2, mod3, dnw, w_out, pw)


def _ffn_kernel(x_ref, mod_ref, nw_ref, w1_hbm, w2_hbm, pw_ref, out_ref, w1_ref, w2_ref, stage_s, sem):
    @pl.when(pl.program_id(0) == 0)
    def _():
        pieces = []
        for j in range(D_FF // FF_CHUNK):
            cs = slice(j * FF_CHUNK, (j + 1) * FF_CHUNK)
            pieces.append((w1_hbm.at[:, cs], w1_ref.at[:, cs]))
        for j in range(D_FF // FF_CHUNK):
            cs = slice(j * FF_CHUNK, (j + 1) * FF_CHUNK)
            pieces.append((w2_hbm.at[cs, :], w2_ref.at[cs, :]))

        def copy(n):
            return pltpu.make_async_copy(pieces[n][0], stage_s.at[n % 2], sem.at[n % 2])

        copy(0).start()
        for n in range(len(pieces)):
            if n + 1 < len(pieces):
                copy(n + 1).start()
            copy(n).wait()
            pieces[n][1][...] = stage_s[n % 2].astype(BF16)

    x = x_ref[...]
    h = x * _rms_scale(x) * nw_ref[...]
    h = h * (1.0 + mod_ref[4:5, :]) + mod_ref[3:4, :]
    hb = h.astype(BF16)
    y = jnp.zeros(x.shape, F32)
    for j in range(D_FF // FF_CHUNK):
        a = jnp.dot(hb, w1_ref[:, j * FF_CHUNK:(j + 1) * FF_CHUNK], preferred_element_type=F32)
        a = jnp.square(jnp.maximum(a, 0.0)).astype(BF16)
        y = y + jnp.dot(a, w2_ref[j * FF_CHUNK:(j + 1) * FF_CHUNK, :], preferred_element_type=F32)
    out_ref[...] = x + mod_ref[5:6, :] * (y * _rms_scale(y) * pw_ref[...])


def _ffn(x2, mod3, nw, w1, w2, pw, seq):
    t = x2.shape[0]
    tm = TM_FFN
    per_b = seq // tm
    return pl.pallas_call(
        _ffn_kernel,
        grid=(t // tm,),
        in_specs=[
            pl.BlockSpec((tm, D_MODEL), lambda i: (i, 0)),
            pl.BlockSpec((None, 6, D_MODEL), lambda i: (i // per_b, 0, 0)),
            pl.BlockSpec((1, D_MODEL), lambda i: (0, 0)),
            pl.BlockSpec(memory_space=pl.ANY),
            pl.BlockSpec(memory_space=pl.ANY),
            pl.BlockSpec((1, D_MODEL), lambda i: (0, 0)),
        ],
        out_specs=pl.BlockSpec((tm, D_MODEL), lambda i: (i, 0)),
        out_shape=jax.ShapeDtypeStruct((t, D_MODEL), F32),
        scratch_shapes=[
            pltpu.VMEM((D_MODEL, D_FF), BF16),
            pltpu.VMEM((D_FF, D_MODEL), BF16),
            pltpu.VMEM((2, FF_CHUNK, FF_CHUNK), F32),
            pltpu.SemaphoreType.DMA((2,)),
        ],
        compiler_params=pltpu.CompilerParams(dimension_semantics=("arbitrary",),
                                             vmem_limit_bytes=VMEM_LIMIT),
        name="ffn",
    )(x2, mod3, nw, w1, w2, pw)


def _layer(x2, c, w_ada, b_ada, pre_mix_w, post_mix_w, w_in, dn_conv_w, dn_a_log, dn_dt_bias,
           dn_norm_w, sc_conv_w, w_out, pre_ffn_w, post_ffn_w, w_ff1, w_ff2, bsz, seq):
    mod3 = _ada(c, w_ada, b_ada[None, :]).reshape(bsz, 6, D_MODEL)

    gp = jnp.zeros((8, LANES), F32)
    gp = gp.at[0, :N_HEADS].set(dn_a_log).at[1, :N_HEADS].set(dn_dt_bias)
    proj, gb = _in_proj(x2, mod3, pre_mix_w[None, :], w_in.T, dn_conv_w, sc_conv_w, gp, seq)

    x2 = _mixer(proj, gb, x2, mod3, dn_norm_w[None, :], w_out.astype(BF16), post_mix_w[None, :],
                bsz, seq)

    return _ffn(x2, mod3, pre_ffn_w[None, :], w_ff1, w_ff2, post_ffn_w[None, :], seq)


def kernel(x, c, w_ada, b_ada, pre_mix_norm_w, post_mix_norm_w, w_in, dn_conv_w, dn_a_log, dn_dt_bias, dn_norm_w, sc_conv_w, w_out, pre_ffn_norm_w, post_ffn_norm_w, w_ff1, w_ff2):
    bsz, seq, d = x.shape
    x2 = x.reshape(bsz * seq, d)
    for l in range(w_ada.shape[0]):
        x2 = _layer(x2, c, w_ada[l], b_ada[l], pre_mix_norm_w[l], post_mix_norm_w[l], w_in[l],
                    dn_conv_w[l], dn_a_log[l], dn_dt_bias[l], dn_norm_w[l], sc_conv_w[l], w_out[l],
                    pre_ffn_norm_w[l], post_ffn_norm_w[l], w_ff1[l], w_ff2[l], bsz, seq)
    return x2.reshape(bsz, seq, d)
```

```python
import functools

import jax
import jax.numpy as jnp
from jax import lax
from jax.experimental import pallas as pl
from jax.experimental.pallas import tpu as pltpu

F32 = jnp.float32
BF16 = jnp.bfloat16

D_MODEL = 1024
N_HEADS = 4
HEAD_DIM = 128
D_DN = N_HEADS * HEAD_DIM
D_SC = D_MODEL - D_DN
DN_CONV = 4
SC_CONV = 3
CHUNK = 64
D_FF = 4 * D_MODEL
EPS = 1e-6
N_QKV = 3 * D_DN
N_MAIN = 4 * D_DN + 3 * D_SC
COL_Z = N_QKV
COL_SB = 4 * D_DN
COL_SC = COL_SB + D_SC
COL_SH = COL_SC + D_SC
N_P2 = N_QKV + D_DN + D_SC
COL_ZG = N_QKV
COL_YSC = N_QKV + D_DN
LANES = 128
HALO = 8

TM_IN = 512
TQ_MIX = 1024
GROUP_CHUNKS = 4
TM_FFN = 512
FF_CHUNK = 1024
VMEM_LIMIT = 56 * 1024 * 1024


def _sigmoid(x):
    return 0.5 + 0.5 * jnp.tanh(0.5 * x)


def _mm_tn(a, b):
    return lax.dot_general(a, b, (((0,), (0,)), ((), ())), preferred_element_type=F32)


def _rms_scale(x):
    return lax.rsqrt(jnp.mean(x * x, axis=-1, keepdims=True) + EPS)


def _interleave(gens, lead):
    live = []
    pending = list(gens)
    rnd = 0
    while live or pending:
        if pending and rnd % lead == 0:
            live.append(pending.pop(0))
        for g in list(live):
            try:
                next(g)
            except StopIteration:
                live.remove(g)
        rnd += 1


def _ada_kernel(c_ref, w_ref, b_ref, o_ref):
    c = c_ref[...]
    ca = (c * _sigmoid(c)).astype(BF16)
    o_ref[...] = jnp.dot(ca, w_ref[...].astype(BF16), preferred_element_type=F32) + b_ref[...]


def _ada(c, w_ada, b_ada):
    bsz = c.shape[0]
    n = w_ada.shape[1]
    bn = 1024
    return pl.pallas_call(
        _ada_kernel,
        grid=(n // bn,),
        in_specs=[
            pl.BlockSpec((bsz, D_MODEL), lambda j: (0, 0)),
            pl.BlockSpec((D_MODEL, bn), lambda j: (0, j)),
            pl.BlockSpec((1, bn), lambda j: (0, j)),
        ],
        out_specs=pl.BlockSpec((bsz, bn), lambda j: (0, j)),
        out_shape=jax.ShapeDtypeStruct((bsz, n), F32),
        compiler_params=pltpu.CompilerParams(dimension_semantics=("arbitrary",),
                                             vmem_limit_bytes=VMEM_LIMIT),
        name="ada_mod",
    )(c, w_ada, b_ada)


def _causal_conv(ext, w, c0):
    width = w.shape[0]
    r = ext.shape[0] - HALO
    n = ext.shape[1]
    e3 = ext.reshape(r // HALO + 1, HALO, n)
    sub = lax.broadcasted_iota(jnp.int32, (1, HALO, n), 1)

    def delayed(a3, s):
        prev3 = jnp.concatenate([a3[-1:], a3[:-1]], axis=0)
        return pltpu.roll(jnp.where(sub >= HALO - s, prev3, a3), s, axis=1)

    def tap(j):
        return w[j:j + 1, c0:c0 + n]

    assert width in (3, 4), width
    x1 = delayed(e3, 1)
    if width == 4:
        out = tap(3) * e3 + tap(2) * x1 + delayed(tap(1) * e3 + tap(0) * x1, 2)
    else:
        out = tap(2) * e3 + tap(1) * x1 + delayed(tap(0) * e3, 2)
    return out[1:].reshape(r, n)


def _silu_from_half(hx):
    return hx + hx * jnp.tanh(hx)


def _in_half(hf, x_ref, mod_ref, nw_ref, wm_ref, wab_ref, cw_ref, scw_ref,
             proj_ref, gb_ref, e_s, p_s):
    half = x_ref.shape[0] // 2
    r0 = hf * half
    rows = slice(r0, r0 + half)
    erows = slice(HALO + r0, HALO + r0 + half)
    wrows = slice(r0, r0 + half + HALO)

    x = x_ref[rows, :]
    h = x * _rms_scale(x) * (nw_ref[...] * (1.0 + mod_ref[1:2, :])) + mod_ref[0:1, :]
    hb = h.astype(BF16)
    half_cw = 0.5 * cw_ref[...]
    yield

    def proj(c0, n=D_DN):
        return jnp.dot(hb, wm_ref[:, c0:c0 + n], preferred_element_type=F32)

    for part in range(3):
        e_s[part, erows, :] = proj(part * D_DN)
        yield
    e_s[3, erows, :] = proj(COL_SC)
    proj_ref[rows, COL_ZG:COL_ZG + D_DN] = proj(COL_Z).astype(BF16)
    yield
    p_s[0, rows, :] = proj(COL_SH)
    p_s[1, rows, :] = proj(COL_SB)
    gb_ref[rows, :] = jnp.dot(hb, wab_ref[...], preferred_element_type=F32)
    yield

    for part in range(3):
        c0 = part * D_DN
        y = _silu_from_half(_causal_conv(e_s[part, wrows, :], half_cw, c0))
        if part < 2:
            parts = []
            for hd in range(N_HEADS):
                yh = y[:, hd * HEAD_DIM:(hd + 1) * HEAD_DIM]
                nrm = lax.rsqrt(jnp.sum(yh * yh, axis=-1, keepdims=True) + EPS)
                if part == 0:
                    nrm = nrm * (HEAD_DIM ** -0.5)
                parts.append(yh * nrm)
            y = jnp.concatenate(parts, axis=1)
        proj_ref[rows, c0:c0 + D_DN] = y.astype(BF16)
        yield

    e_s[3, erows, :] = e_s[3, erows, :] * p_s[0, rows, :]
    conv = _causal_conv(e_s[3, wrows, :], scw_ref[...], 0)
    proj_ref[rows, COL_YSC:COL_YSC + D_SC] = (p_s[1, rows, :] * conv).astype(BF16)


def _in_kernel(x_ref, mod_ref, nw_ref, win_ref, cw_ref, scw_ref,
               proj_ref, gb_ref, e_s, p_s, wm_ref, wab_ref, *, per_b):
    tm = x_ref.shape[0]

    @pl.when(pl.program_id(0) == 0)
    def _():
        n_gate = 2 * N_HEADS
        cb = D_DN // 2
        for r in range(N_MAIN // cb):
            src = r * cb + (n_gate if r * cb >= COL_SB else 0)
            wm_ref[:, r * cb:(r + 1) * cb] = win_ref[src:src + cb, :].T.astype(BF16)
        wab_ref[...] = win_ref[COL_SB:COL_SB + LANES, :].T.astype(BF16)

    @pl.when(pl.program_id(0) % per_b == 0)
    def _():
        e_s[:, 0:HALO, :] = jnp.zeros((e_s.shape[0], HALO, D_DN), F32)

    _interleave([_in_half(hf, x_ref, mod_ref, nw_ref, wm_ref, wab_ref, cw_ref, scw_ref,
                          proj_ref, gb_ref, e_s, p_s) for hf in range(2)], lead=5)
    e_s[:, 0:HALO, :] = e_s[:, tm:tm + HALO, :]


def _in_proj(x2, mod3, nw, w_in, cw, scw, seq):
    t = x2.shape[0]
    tm = TM_IN
    per_b = seq // tm
    const = lambda i: (0, 0)
    return pl.pallas_call(
        functools.partial(_in_kernel, per_b=per_b),
        grid=(t // tm,),
        in_specs=[
            pl.BlockSpec((tm, D_MODEL), lambda i: (i, 0)),
            pl.BlockSpec((None, 6, D_MODEL), lambda i: (i // per_b, 0, 0)),
            pl.BlockSpec((1, D_MODEL), const),
            pl.BlockSpec(w_in.shape, const, pipeline_mode=pl.Buffered(1)),
            pl.BlockSpec((DN_CONV, N_QKV), const),
            pl.BlockSpec((SC_CONV, D_SC), const),
        ],
        out_specs=[
            pl.BlockSpec((tm, N_P2), lambda i: (i, 0)),
            pl.BlockSpec((tm, LANES), lambda i: (i, 0)),
        ],
        out_shape=[
            jax.ShapeDtypeStruct((t, N_P2), BF16),
            jax.ShapeDtypeStruct((t, LANES), F32),
        ],
        scratch_shapes=[
            pltpu.VMEM((4, HALO + tm, D_DN), F32),
            pltpu.VMEM((2, tm, D_DN), F32),
            pltpu.VMEM((D_MODEL, N_MAIN), BF16),
            pltpu.VMEM((D_MODEL, LANES), BF16),
        ],
        compiler_params=pltpu.CompilerParams(dimension_semantics=("arbitrary",),
                                             vmem_limit_bytes=VMEM_LIMIT),
        name="in_proj",
    )(x2, mod3, nw, w_in, cw, scw)


def _mix_kernel(proj_ref, gb_ref, x_ref, mod_ref, gp_ref, dnw_ref, wout_ref, pw_ref, out_ref,
                o_s, state_s, aq_s, bb_s, op_s, gam_s, bdm_s, lvl_s, xs_s, mt_s, at_s, gw_s, rv_s):
    tq = x_ref.shape[0]
    t = pl.program_id(1)

    @pl.when(t == 0)
    def _():
        state_s[...] = jnp.zeros_like(state_s)

    q_ref = proj_ref.at[:, 0:D_DN]
    k_ref = proj_ref.at[:, D_DN:2 * D_DN]
    v_ref = proj_ref.at[:, 2 * D_DN:3 * D_DN]

    n_chunks = tq // CHUNK
    quad = N_HEADS * CHUNK
    row_q = lax.broadcasted_iota(jnp.int32, (CHUNK, quad), 0)
    lane_q = lax.broadcasted_iota(jnp.int32, (CHUNK, quad), 1)
    col_q = lane_q & (CHUNK - 1)
    head_q = lane_q >> 6
    tri_incl = row_q >= col_q
    tri_strict = row_q > col_q
    eye_q = row_q == col_q
    eye_f = eye_q.astype(F32)
    row_l = lax.broadcasted_iota(jnp.int32, (CHUNK, LANES), 0)
    lane_l = lax.broadcasted_iota(jnp.int32, (CHUNK, LANES), 1)

    def level_mask(r, c, l):
        return (((r >> l) & 1) == 1) & (((c >> l) & 1) == 0) & ((r >> (l + 1)) == (c >> (l + 1)))

    @pl.when((pl.program_id(0) == 0) & (t == 0))
    def _():
        r2 = lax.broadcasted_iota(jnp.int32, (quad, quad), 0)
        c2 = lax.broadcasted_iota(jnp.int32, (quad, quad), 1)
        same_head = (r2 >> 6) == (c2 >> 6)
        bdm_s[...] = same_head.astype(F32).astype(BF16)
        for l in range(1, 6):
            lm = same_head & level_mask(r2 & (CHUNK - 1), c2 & (CHUNK - 1), l)
            lvl_s[l - 1] = lm.astype(F32).astype(BF16)

    lm0_f = level_mask(row_q, col_q, 0).astype(F32)
    zero_blk = jnp.zeros((CHUNK, HEAD_DIM), BF16)
    zero_sq = jnp.zeros((HEAD_DIM, HEAD_DIM), BF16)

    def head_bcast(cols, width):
        if width == HEAD_DIM:
            return jnp.concatenate([jnp.broadcast_to(c_, (CHUNK, HEAD_DIM)) for c_ in cols], axis=1)
        full = [jnp.broadcast_to(c_, (CHUNK, quad)) for c_ in cols]
        return jnp.where(head_q == 0, full[0],
                         jnp.where(head_q == 1, full[1], jnp.where(head_q == 2, full[2], full[3])))

    def bd_wide(a):
        rows = []
        for h in range(N_HEADS):
            rows.append(jnp.concatenate(
                [a[:, j * HEAD_DIM:(j + 1) * HEAD_DIM] if j == h else zero_blk for j in range(N_HEADS)],
                axis=1))
        return jnp.concatenate(rows, axis=0)

    def tile4(a):
        return jnp.concatenate([a] * N_HEADS, axis=0)

    carried = {"state": None}

    def group(gi):
        chunks = range(gi * GROUP_CHUNKS, (gi + 1) * GROUP_CHUNKS)
        rows = {c: slice(c * CHUNK, (c + 1) * CHUNK) for c in chunks}

        for n, c in enumerate(chunks):
            ab = gb_ref[rows[c], :]
            xg = ab + gp_ref[1:2, :]
            softplus = jnp.maximum(xg, 0.0) + jnp.log1p(jnp.exp(-jnp.abs(xg)))
            bt = jnp.where(lane_l < N_HEADS, -jnp.exp(gp_ref[0:1, :]) * softplus, _sigmoid(ab))
            gcum = bt
            for s in (1, 2, 4, 8, 16, 32):
                gcum = gcum + jnp.where(row_l >= s, pltpu.roll(gcum, s, axis=0), 0.0)
            g_cols = [gcum[:, h:h + 1] for h in range(N_HEADS)]
            b_cols = [bt[:, N_HEADS + h:N_HEADS + h + 1] for h in range(N_HEADS)]
            g_i = head_bcast(g_cols, CHUNK)
            b_i = head_bcast(b_cols, CHUNK)
            gw_s[c] = head_bcast(g_cols, HEAD_DIM)
            g_j = jnp.sum(jnp.where(eye_q, g_i, 0.0), axis=0, keepdims=True)
            b_j = jnp.sum(jnp.where(eye_q, b_i, 0.0), axis=0, keepdims=True)
            rv_s[c, 0:1, :] = b_j
            rv_s[c, 1:2, :] = b_j * jnp.exp(g_j)
            decay = jnp.exp(jnp.where(tri_incl, g_i - g_j, -1e30))

            k16 = k_ref[rows[c], :]
            kq = jnp.concatenate([k16, q_ref[rows[c], :]], axis=0)
            kq_k = lax.dot_general(kq, bd_wide(k16), (((1,), (1,)), ((), ())),
                                   preferred_element_type=F32)
            m_st = jnp.where(tri_strict, kq_k[:CHUNK] * b_i * decay, 0.0)
            at_s[c] = (kq_k[CHUNK:] * decay).astype(BF16)
            xs_s[c] = eye_f - m_st * lm0_f
            mt_s[c] = m_st.astype(BF16)
            if n % 2 == 1:
                yield

        for l in range(1, 6):
            for c in chunks:
                x = xs_s[c]
                x16 = x.astype(BF16)
                y = jnp.dot(x16, tile4(mt_s[c]) * lvl_s[l - 1], preferred_element_type=F32)
                z = jnp.dot(y.astype(BF16), tile4(x16) * bdm_s[...], preferred_element_type=F32)
                xs_s[c] = x - z
            yield

        for n, c in enumerate(chunks):
            k16 = k_ref[rows[c], :]
            q = q_ref[rows[c], :].astype(F32)
            k = k16.astype(F32)
            x = xs_s[c]
            u = jnp.dot((x * rv_s[c, 0:1, :]).astype(BF16), bd_wide(v_ref[rows[c], :]),
                        preferred_element_type=F32)
            w = jnp.dot((x * rv_s[c, 1:2, :]).astype(BF16), bd_wide(k16), preferred_element_type=F32)
            u16 = u.astype(BF16)
            w16 = w.astype(BF16)
            a_wu = jnp.dot(at_s[c], jnp.concatenate([bd_wide(w16), bd_wide(u16)], axis=1),
                           preferred_element_type=F32)
            g_wide = gw_s[c]
            g_last = g_wide[CHUNK - 1:CHUNK, :]
            aq_s[c, 2 * CHUNK:3 * CHUNK, :] = (q * jnp.exp(g_wide) - a_wu[:, :D_DN]).astype(BF16)
            op_s[c] = a_wu[:, D_DN:]
            gam_s[c] = jnp.broadcast_to(jnp.exp(g_last), (8, D_DN))
            k_dec = (k * jnp.exp(g_last - g_wide)).astype(BF16)
            for h in range(N_HEADS):
                hs = slice(h * HEAD_DIM, (h + 1) * HEAD_DIM)
                ab_h = _mm_tn(k_dec[:, hs], jnp.concatenate([w16[:, hs], u16[:, hs]], axis=1))
                aq_s[c, 0:2 * CHUNK, hs] = ab_h[:, :HEAD_DIM].astype(BF16)
                bb_s[c, :, hs] = ab_h[:, HEAD_DIM:]
            if n % 2 == 1:
                yield

        state = state_s[...] if gi == 0 else carried["state"]
        for c in chunks:
            s16 = state.astype(BF16)
            new_parts = []
            o_parts_c = []
            for p in range(N_HEADS // 2):
                ps = slice(p * 2 * HEAD_DIM, (p + 1) * 2 * HEAD_DIM)
                s_a = s16[:, (2 * p) * HEAD_DIM:(2 * p + 1) * HEAD_DIM]
                s_b = s16[:, (2 * p + 1) * HEAD_DIM:(2 * p + 2) * HEAD_DIM]
                bd = jnp.concatenate([jnp.concatenate([s_a, zero_sq], axis=1),
                                      jnp.concatenate([zero_sq, s_b], axis=1)], axis=0)
                r = jnp.dot(aq_s[c, :, ps], bd, preferred_element_type=F32)
                new_parts.append(gam_s[c, 0:1, ps] * state[:, ps] + bb_s[c, :, ps] - r[:2 * CHUNK])
                o_parts_c.append(op_s[c, :, ps] + r[2 * CHUNK:])
            state = jnp.concatenate(new_parts, axis=1)
            o_s[rows[c], :] = jnp.concatenate(o_parts_c, axis=1)
        carried["state"] = state
        if gi == n_chunks // GROUP_CHUNKS - 1:
            state_s[...] = state
        yield

        rg = slice(gi * GROUP_CHUNKS * CHUNK, (gi + 1) * GROUP_CHUNKS * CHUNK)
        zg = _silu_from_half(0.5 * proj_ref[rg, COL_ZG:COL_ZG + D_DN].astype(F32))
        o_parts = []
        for h in range(N_HEADS):
            hs = slice(h * HEAD_DIM, (h + 1) * HEAD_DIM)
            oh = o_s[rg, hs]
            o_parts.append(oh * _rms_scale(oh) * dnw_ref[...] * zg[:, hs])
        o_dn = jnp.concatenate(o_parts, axis=1)
        y = (jnp.dot(o_dn.astype(BF16), wout_ref[0:D_DN, :], preferred_element_type=F32)
             + jnp.dot(proj_ref[rg, COL_YSC:COL_YSC + D_SC], wout_ref[D_DN:D_MODEL, :],
                       preferred_element_type=F32))
        out_ref[rg, :] = x_ref[rg, :] + mod_ref[2:3, :] * (y * _rms_scale(y) * pw_ref[...])

    _interleave([group(gi) for gi in range(n_chunks // GROUP_CHUNKS)], lead=2)


def _mixer(proj, gb, x2, mod3, gp, dnw, w_out, pw, bsz, seq):
    tq = TQ_MIX
    per_b = seq // tq
    t = x2.shape[0]

    def tile(b, s):
        return (b * per_b + s, 0)

    const = lambda b, s: (0, 0)
    return pl.pallas_call(
        _mix_kernel,
        grid=(bsz, per_b),
        in_specs=[
            pl.BlockSpec((tq, N_P2), tile),
            pl.BlockSpec((tq, LANES), tile),
            pl.BlockSpec((tq, D_MODEL), tile),
            pl.BlockSpec((None, 6, D_MODEL), lambda b, s: (b, 0, 0)),
            pl.BlockSpec((8, LANES), const),
            pl.BlockSpec((1, HEAD_DIM), const),
            pl.BlockSpec((D_MODEL, D_MODEL), const),
            pl.BlockSpec((1, D_MODEL), const),
        ],
        out_specs=pl.BlockSpec((tq, D_MODEL), tile),
        out_shape=jax.ShapeDtypeStruct((t, D_MODEL), F32),
        scratch_shapes=[
            pltpu.VMEM((tq, D_DN), F32),
            pltpu.VMEM((HEAD_DIM, D_DN), F32),
            pltpu.VMEM((tq // CHUNK, 3 * CHUNK, D_DN), BF16),
            pltpu.VMEM((tq // CHUNK, HEAD_DIM, D_DN), F32),
            pltpu.VMEM((tq // CHUNK, CHUNK, D_DN), F32),
            pltpu.VMEM((tq // CHUNK, 8, D_DN), F32),
            pltpu.VMEM((N_HEADS * CHUNK, N_HEADS * CHUNK), BF16),
            pltpu.VMEM((5, N_HEADS * CHUNK, N_HEADS * CHUNK), BF16),
            pltpu.VMEM((tq // CHUNK, CHUNK, N_HEADS * CHUNK), F32),
            pltpu.VMEM((tq // CHUNK, CHUNK, N_HEADS * CHUNK), BF16),
            pltpu.VMEM((tq // CHUNK, CHUNK, N_HEADS * CHUNK), BF16),
            pltpu.VMEM((tq // CHUNK, CHUNK, D_DN), F32),
            pltpu.VMEM((tq // CHUNK, 8, N_HEADS * CHUNK), F32),
        ],
        compiler_params=pltpu.CompilerParams(dimension_semantics=("arbitrary", "arbitrary"),
                                             vmem_limit_bytes=VMEM_LIMIT),
        name="token_mixer",
    )(proj, gb, x2, mod3, gp, dnw, w_out, pw)


def _ffn_kernel(x_ref, mod_ref, nw_ref, w1_hbm, w2_hbm, pw_ref, out_ref, w1_ref, w2_ref, stage_s, sem):
    @pl.when(pl.program_id(0) == 0)
    def _():
        pieces = []
        for j in range(D_FF // FF_CHUNK):
            cs = slice(j * FF_CHUNK, (j + 1) * FF_CHUNK)
            pieces.append((w1_hbm.at[:, cs], w1_ref.at[:, cs]))
        for j in range(D_FF // FF_CHUNK):
            cs = slice(j * FF_CHUNK, (j + 1) * FF_CHUNK)
            pieces.append((w2_hbm.at[cs, :], w2_ref.at[cs, :]))

        def copy(n):
            return pltpu.make_async_copy(pieces[n][0], stage_s.at[n % 2], sem.at[n % 2])

        copy(0).start()
        for n in range(len(pieces)):
            if n + 1 < len(pieces):
                copy(n + 1).start()
            copy(n).wait()
            pieces[n][1][...] = stage_s[n % 2].astype(BF16)

    x = x_ref[...]
    h = x * _rms_scale(x) * nw_ref[...]
    h = h * (1.0 + mod_ref[4:5, :]) + mod_ref[3:4, :]
    hb = h.astype(BF16)
    y = jnp.zeros(x.shape, F32)
    for j in range(D_FF // FF_CHUNK):
        a = jnp.dot(hb, w1_ref[:, j * FF_CHUNK:(j + 1) * FF_CHUNK], preferred_element_type=F32)
        a = jnp.square(jnp.maximum(a, 0.0)).astype(BF16)
        y = y + jnp.dot(a, w2_ref[j * FF_CHUNK:(j + 1) * FF_CHUNK, :], preferred_element_type=F32)
    out_ref[...] = x + mod_ref[5:6, :] * (y * _rms_scale(y) * pw_ref[...])


def _ffn(x2, mod3, nw, w1, w2, pw, seq):
    t = x2.shape[0]
    tm = TM_FFN
    per_b = seq // tm
    return pl.pallas_call(
        _ffn_kernel,
        grid=(t // tm,),
        in_specs=[
            pl.BlockSpec((tm, D_MODEL), lambda i: (i, 0)),
            pl.BlockSpec((None, 6, D_MODEL), lambda i: (i // per_b, 0, 0)),
            pl.BlockSpec((1, D_MODEL), lambda i: (0, 0)),
            pl.BlockSpec(memory_space=pl.ANY),
            pl.BlockSpec(memory_space=pl.ANY),
            pl.BlockSpec((1, D_MODEL), lambda i: (0, 0)),
        ],
        out_specs=pl.BlockSpec((tm, D_MODEL), lambda i: (i, 0)),
        out_shape=jax.ShapeDtypeStruct((t, D_MODEL), F32),
        scratch_shapes=[
            pltpu.VMEM((D_MODEL, D_FF), BF16),
            pltpu.VMEM((D_FF, D_MODEL), BF16),
            pltpu.VMEM((2, FF_CHUNK, FF_CHUNK), F32),
            pltpu.SemaphoreType.DMA((2,)),
        ],
        compiler_params=pltpu.CompilerParams(dimension_semantics=("arbitrary",),
                                             vmem_limit_bytes=VMEM_LIMIT),
        name="ffn",
    )(x2, mod3, nw, w1, w2, pw)


def _layer(x2, c, w_ada, b_ada, pre_mix_w, post_mix_w, w_in, dn_conv_w, dn_a_log, dn_dt_bias,
           dn_norm_w, sc_conv_w, w_out, pre_ffn_w, post_ffn_w, w_ff1, w_ff2, bsz, seq):
    mod3 = _ada(c, w_ada, b_ada[None, :]).reshape(bsz, 6, D_MODEL)

    proj, gb = _in_proj(x2, mod3, pre_mix_w[None, :], w_in.T, dn_conv_w, sc_conv_w, seq)

    gp = jnp.zeros((8, LANES), F32)
    gp = gp.at[0, :N_HEADS].set(dn_a_log).at[1, :N_HEADS].set(dn_dt_bias)
    x2 = _mixer(proj, gb, x2, mod3, gp, dn_norm_w[None, :], w_out.astype(BF16), post_mix_w[None, :],
                bsz, seq)

    return _ffn(x2, mod3, pre_ffn_w[None, :], w_ff1, w_ff2, post_ffn_w[None, :], seq)


def kernel(x, c, w_ada, b_ada, pre_mix_norm_w, post_mix_norm_w, w_in, dn_conv_w, dn_a_log, dn_dt_bias, dn_norm_w, sc_conv_w, w_out, pre_ffn_norm_w, post_ffn_norm_w, w_ff1, w_ff2):
    bsz, seq, d = x.shape
    x2 = x.reshape(bsz * seq, d)
    for l in range(w_ada.shape[0]):
        x2 = _layer(x2, c, w_ada[l], b_ada[l], pre_mix_norm_w[l], post_mix_norm_w[l], w_in[l],
                    dn_conv_w[l], dn_a_log[l], dn_dt_bias[l], dn_norm_w[l], sc_conv_w[l], w_out[l],
                    pre_ffn_norm_w[l], post_ffn_norm_w[l], w_ff1[l], w_ff2[l], bsz, seq)
    return x2.reshape(bsz, seq, d)
```

```python
import functools

import jax
import jax.numpy as jnp
from jax import lax
from jax.experimental import pallas as pl
from jax.experimental.pallas import tpu as pltpu

F32 = jnp.float32
BF16 = jnp.bfloat16

D_MODEL = 1024
N_HEADS = 4
HEAD_DIM = 128
D_DN = N_HEADS * HEAD_DIM
D_SC = D_MODEL - D_DN
DN_CONV = 4
SC_CONV = 3
CHUNK = 64
D_FF = 4 * D_MODEL
EPS = 1e-6
N_QKV = 3 * D_DN
N_MAIN = 4 * D_DN + 3 * D_SC
COL_Z = N_QKV
COL_SB = 4 * D_DN
COL_SC = COL_SB + D_SC
COL_SH = COL_SC + D_SC
N_P2 = N_QKV + D_DN + D_SC
COL_ZG = N_QKV
COL_YSC = N_QKV + D_DN
LANES = 128
HALO = 8

TM_IN = 512
TQ_MIX = 1024
GROUP_CHUNKS = 4
TM_FFN = 512
FF_CHUNK = 1024
VMEM_LIMIT = 56 * 1024 * 1024


def _sigmoid(x):
    return 0.5 + 0.5 * jnp.tanh(0.5 * x)


def _mm_tn(a, b):
    return lax.dot_general(a, b, (((0,), (0,)), ((), ())), preferred_element_type=F32)


def _rms_scale(x):
    return lax.rsqrt(jnp.mean(x * x, axis=-1, keepdims=True) + EPS)


def _interleave(gens, lead):
    live = []
    pending = list(gens)
    rnd = 0
    while live or pending:
        if pending and rnd % lead == 0:
            live.append(pending.pop(0))
        for g in list(live):
            try:
                next(g)
            except StopIteration:
                live.remove(g)
        rnd += 1


def _ada_kernel(c_ref, w_ref, b_ref, o_ref):
    c = c_ref[...]
    ca = (c * _sigmoid(c)).astype(BF16)
    o_ref[...] = jnp.dot(ca, w_ref[...].astype(BF16), preferred_element_type=F32) + b_ref[...]


def _ada(c, w_ada, b_ada):
    bsz = c.shape[0]
    n = w_ada.shape[1]
    bn = 1024
    return pl.pallas_call(
        _ada_kernel,
        grid=(n // bn,),
        in_specs=[
            pl.BlockSpec((bsz, D_MODEL), lambda j: (0, 0)),
            pl.BlockSpec((D_MODEL, bn), lambda j: (0, j)),
            pl.BlockSpec((1, bn), lambda j: (0, j)),
        ],
        out_specs=pl.BlockSpec((bsz, bn), lambda j: (0, j)),
        out_shape=jax.ShapeDtypeStruct((bsz, n), F32),
        compiler_params=pltpu.CompilerParams(dimension_semantics=("arbitrary",),
                                             vmem_limit_bytes=VMEM_LIMIT),
        name="ada_mod",
    )(c, w_ada, b_ada)


def _causal_conv(ext, w, c0):
    width = w.shape[0]
    r = ext.shape[0] - HALO
    n = ext.shape[1]
    e3 = ext.reshape(r // HALO + 1, HALO, n)
    sub = lax.broadcasted_iota(jnp.int32, (1, HALO, n), 1)

    def delayed(a3, s):
        prev3 = jnp.concatenate([a3[-1:], a3[:-1]], axis=0)
        return pltpu.roll(jnp.where(sub >= HALO - s, prev3, a3), s, axis=1)

    def tap(j):
        return w[j:j + 1, c0:c0 + n]

    assert width in (3, 4), width
    x1 = delayed(e3, 1)
    if width == 4:
        out = tap(3) * e3 + tap(2) * x1 + delayed(tap(1) * e3 + tap(0) * x1, 2)
    else:
        out = tap(2) * e3 + tap(1) * x1 + delayed(tap(0) * e3, 2)
    return out[1:].reshape(r, n)


def _silu_from_half(hx):
    return hx + hx * jnp.tanh(hx)


def _in_half(hf, x_ref, mod_ref, nw_ref, wm_ref, wab_ref, cw_ref, scw_ref,
             proj_ref, gb_ref, e_s, p_s):
    half = x_ref.shape[0] // 2
    r0 = hf * half
    rows = slice(r0, r0 + half)
    erows = slice(HALO + r0, HALO + r0 + half)
    wrows = slice(r0, r0 + half + HALO)

    x = x_ref[rows, :]
    h = x * _rms_scale(x) * (nw_ref[...] * (1.0 + mod_ref[1:2, :])) + mod_ref[0:1, :]
    hb = h.astype(BF16)
    half_cw = 0.5 * cw_ref[...]
    yield

    def proj(c0, n=D_DN):
        return jnp.dot(hb, wm_ref[:, c0:c0 + n], preferred_element_type=F32)

    for part in range(3):
        e_s[part, erows, :] = proj(part * D_DN)
        yield
    e_s[3, erows, :] = proj(COL_SC)
    proj_ref[rows, COL_ZG:COL_ZG + D_DN] = proj(COL_Z).astype(BF16)
    yield
    p_s[0, rows, :] = proj(COL_SH)
    p_s[1, rows, :] = proj(COL_SB)
    gb_ref[rows, :] = jnp.dot(hb, wab_ref[...], preferred_element_type=F32)
    yield

    for part in range(3):
        c0 = part * D_DN
        y = _silu_from_half(_causal_conv(e_s[part, wrows, :], half_cw, c0))
        if part < 2:
            parts = []
            for hd in range(N_HEADS):
                yh = y[:, hd * HEAD_DIM:(hd + 1) * HEAD_DIM]
                nrm = lax.rsqrt(jnp.sum(yh * yh, axis=-1, keepdims=True) + EPS)
                if part == 0:
                    nrm = nrm * (HEAD_DIM ** -0.5)
                parts.append(yh * nrm)
            y = jnp.concatenate(parts, axis=1)
        proj_ref[rows, c0:c0 + D_DN] = y.astype(BF16)
        yield

    e_s[3, erows, :] = e_s[3, erows, :] * p_s[0, rows, :]
    conv = _causal_conv(e_s[3, wrows, :], scw_ref[...], 0)
    proj_ref[rows, COL_YSC:COL_YSC + D_SC] = (p_s[1, rows, :] * conv).astype(BF16)


def _in_kernel(x_ref, mod_ref, nw_ref, win_ref, cw_ref, scw_ref,
               proj_ref, gb_ref, e_s, p_s, wm_ref, wab_ref, *, per_b):
    tm = x_ref.shape[0]

    @pl.when(pl.program_id(0) == 0)
    def _():
        n_gate = 2 * N_HEADS
        cb = D_DN // 2
        for r in range(N_MAIN // cb):
            src = r * cb + (n_gate if r * cb >= COL_SB else 0)
            wm_ref[:, r * cb:(r + 1) * cb] = win_ref[src:src + cb, :].T.astype(BF16)
        wab_ref[...] = win_ref[COL_SB:COL_SB + LANES, :].T.astype(BF16)

    @pl.when(pl.program_id(0) % per_b == 0)
    def _():
        e_s[:, 0:HALO, :] = jnp.zeros((e_s.shape[0], HALO, D_DN), F32)

    _interleave([_in_half(hf, x_ref, mod_ref, nw_ref, wm_ref, wab_ref, cw_ref, scw_ref,
                          proj_ref, gb_ref, e_s, p_s) for hf in range(2)], lead=5)
    e_s[:, 0:HALO, :] = e_s[:, tm:tm + HALO, :]


def _in_proj(x2, mod3, nw, w_in, cw, scw, seq):
    t = x2.shape[0]
    tm = TM_IN
    per_b = seq // tm
    const = lambda i: (0, 0)
    return pl.pallas_call(
        functools.partial(_in_kernel, per_b=per_b),
        grid=(t // tm,),
        in_specs=[
            pl.BlockSpec((tm, D_MODEL), lambda i: (i, 0)),
            pl.BlockSpec((None, 6, D_MODEL), lambda i: (i // per_b, 0, 0)),
            pl.BlockSpec((1, D_MODEL), const),
            pl.BlockSpec(w_in.shape, const, pipeline_mode=pl.Buffered(1)),
            pl.BlockSpec((DN_CONV, N_QKV), const),
            pl.BlockSpec((SC_CONV, D_SC), const),
        ],
        out_specs=[
            pl.BlockSpec((tm, N_P2), lambda i: (i, 0)),
            pl.BlockSpec((tm, LANES), lambda i: (i, 0)),
        ],
        out_shape=[
            jax.ShapeDtypeStruct((t, N_P2), BF16),
            jax.ShapeDtypeStruct((t, LANES), F32),
        ],
        scratch_shapes=[
            pltpu.VMEM((4, HALO + tm, D_DN), F32),
            pltpu.VMEM((2, tm, D_DN), F32),
            pltpu.VMEM((D_MODEL, N_MAIN), BF16),
            pltpu.VMEM((D_MODEL, LANES), BF16),
        ],
        compiler_params=pltpu.CompilerParams(dimension_semantics=("arbitrary",),
                                             vmem_limit_bytes=VMEM_LIMIT),
        name="in_proj",
    )(x2, mod3, nw, w_in, cw, scw)


def _mix_kernel(proj_ref, gb_ref, x_ref, mod_ref, gp_ref, dnw_ref, wout_ref, pw_ref, out_ref,
                o_s, state_s, aq_s, bb_s, op_s, gam_s, bdm_s, lvl_s, xs_s, mt_s, at_s, gw_s, rv_s):
    tq = x_ref.shape[0]
    t = pl.program_id(1)

    @pl.when(t == 0)
    def _():
        state_s[...] = jnp.zeros_like(state_s)

    q_ref = proj_ref.at[:, 0:D_DN]
    k_ref = proj_ref.at[:, D_DN:2 * D_DN]
    v_ref = proj_ref.at[:, 2 * D_DN:3 * D_DN]

    n_chunks = tq // CHUNK
    quad = N_HEADS * CHUNK
    row_q = lax.broadcasted_iota(jnp.int32, (CHUNK, quad), 0)
    lane_q = lax.broadcasted_iota(jnp.int32, (CHUNK, quad), 1)
    col_q = lane_q & (CHUNK - 1)
    tri_incl = row_q >= col_q
    tri_strict = row_q > col_q
    eye_q = row_q == col_q
    eye_f = eye_q.astype(F32)
    lane_l = lax.broadcasted_iota(jnp.int32, (CHUNK, LANES), 1)
    low_half = lane_l < CHUNK
    sub_l = lax.broadcasted_iota(jnp.int32, (1, HALO, LANES), 1)

    def chunk_cumsum(a):
        a3 = a.reshape(CHUNK // HALO, HALO, LANES)
        for s in (1, 2, 4):
            prev3 = jnp.concatenate([jnp.zeros((1, HALO, LANES), F32), a3[:-1]], axis=0)
            a3 = a3 + pltpu.roll(jnp.where(sub_l >= HALO - s, prev3, a3), s, axis=1)
        for t_ in (1, 2, 4):
            a3 = a3 + jnp.concatenate([jnp.zeros((t_, HALO, LANES), F32), a3[:-t_]], axis=0)
        return a3.reshape(CHUNK, LANES)

    def level_mask(r, c, l):
        return (((r >> l) & 1) == 1) & (((c >> l) & 1) == 0) & ((r >> (l + 1)) == (c >> (l + 1)))

    @pl.when((pl.program_id(0) == 0) & (t == 0))
    def _():
        r2 = lax.broadcasted_iota(jnp.int32, (quad, quad), 0)
        c2 = lax.broadcasted_iota(jnp.int32, (quad, quad), 1)
        same_head = (r2 >> 6) == (c2 >> 6)
        bdm_s[...] = same_head.astype(F32).astype(BF16)
        for l in range(1, 6):
            lm = same_head & level_mask(r2 & (CHUNK - 1), c2 & (CHUNK - 1), l)
            lvl_s[l - 1] = lm.astype(F32).astype(BF16)

    lm0_f = level_mask(row_q, col_q, 0).astype(F32)
    zero_blk = jnp.zeros((CHUNK, HEAD_DIM), BF16)
    zero_sq = jnp.zeros((HEAD_DIM, HEAD_DIM), BF16)

    def head_bcast(cols, width):
        full = [jnp.broadcast_to(c_, (CHUNK, HEAD_DIM)) for c_ in cols]
        if width == HEAD_DIM:
            return jnp.concatenate(full, axis=1)
        return jnp.concatenate([jnp.where(low_half, full[2 * p], full[2 * p + 1])
                                for p in range(N_HEADS // 2)], axis=1)

    def bd_wide(a):
        rows = []
        for h in range(N_HEADS):
            rows.append(jnp.concatenate(
                [a[:, j * HEAD_DIM:(j + 1) * HEAD_DIM] if j == h else zero_blk for j in range(N_HEADS)],
                axis=1))
        return jnp.concatenate(rows, axis=0)

    def tile4(a):
        return jnp.concatenate([a] * N_HEADS, axis=0)

    carried = {"state": None}

    def group(gi):
        chunks = range(gi * GROUP_CHUNKS, (gi + 1) * GROUP_CHUNKS)
        rows = {c: slice(c * CHUNK, (c + 1) * CHUNK) for c in chunks}

        for n, c in enumerate(chunks):
            ab = gb_ref[rows[c], :]
            xg = ab + gp_ref[1:2, :]
            softplus = jnp.maximum(xg, 0.0) + jnp.log1p(jnp.exp(-jnp.abs(xg)))
            bt = jnp.where(lane_l < N_HEADS, -jnp.exp(gp_ref[0:1, :]) * softplus, _sigmoid(ab))
            gcum = chunk_cumsum(bt)
            g_cols = [gcum[:, h:h + 1] for h in range(N_HEADS)]
            b_cols = [bt[:, N_HEADS + h:N_HEADS + h + 1] for h in range(N_HEADS)]
            g_i = head_bcast(g_cols, CHUNK)
            b_i = head_bcast(b_cols, CHUNK)
            gw_s[c] = head_bcast(g_cols, HEAD_DIM)
            g_j = jnp.sum(jnp.where(eye_q, g_i, 0.0), axis=0, keepdims=True)
            b_j = jnp.sum(jnp.where(eye_q, b_i, 0.0), axis=0, keepdims=True)
            rv_s[c, 0:1, :] = b_j
            rv_s[c, 1:2, :] = b_j * jnp.exp(g_j)
            decay = jnp.exp(jnp.where(tri_incl, g_i - g_j, -1e30))

            k16 = k_ref[rows[c], :]
            kq = jnp.concatenate([k16, q_ref[rows[c], :]], axis=0)
            kq_k = lax.dot_general(kq, bd_wide(k16), (((1,), (1,)), ((), ())),
                                   preferred_element_type=F32)
            m_st = jnp.where(tri_strict, kq_k[:CHUNK] * b_i * decay, 0.0)
            at_s[c] = (kq_k[CHUNK:] * decay).astype(BF16)
            xs_s[c] = eye_f - m_st * lm0_f
            mt_s[c] = m_st.astype(BF16)
            if n % 2 == 1:
                yield

        for l in range(1, 6):
            for c in chunks:
                x = xs_s[c]
                x16 = x.astype(BF16)
                y = jnp.dot(x16, tile4(mt_s[c]) * lvl_s[l - 1], preferred_element_type=F32)
                z = jnp.dot(y.astype(BF16), tile4(x16) * bdm_s[...], preferred_element_type=F32)
                xs_s[c] = x - z
            yield

        for n, c in enumerate(chunks):
            k16 = k_ref[rows[c], :]
            q = q_ref[rows[c], :].astype(F32)
            k = k16.astype(F32)
            x = xs_s[c]
            u = jnp.dot((x * rv_s[c, 0:1, :]).astype(BF16), bd_wide(v_ref[rows[c], :]),
                        preferred_element_type=F32)
            w = jnp.dot((x * rv_s[c, 1:2, :]).astype(BF16), bd_wide(k16), preferred_element_type=F32)
            u16 = u.astype(BF16)
            w16 = w.astype(BF16)
            a_wu = jnp.dot(at_s[c], jnp.concatenate([bd_wide(w16), bd_wide(u16)], axis=1),
                           preferred_element_type=F32)
            g_wide = gw_s[c]
            g_last = g_wide[CHUNK - 1:CHUNK, :]
            aq_s[c, 2 * CHUNK:3 * CHUNK, :] = (q * jnp.exp(g_wide) - a_wu[:, :D_DN]).astype(BF16)
            op_s[c] = a_wu[:, D_DN:]
            gam_s[c] = jnp.broadcast_to(jnp.exp(g_last), (8, D_DN))
            k_dec = (k * jnp.exp(g_last - g_wide)).astype(BF16)
            for h in range(N_HEADS):
                hs = slice(h * HEAD_DIM, (h + 1) * HEAD_DIM)
                ab_h = _mm_tn(k_dec[:, hs], jnp.concatenate([w16[:, hs], u16[:, hs]], axis=1))
                aq_s[c, 0:2 * CHUNK, hs] = ab_h[:, :HEAD_DIM].astype(BF16)
                bb_s[c, :, hs] = ab_h[:, HEAD_DIM:]
            if n % 2 == 1:
                yield

        state = state_s[...] if gi == 0 else carried["state"]
        for c in chunks:
            s16 = state.astype(BF16)
            new_parts = []
            o_parts_c = []
            for p in range(N_HEADS // 2):
                ps = slice(p * 2 * HEAD_DIM, (p + 1) * 2 * HEAD_DIM)
                s_a = s16[:, (2 * p) * HEAD_DIM:(2 * p + 1) * HEAD_DIM]
                s_b = s16[:, (2 * p + 1) * HEAD_DIM:(2 * p + 2) * HEAD_DIM]
                bd = jnp.concatenate([jnp.concatenate([s_a, zero_sq], axis=1),
                                      jnp.concatenate([zero_sq, s_b], axis=1)], axis=0)
                r = jnp.dot(aq_s[c, :, ps], bd, preferred_element_type=F32)
                new_parts.append(gam_s[c, 0:1, ps] * state[:, ps] + bb_s[c, :, ps] - r[:2 * CHUNK])
                o_parts_c.append(op_s[c, :, ps] + r[2 * CHUNK:])
            state = jnp.concatenate(new_parts, axis=1)
            o_s[rows[c], :] = jnp.concatenate(o_parts_c, axis=1)
        carried["state"] = state
        if gi == n_chunks // GROUP_CHUNKS - 1:
            state_s[...] = state
        yield

        rg = slice(gi * GROUP_CHUNKS * CHUNK, (gi + 1) * GROUP_CHUNKS * CHUNK)
        zg = _silu_from_half(0.5 * proj_ref[rg, COL_ZG:COL_ZG + D_DN].astype(F32))
        o_parts = []
        for h in range(N_HEADS):
            hs = slice(h * HEAD_DIM, (h + 1) * HEAD_DIM)
            oh = o_s[rg, hs]
            o_parts.append(oh * _rms_scale(oh) * dnw_ref[...] * zg[:, hs])
        mixed = jnp.concatenate([p.astype(BF16) for p in o_parts]
                                + [proj_ref[rg, COL_YSC:COL_YSC + D_SC]], axis=1)
        y = jnp.dot(mixed, wout_ref[...], preferred_element_type=F32)
        out_ref[rg, :] = x_ref[rg, :] + (y * _rms_scale(y)) * (mod_ref[2:3, :] * pw_ref[...])

    _interleave([group(gi) for gi in range(n_chunks // GROUP_CHUNKS)], lead=2)


def _mixer(proj, gb, x2, mod3, gp, dnw, w_out, pw, bsz, seq):
    tq = TQ_MIX
    per_b = seq // tq
    t = x2.shape[0]

    def tile(b, s):
        return (b * per_b + s, 0)

    const = lambda b, s: (0, 0)
    return pl.pallas_call(
        _mix_kernel,
        grid=(bsz, per_b),
        in_specs=[
            pl.BlockSpec((tq, N_P2), tile),
            pl.BlockSpec((tq, LANES), tile),
            pl.BlockSpec((tq, D_MODEL), tile),
            pl.BlockSpec((None, 6, D_MODEL), lambda b, s: (b, 0, 0)),
            pl.BlockSpec((8, LANES), const),
            pl.BlockSpec((1, HEAD_DIM), const),
            pl.BlockSpec((D_MODEL, D_MODEL), const),
            pl.BlockSpec((1, D_MODEL), const),
        ],
        out_specs=pl.BlockSpec((tq, D_MODEL), tile),
        out_shape=jax.ShapeDtypeStruct((t, D_MODEL), F32),
        scratch_shapes=[
            pltpu.VMEM((tq, D_DN), F32),
            pltpu.VMEM((HEAD_DIM, D_DN), F32),
            pltpu.VMEM((tq // CHUNK, 3 * CHUNK, D_DN), BF16),
            pltpu.VMEM((tq // CHUNK, HEAD_DIM, D_DN), F32),
            pltpu.VMEM((tq // CHUNK, CHUNK, D_DN), F32),
            pltpu.VMEM((tq // CHUNK, 8, D_DN), F32),
            pltpu.VMEM((N_HEADS * CHUNK, N_HEADS * CHUNK), BF16),
            pltpu.VMEM((5, N_HEADS * CHUNK, N_HEADS * CHUNK), BF16),
            pltpu.VMEM((tq // CHUNK, CHUNK, N_HEADS * CHUNK), F32),
            pltpu.VMEM((tq // CHUNK, CHUNK, N_HEADS * CHUNK), BF16),
            pltpu.VMEM((tq // CHUNK, CHUNK, N_HEADS * CHUNK), BF16),
            pltpu.VMEM((tq // CHUNK, CHUNK, D_DN), F32),
            pltpu.VMEM((tq // CHUNK, 8, N_HEADS * CHUNK), F32),
        ],
        compiler_params=pltpu.CompilerParams(dimension_semantics=("arbitrary", "arbitrary"),
                                             vmem_limit_bytes=VMEM_LIMIT),
        name="token_mixer",
    )(proj, gb, x2, mod3, gp, dnw, w_out, pw)


def _ffn_kernel(x_ref, mod_ref, nw_ref, w1_hbm, w2_hbm, pw_ref, out_ref, w1_ref, w2_ref, stage_s, sem):
    @pl.when(pl.program_id(0) == 0)
    def _():
        pieces = []
        for j in range(D_FF // FF_CHUNK):
            cs = slice(j * FF_CHUNK, (j + 1) * FF_CHUNK)
            pieces.append((w1_hbm.at[:, cs], w1_ref.at[:, cs]))
        for j in range(D_FF // FF_CHUNK):
            cs = slice(j * FF_CHUNK, (j + 1) * FF_CHUNK)
            pieces.append((w2_hbm.at[cs, :], w2_ref.at[cs, :]))

        def copy(n):
            return pltpu.make_async_copy(pieces[n][0], stage_s.at[n % 2], sem.at[n % 2])

        copy(0).start()
        for n in range(len(pieces)):
            if n + 1 < len(pieces):
                copy(n + 1).start()
            copy(n).wait()
            pieces[n][1][...] = stage_s[n % 2].astype(BF16)

    x = x_ref[...]
    h = x * _rms_scale(x) * nw_ref[...]
    h = h * (1.0 + mod_ref[4:5, :]) + mod_ref[3:4, :]
    hb = h.astype(BF16)
    y = jnp.zeros(x.shape, F32)
    for j in range(D_FF // FF_CHUNK):
        a = jnp.dot(hb, w1_ref[:, j * FF_CHUNK:(j + 1) * FF_CHUNK], preferred_element_type=F32)
        a = jnp.square(jnp.maximum(a, 0.0)).astype(BF16)
        y = y + jnp.dot(a, w2_ref[j * FF_CHUNK:(j + 1) * FF_CHUNK, :], preferred_element_type=F32)
    out_ref[...] = x + mod_ref[5:6, :] * (y * _rms_scale(y) * pw_ref[...])


def _ffn(x2, mod3, nw, w1, w2, pw, seq):
    t = x2.shape[0]
    tm = TM_FFN
    per_b = seq // tm
    return pl.pallas_call(
        _ffn_kernel,
        grid=(t // tm,),
        in_specs=[
            pl.BlockSpec((tm, D_MODEL), lambda i: (i, 0)),
            pl.BlockSpec((None, 6, D_MODEL), lambda i: (i // per_b, 0, 0)),
            pl.BlockSpec((1, D_MODEL), lambda i: (0, 0)),
            pl.BlockSpec(memory_space=pl.ANY),
            pl.BlockSpec(memory_space=pl.ANY),
            pl.BlockSpec((1, D_MODEL), lambda i: (0, 0)),
        ],
        out_specs=pl.BlockSpec((tm, D_MODEL), lambda i: (i, 0)),
        out_shape=jax.ShapeDtypeStruct((t, D_MODEL), F32),
        scratch_shapes=[
            pltpu.VMEM((D_MODEL, D_FF), BF16),
            pltpu.VMEM((D_FF, D_MODEL), BF16),
            pltpu.VMEM((2, FF_CHUNK, FF_CHUNK), F32),
            pltpu.SemaphoreType.DMA((2,)),
        ],
        compiler_params=pltpu.CompilerParams(dimension_semantics=("arbitrary",),
                                             vmem_limit_bytes=VMEM_LIMIT),
        name="ffn",
    )(x2, mod3, nw, w1, w2, pw)


def _layer(x2, c, w_ada, b_ada, pre_mix_w, post_mix_w, w_in, dn_conv_w, dn_a_log, dn_dt_bias,
           dn_norm_w, sc_conv_w, w_out, pre_ffn_w, post_ffn_w, w_ff1, w_ff2, bsz, seq):
    mod3 = _ada(c, w_ada, b_ada[None, :]).reshape(bsz, 6, D_MODEL)

    proj, gb = _in_proj(x2, mod3, pre_mix_w[None, :], w_in.T, dn_conv_w, sc_conv_w, seq)

    gp = jnp.zeros((8, LANES), F32)
    gp = gp.at[0, :N_HEADS].set(dn_a_log).at[1, :N_HEADS].set(dn_dt_bias)
    x2 = _mixer(proj, gb, x2, mod3, gp, dn_norm_w[None, :], w_out.astype(BF16), post_mix_w[None, :],
                bsz, seq)

    return _ffn(x2, mod3, pre_ffn_w[None, :], w_ff1, w_ff2, post_ffn_w[None, :], seq)


def kernel(x, c, w_ada, b_ada, pre_mix_norm_w, post_mix_norm_w, w_in, dn_conv_w, dn_a_log, dn_dt_bias, dn_norm_w, sc_conv_w, w_out, pre_ffn_norm_w, post_ffn_norm_w, w_ff1, w_ff2):
    bsz, seq, d = x.shape
    x2 = x.reshape(bsz * seq, d)
    for l in range(w_ada.shape[0]):
        x2 = _layer(x2, c, w_ada[l], b_ada[l], pre_mix_norm_w[l], post_mix_norm_w[l], w_in[l],
                    dn_conv_w[l], dn_a_log[l], dn_dt_bias[l], dn_norm_w[l], sc_conv_w[l], w_out[l],
                    pre_ffn_norm_w[l], post_ffn_norm_w[l], w_ff1[l], w_ff2[l], bsz, seq)
    return x2.reshape(bsz, seq, d)
```

```python
import functools

import jax
import jax.numpy as jnp
from jax import lax
from jax.experimental import pallas as pl
from jax.experimental.pallas import tpu as pltpu

F32 = jnp.float32
BF16 = jnp.bfloat16

D_MODEL = 1024
N_HEADS = 4
HEAD_DIM = 128
D_DN = N_HEADS * HEAD_DIM
D_SC = D_MODEL - D_DN
DN_CONV = 4
SC_CONV = 3
CHUNK = 64
D_FF = 4 * D_MODEL
EPS = 1e-6
N_QKV = 3 * D_DN
N_MAIN = 4 * D_DN + 3 * D_SC
COL_Z = N_QKV
COL_SB = 4 * D_DN
COL_SC = COL_SB + D_SC
COL_SH = COL_SC + D_SC
N_P2 = N_QKV + D_DN + D_SC
COL_ZG = N_QKV
COL_YSC = N_QKV + D_DN
LANES = 128
HALO = 8

TM_IN = 512
IN_BLOCKS = 4
TQ_MIX = 1024
GROUP_CHUNKS = 4
TM_FFN = 512
FF_CHUNK = 1024
VMEM_LIMIT = 56 * 1024 * 1024


def _sigmoid(x):
    return 0.5 + 0.5 * jnp.tanh(0.5 * x)


def _mm_tn(a, b):
    return lax.dot_general(a, b, (((0,), (0,)), ((), ())), preferred_element_type=F32)


def _rms_scale(x):
    return lax.rsqrt(jnp.mean(x * x, axis=-1, keepdims=True) + EPS)


def _interleave(gens, lead):
    live = []
    pending = list(gens)
    rnd = 0
    while live or pending:
        if pending and rnd % lead == 0:
            live.append(pending.pop(0))
        for g in list(live):
            try:
                next(g)
            except StopIteration:
                live.remove(g)
        rnd += 1


def _ada_kernel(c_ref, w_ref, b_ref, o_ref):
    c = c_ref[...]
    ca = (c * _sigmoid(c)).astype(BF16)
    o_ref[...] = jnp.dot(ca, w_ref[...].astype(BF16), preferred_element_type=F32) + b_ref[...]


def _ada(c, w_ada, b_ada):
    bsz = c.shape[0]
    n = w_ada.shape[1]
    bn = 1024
    return pl.pallas_call(
        _ada_kernel,
        grid=(n // bn,),
        in_specs=[
            pl.BlockSpec((bsz, D_MODEL), lambda j: (0, 0)),
            pl.BlockSpec((D_MODEL, bn), lambda j: (0, j)),
            pl.BlockSpec((1, bn), lambda j: (0, j)),
        ],
        out_specs=pl.BlockSpec((bsz, bn), lambda j: (0, j)),
        out_shape=jax.ShapeDtypeStruct((bsz, n), F32),
        compiler_params=pltpu.CompilerParams(dimension_semantics=("arbitrary",),
                                             vmem_limit_bytes=VMEM_LIMIT),
        name="ada_mod",
    )(c, w_ada, b_ada)


def _causal_conv(ext, w, c0):
    width = w.shape[0]
    r = ext.shape[0] - HALO
    n = ext.shape[1]
    e3 = ext.reshape(r // HALO + 1, HALO, n)
    sub = lax.broadcasted_iota(jnp.int32, (1, HALO, n), 1)

    def delayed(a3, s):
        prev3 = jnp.concatenate([a3[-1:], a3[:-1]], axis=0)
        return pltpu.roll(jnp.where(sub >= HALO - s, prev3, a3), s, axis=1)

    def tap(j):
        return w[j:j + 1, c0:c0 + n]

    assert width in (3, 4), width
    x1 = delayed(e3, 1)
    if width == 4:
        out = tap(3) * e3 + tap(2) * x1 + delayed(tap(1) * e3 + tap(0) * x1, 2)
    else:
        out = tap(2) * e3 + tap(1) * x1 + delayed(tap(0) * e3, 2)
    return out[1:].reshape(r, n)


def _silu_from_half(hx):
    return hx + hx * jnp.tanh(hx)


def _in_block(bi, x_ref, mod_ref, nw_ref, wm_ref, wab_ref, cw_ref, scw_ref, gp_ref,
              proj_ref, gb_ref, e_s, p_s):
    nb = x_ref.shape[0] // IN_BLOCKS
    r0 = bi * nb
    rows = slice(r0, r0 + nb)
    erows = slice(HALO + r0, HALO + r0 + nb)
    wrows = slice(r0, r0 + nb + HALO)

    x = x_ref[rows, :]
    h = x * _rms_scale(x) * (nw_ref[...] * (1.0 + mod_ref[1:2, :])) + mod_ref[0:1, :]
    hb = h.astype(BF16)
    half_cw = 0.5 * cw_ref[...]
    yield

    def proj(c0, n=D_DN):
        return jnp.dot(hb, wm_ref[:, c0:c0 + n], preferred_element_type=F32)

    for part in range(3):
        e_s[part, erows, :] = proj(part * D_DN)
        yield
    e_s[3, erows, :] = proj(COL_SC)
    p_s[0, rows, :] = proj(COL_Z)
    yield
    p_s[1, rows, :] = proj(COL_SH)
    p_s[2, rows, :] = proj(COL_SB)
    ab = jnp.dot(hb, wab_ref[...], preferred_element_type=F32)
    yield

    for part in range(3):
        c0 = part * D_DN
        y = _silu_from_half(_causal_conv(e_s[part, wrows, :], half_cw, c0))
        if part < 2:
            parts = []
            for hd in range(N_HEADS):
                yh = y[:, hd * HEAD_DIM:(hd + 1) * HEAD_DIM]
                nrm = lax.rsqrt(jnp.sum(yh * yh, axis=-1, keepdims=True) + EPS)
                if part == 0:
                    nrm = nrm * (HEAD_DIM ** -0.5)
                parts.append(yh * nrm)
            y = jnp.concatenate(parts, axis=1)
        proj_ref[rows, c0:c0 + D_DN] = y.astype(BF16)
        yield

    proj_ref[rows, COL_ZG:COL_ZG + D_DN] = _silu_from_half(0.5 * p_s[0, rows, :]).astype(BF16)
    e_s[3, erows, :] = e_s[3, erows, :] * p_s[1, rows, :]
    conv = _causal_conv(e_s[3, wrows, :], scw_ref[...], 0)
    proj_ref[rows, COL_YSC:COL_YSC + D_SC] = (p_s[2, rows, :] * conv).astype(BF16)

    xg = ab + gp_ref[1:2, :]
    softplus = jnp.maximum(xg, 0.0) + jnp.log1p(jnp.exp(-jnp.abs(xg)))
    lane = lax.broadcasted_iota(jnp.int32, ab.shape, 1)
    gb_ref[rows, :] = jnp.where(lane < N_HEADS, -jnp.exp(gp_ref[0:1, :]) * softplus, _sigmoid(ab))


def _in_kernel(x_ref, mod_ref, nw_ref, win_ref, cw_ref, scw_ref, gp_ref,
               proj_ref, gb_ref, e_s, p_s, wm_ref, wab_ref, *, per_b):
    tm = x_ref.shape[0]

    @pl.when(pl.program_id(0) == 0)
    def _():
        n_gate = 2 * N_HEADS
        cb = D_DN // 2
        for r in range(N_MAIN // cb):
            src = r * cb + (n_gate if r * cb >= COL_SB else 0)
            wm_ref[:, r * cb:(r + 1) * cb] = win_ref[src:src + cb, :].T.astype(BF16)
        wab_ref[...] = win_ref[COL_SB:COL_SB + LANES, :].T.astype(BF16)

    @pl.when(pl.program_id(0) % per_b == 0)
    def _():
        e_s[:, 0:HALO, :] = jnp.zeros((e_s.shape[0], HALO, D_DN), F32)

    _interleave([_in_block(bi, x_ref, mod_ref, nw_ref, wm_ref, wab_ref, cw_ref, scw_ref, gp_ref,
                           proj_ref, gb_ref, e_s, p_s) for bi in range(IN_BLOCKS)], lead=3)
    e_s[:, 0:HALO, :] = e_s[:, tm:tm + HALO, :]


def _in_proj(x2, mod3, nw, w_in, cw, scw, gp, seq):
    t = x2.shape[0]
    tm = TM_IN
    per_b = seq // tm
    const = lambda i: (0, 0)
    return pl.pallas_call(
        functools.partial(_in_kernel, per_b=per_b),
        grid=(t // tm,),
        in_specs=[
            pl.BlockSpec((tm, D_MODEL), lambda i: (i, 0)),
            pl.BlockSpec((None, 6, D_MODEL), lambda i: (i // per_b, 0, 0)),
            pl.BlockSpec((1, D_MODEL), const),
            pl.BlockSpec(w_in.shape, const, pipeline_mode=pl.Buffered(1)),
            pl.BlockSpec((DN_CONV, N_QKV), const),
            pl.BlockSpec((SC_CONV, D_SC), const),
            pl.BlockSpec((8, LANES), const),
        ],
        out_specs=[
            pl.BlockSpec((tm, N_P2), lambda i: (i, 0)),
            pl.BlockSpec((tm, LANES), lambda i: (i, 0)),
        ],
        out_shape=[
            jax.ShapeDtypeStruct((t, N_P2), BF16),
            jax.ShapeDtypeStruct((t, LANES), F32),
        ],
        scratch_shapes=[
            pltpu.VMEM((4, HALO + tm, D_DN), F32),
            pltpu.VMEM((3, tm, D_DN), F32),
            pltpu.VMEM((D_MODEL, N_MAIN), BF16),
            pltpu.VMEM((D_MODEL, LANES), BF16),
        ],
        compiler_params=pltpu.CompilerParams(dimension_semantics=("arbitrary",),
                                             vmem_limit_bytes=VMEM_LIMIT),
        name="in_proj",
    )(x2, mod3, nw, w_in, cw, scw, gp)


def _mix_kernel(proj_ref, gb_ref, x_ref, mod_ref, dnw_ref, wout_ref, pw_ref, out_ref,
                o_s, state_s, aq_s, bb_s, op_s, gam_s, bdm_s, lvl_s, xs_s, mt_s, at_s, gw_s, rv_s):
    tq = x_ref.shape[0]
    t = pl.program_id(1)

    @pl.when(t == 0)
    def _():
        state_s[...] = jnp.zeros_like(state_s)

    q_ref = proj_ref.at[:, 0:D_DN]
    k_ref = proj_ref.at[:, D_DN:2 * D_DN]
    v_ref = proj_ref.at[:, 2 * D_DN:3 * D_DN]

    n_chunks = tq // CHUNK
    quad = N_HEADS * CHUNK
    row_q = lax.broadcasted_iota(jnp.int32, (CHUNK, quad), 0)
    lane_q = lax.broadcasted_iota(jnp.int32, (CHUNK, quad), 1)
    col_q = lane_q & (CHUNK - 1)
    head_q = lane_q >> 6
    tri_incl = row_q >= col_q
    tri_strict = row_q > col_q
    eye_q = row_q == col_q
    eye_f = eye_q.astype(F32)
    row_l = lax.broadcasted_iota(jnp.int32, (CHUNK, LANES), 0)

    def level_mask(r, c, l):
        return (((r >> l) & 1) == 1) & (((c >> l) & 1) == 0) & ((r >> (l + 1)) == (c >> (l + 1)))

    @pl.when((pl.program_id(0) == 0) & (t == 0))
    def _():
        r2 = lax.broadcasted_iota(jnp.int32, (quad, quad), 0)
        c2 = lax.broadcasted_iota(jnp.int32, (quad, quad), 1)
        same_head = (r2 >> 6) == (c2 >> 6)
        bdm_s[...] = same_head.astype(F32).astype(BF16)
        for l in range(1, 6):
            lm = same_head & level_mask(r2 & (CHUNK - 1), c2 & (CHUNK - 1), l)
            lvl_s[l - 1] = lm.astype(F32).astype(BF16)

    lm0_f = level_mask(row_q, col_q, 0).astype(F32)
    zero_blk = jnp.zeros((CHUNK, HEAD_DIM), BF16)
    zero_sq = jnp.zeros((HEAD_DIM, HEAD_DIM), BF16)

    def head_bcast(cols, width):
        if width == HEAD_DIM:
            return jnp.concatenate([jnp.broadcast_to(c_, (CHUNK, HEAD_DIM)) for c_ in cols], axis=1)
        full = [jnp.broadcast_to(c_, (CHUNK, quad)) for c_ in cols]
        return jnp.where(head_q == 0, full[0],
                         jnp.where(head_q == 1, full[1], jnp.where(head_q == 2, full[2], full[3])))

    def bd_wide(a):
        rows = []
        for h in range(N_HEADS):
            rows.append(jnp.concatenate(
                [a[:, j * HEAD_DIM:(j + 1) * HEAD_DIM] if j == h else zero_blk for j in range(N_HEADS)],
                axis=1))
        return jnp.concatenate(rows, axis=0)

    def tile4(a):
        return jnp.concatenate([a] * N_HEADS, axis=0)

    carried = {"state": None}

    def group(gi):
        chunks = range(gi * GROUP_CHUNKS, (gi + 1) * GROUP_CHUNKS)
        rows = {c: slice(c * CHUNK, (c + 1) * CHUNK) for c in chunks}

        for n, c in enumerate(chunks):
            bt = gb_ref[rows[c], :]
            gcum = bt
            for s in (1, 2, 4, 8, 16, 32):
                gcum = gcum + jnp.where(row_l >= s, pltpu.roll(gcum, s, axis=0), 0.0)
            g_cols = [gcum[:, h:h + 1] for h in range(N_HEADS)]
            b_cols = [bt[:, N_HEADS + h:N_HEADS + h + 1] for h in range(N_HEADS)]
            g_i = head_bcast(g_cols, CHUNK)
            b_i = head_bcast(b_cols, CHUNK)
            gw_s[c] = head_bcast(g_cols, HEAD_DIM)
            g_j = jnp.sum(jnp.where(eye_q, g_i, 0.0), axis=0, keepdims=True)
            b_j = jnp.sum(jnp.where(eye_q, b_i, 0.0), axis=0, keepdims=True)
            rv_s[c, 0:1, :] = b_j
            rv_s[c, 1:2, :] = b_j * jnp.exp(g_j)
            decay = jnp.exp(jnp.where(tri_incl, g_i - g_j, -1e30))

            k16 = k_ref[rows[c], :]
            kq = jnp.concatenate([k16, q_ref[rows[c], :]], axis=0)
            kq_k = lax.dot_general(kq, bd_wide(k16), (((1,), (1,)), ((), ())),
                                   preferred_element_type=F32)
            m_st = jnp.where(tri_strict, kq_k[:CHUNK] * b_i * decay, 0.0)
            at_s[c] = (kq_k[CHUNK:] * decay).astype(BF16)
            xs_s[c] = eye_f - m_st * lm0_f
            mt_s[c] = m_st.astype(BF16)
            if n % 2 == 1:
                yield

        for l in range(1, 6):
            for c in chunks:
                x = xs_s[c]
                x16 = x.astype(BF16)
                y = jnp.dot(x16, tile4(mt_s[c]) * lvl_s[l - 1], preferred_element_type=F32)
                z = jnp.dot(y.astype(BF16), tile4(x16) * bdm_s[...], preferred_element_type=F32)
                xs_s[c] = x - z
            yield

        for n, c in enumerate(chunks):
            k16 = k_ref[rows[c], :]
            q = q_ref[rows[c], :].astype(F32)
            k = k16.astype(F32)
            x = xs_s[c]
            u = jnp.dot((x * rv_s[c, 0:1, :]).astype(BF16), bd_wide(v_ref[rows[c], :]),
                        preferred_element_type=F32)
            w = jnp.dot((x * rv_s[c, 1:2, :]).astype(BF16), bd_wide(k16), preferred_element_type=F32)
            u16 = u.astype(BF16)
            w16 = w.astype(BF16)
            a_wu = jnp.dot(at_s[c], jnp.concatenate([bd_wide(w16), bd_wide(u16)], axis=1),
                           preferred_element_type=F32)
            g_wide = gw_s[c]
            g_last = g_wide[CHUNK - 1:CHUNK, :]
            aq_s[c, 2 * CHUNK:3 * CHUNK, :] = (q * jnp.exp(g_wide) - a_wu[:, :D_DN]).astype(BF16)
            op_s[c] = a_wu[:, D_DN:]
            gam_s[c] = jnp.broadcast_to(jnp.exp(g_last), (8, D_DN))
            k_dec = (k * jnp.exp(g_last - g_wide)).astype(BF16)
            for h in range(N_HEADS):
                hs = slice(h * HEAD_DIM, (h + 1) * HEAD_DIM)
                ab_h = _mm_tn(k_dec[:, hs], jnp.concatenate([w16[:, hs], u16[:, hs]], axis=1))
                aq_s[c, 0:2 * CHUNK, hs] = ab_h[:, :HEAD_DIM].astype(BF16)
                bb_s[c, :, hs] = ab_h[:, HEAD_DIM:]
            if n % 2 == 1:
                yield

        state = state_s[...] if gi == 0 else carried["state"]
        for c in chunks:
            s16 = state.astype(BF16)
            new_parts = []
            o_parts_c = []
            for p in range(N_HEADS // 2):
                ps = slice(p * 2 * HEAD_DIM, (p + 1) * 2 * HEAD_DIM)
                s_a = s16[:, (2 * p) * HEAD_DIM:(2 * p + 1) * HEAD_DIM]
                s_b = s16[:, (2 * p + 1) * HEAD_DIM:(2 * p + 2) * HEAD_DIM]
                bd = jnp.concatenate([jnp.concatenate([s_a, zero_sq], axis=1),
                                      jnp.concatenate([zero_sq, s_b], axis=1)], axis=0)
                r = jnp.dot(aq_s[c, :, ps], bd, preferred_element_type=F32)
                new_parts.append(gam_s[c, 0:1, ps] * state[:, ps] + bb_s[c, :, ps] - r[:2 * CHUNK])
                o_parts_c.append(op_s[c, :, ps] + r[2 * CHUNK:])
            state = jnp.concatenate(new_parts, axis=1)
            o_s[rows[c], :] = jnp.concatenate(o_parts_c, axis=1)
        carried["state"] = state
        if gi == n_chunks // GROUP_CHUNKS - 1:
            state_s[...] = state
        yield

        rg = slice(gi * GROUP_CHUNKS * CHUNK, (gi + 1) * GROUP_CHUNKS * CHUNK)
        zg = proj_ref[rg, COL_ZG:COL_ZG + D_DN].astype(F32)
        o_parts = []
        for h in range(N_HEADS):
            hs = slice(h * HEAD_DIM, (h + 1) * HEAD_DIM)
            oh = o_s[rg, hs]
            o_parts.append(oh * _rms_scale(oh) * dnw_ref[...] * zg[:, hs])
        o_dn = jnp.concatenate(o_parts, axis=1)
        y = (jnp.dot(o_dn.astype(BF16), wout_ref[0:D_DN, :], preferred_element_type=F32)
             + jnp.dot(proj_ref[rg, COL_YSC:COL_YSC + D_SC], wout_ref[D_DN:D_MODEL, :],
                       preferred_element_type=F32))
        out_ref[rg, :] = x_ref[rg, :] + mod_ref[2:3, :] * (y * _rms_scale(y) * pw_ref[...])

    _interleave([group(gi) for gi in range(n_chunks // GROUP_CHUNKS)], lead=2)


def _mixer(proj, gb, x2, mod3, dnw, w_out, pw, bsz, seq):
    tq = TQ_MIX
    per_b = seq // tq
    t = x2.shape[0]

    def tile(b, s):
        return (b * per_b + s, 0)

    const = lambda b, s: (0, 0)
    return pl.pallas_call(
        _mix_kernel,
        grid=(bsz, per_b),
        in_specs=[
            pl.BlockSpec((tq, N_P2), tile),
            pl.BlockSpec((tq, LANES), tile),
            pl.BlockSpec((tq, D_MODEL), tile),
            pl.BlockSpec((None, 6, D_MODEL), lambda b, s: (b, 0, 0)),
            pl.BlockSpec((1, HEAD_DIM), const),
            pl.BlockSpec((D_MODEL, D_MODEL), const),
            pl.BlockSpec((1, D_MODEL), const),
        ],
        out_specs=pl.BlockSpec((tq, D_MODEL), tile),
        out_shape=jax.ShapeDtypeStruct((t, D_MODEL), F32),
        scratch_shapes=[
            pltpu.VMEM((tq, D_DN), F32),
            pltpu.VMEM((HEAD_DIM, D_DN), F32),
            pltpu.VMEM((tq // CHUNK, 3 * CHUNK, D_DN), BF16),
            pltpu.VMEM((tq // CHUNK, HEAD_DIM, D_DN), F32),
            pltpu.VMEM((tq // CHUNK, CHUNK, D_DN), F32),
            pltpu.VMEM((tq // CHUNK, 8, D_DN), F32),
            pltpu.VMEM((N_HEADS * CHUNK, N_HEADS * CHUNK), BF16),
            pltpu.VMEM((5, N_HEADS * CHUNK, N_HEADS * CHUNK), BF16),
            pltpu.VMEM((tq // CHUNK, CHUNK, N_HEADS * CHUNK), F32),
            pltpu.VMEM((tq // CHUNK, CHUNK, N_HEADS * CHUNK), BF16),
            pltpu.VMEM((tq // CHUNK, CHUNK, N_HEADS * CHUNK), BF16),
            pltpu.VMEM((tq // CHUNK, CHUNK, D_DN), F32),
            pltpu.VMEM((tq // CHUNK, 8, N_HEADS * CHUNK), F32),
        ],
        compiler_params=pltpu.CompilerParams(dimension_semantics=("arbitrary", "arbitrary"),
                                             vmem_limit_bytes=VMEM_LIMIT),
        name="token_mixer",
    )(proj, gb, x2, mod3, dnw, w_out, pw)


def _ffn_kernel(x_ref, mod_ref, nw_ref, w1_hbm, w2_hbm, pw_ref, out_ref, w1_ref, w2_ref, stage_s, sem):
    @pl.when(pl.program_id(0) == 0)
    def _():
        pieces = []
        for j in range(D_FF // FF_CHUNK):
            cs = slice(j * FF_CHUNK, (j + 1) * FF_CHUNK)
            pieces.append((w1_hbm.at[:, cs], w1_ref.at[:, cs]))
        for j in range(D_FF // FF_CHUNK):
            cs = slice(j * FF_CHUNK, (j + 1) * FF_CHUNK)
            pieces.append((w2_hbm.at[cs, :], w2_ref.at[cs, :]))

        def copy(n):
            return pltpu.make_async_copy(pieces[n][0], stage_s.at[n % 2], sem.at[n % 2])

        copy(0).start()
        for n in range(len(pieces)):
            if n + 1 < len(pieces):
                copy(n + 1).start()
            copy(n).wait()
            pieces[n][1][...] = stage_s[n % 2].astype(BF16)

    x = x_ref[...]
    h = x * _rms_scale(x) * nw_ref[...]
    h = h * (1.0 + mod_ref[4:5, :]) + mod_ref[3:4, :]
    hb = h.astype(BF16)
    y = jnp.zeros(x.shape, F32)
    for j in range(D_FF // FF_CHUNK):
        a = jnp.dot(hb, w1_ref[:, j * FF_CHUNK:(j + 1) * FF_CHUNK], preferred_element_type=F32)
        a = jnp.square(jnp.maximum(a, 0.0)).astype(BF16)
        y = y + jnp.dot(a, w2_ref[j * FF_CHUNK:(j + 1) * FF_CHUNK, :], preferred_element_type=F32)
    out_ref[...] = x + mod_ref[5:6, :] * (y * _rms_scale(y) * pw_ref[...])


def _ffn(x2, mod3, nw, w1, w2, pw, seq):
    t = x2.shape[0]
    tm = TM_FFN
    per_b = seq // tm
    return pl.pallas_call(
        _ffn_kernel,
        grid=(t // tm,),
        in_specs=[
            pl.BlockSpec((tm, D_MODEL), lambda i: (i, 0)),
            pl.BlockSpec((None, 6, D_MODEL), lambda i: (i // per_b, 0, 0)),
            pl.BlockSpec((1, D_MODEL), lambda i: (0, 0)),
            pl.BlockSpec(memory_space=pl.ANY),
            pl.BlockSpec(memory_space=pl.ANY),
            pl.BlockSpec((1, D_MODEL), lambda i: (0, 0)),
        ],
        out_specs=pl.BlockSpec((tm, D_MODEL), lambda i: (i, 0)),
        out_shape=jax.ShapeDtypeStruct((t, D_MODEL), F32),
        scratch_shapes=[
            pltpu.VMEM((D_MODEL, D_FF), BF16),
            pltpu.VMEM((D_FF, D_MODEL), BF16),
            pltpu.VMEM((2, FF_CHUNK, FF_CHUNK), F32),
            pltpu.SemaphoreType.DMA((2,)),
        ],
        compiler_params=pltpu.CompilerParams(dimension_semantics=("arbitrary",),
                                             vmem_limit_bytes=VMEM_LIMIT),
        name="ffn",
    )(x2, mod3, nw, w1, w2, pw)


def _layer(x2, c, w_ada, b_ada, pre_mix_w, post_mix_w, w_in, dn_conv_w, dn_a_log, dn_dt_bias,
           dn_norm_w, sc_conv_w, w_out, pre_ffn_w, post_ffn_w, w_ff1, w_ff2, bsz, seq):
    mod3 = _ada(c, w_ada, b_ada[None, :]).reshape(bsz, 6, D_MODEL)

    gp = jnp.zeros((8, LANES), F32)
    gp = gp.at[0, :N_HEADS].set(dn_a_log).at[1, :N_HEADS].set(dn_dt_bias)
    proj, gb = _in_proj(x2, mod3, pre_mix_w[None, :], w_in.T, dn_conv_w, sc_conv_w, gp, seq)

    x2 = _mixer(proj, gb, x2, mod3, dn_norm_w[None, :], w_out.astype(BF16), post_mix_w[None, :],
                bsz, seq)

    return _ffn(x2, mod3, pre_ffn_w[None, :], w_ff1, w_ff2, post_ffn_w[None, :], seq)


def kernel(x, c, w_ada, b_ada, pre_mix_norm_w, post_mix_norm_w, w_in, dn_conv_w, dn_a_log, dn_dt_bias, dn_norm_w, sc_conv_w, w_out, pre_ffn_norm_w, post_ffn_norm_w, w_ff1, w_ff2):
    bsz, seq, d = x.shape
    x2 = x.reshape(bsz * seq, d)
    for l in range(w_ada.shape[0]):
        x2 = _layer(x2, c, w_ada[l], b_ada[l], pre_mix_norm_w[l], post_mix_norm_w[l], w_in[l],
                    dn_conv_w[l], dn_a_log[l], dn_dt_bias[l], dn_norm_w[l], sc_conv_w[l], w_out[l],
                    pre_ffn_norm_w[l], post_ffn_norm_w[l], w_ff1[l], w_ff2[l], bsz, seq)
    return x2.reshape(bsz, seq, d)
```

```python
import functools

import jax
import jax.numpy as jnp
from jax import lax
from jax.experimental import pallas as pl
from jax.experimental.pallas import tpu as pltpu

F32 = jnp.float32
BF16 = jnp.bfloat16

D_MODEL = 1024
N_HEADS = 4
HEAD_DIM = 128
D_DN = N_HEADS * HEAD_DIM
D_SC = D_MODEL - D_DN
DN_CONV = 4
SC_CONV = 3
CHUNK = 64
D_FF = 4 * D_MODEL
EPS = 1e-6
N_QKV = 3 * D_DN
N_MAIN = 4 * D_DN + 3 * D_SC
COL_Z = N_QKV
COL_SB = 4 * D_DN
COL_SC = COL_SB + D_SC
COL_SH = COL_SC + D_SC
N_P2 = N_QKV + D_DN + D_SC
COL_ZG = N_QKV
COL_YSC = N_QKV + D_DN
LANES = 128
HALO = 8

TM_IN = 512
IN_BLOCKS = 4
TQ_MIX = 1024
GROUP_CHUNKS = 4
TM_FFN = 512
FF_CHUNK = 1024
VMEM_LIMIT = 56 * 1024 * 1024


def _sigmoid(x):
    return 0.5 + 0.5 * jnp.tanh(0.5 * x)


def _mm_tn(a, b):
    return lax.dot_general(a, b, (((0,), (0,)), ((), ())), preferred_element_type=F32)


def _rms_scale(x):
    return lax.rsqrt(jnp.mean(x * x, axis=-1, keepdims=True) + EPS)


def _interleave(gens, lead):
    live = []
    pending = list(gens)
    rnd = 0
    while live or pending:
        if pending and rnd % lead == 0:
            live.append(pending.pop(0))
        for g in list(live):
            try:
                next(g)
            except StopIteration:
                live.remove(g)
        rnd += 1


def _ada_kernel(c_ref, w_ref, b_ref, o_ref):
    c = c_ref[...]
    ca = (c * _sigmoid(c)).astype(BF16)
    o_ref[...] = jnp.dot(ca, w_ref[...].astype(BF16), preferred_element_type=F32) + b_ref[...]


def _ada(c, w_ada, b_ada):
    bsz = c.shape[0]
    n = w_ada.shape[1]
    bn = 1024
    return pl.pallas_call(
        _ada_kernel,
        grid=(n // bn,),
        in_specs=[
            pl.BlockSpec((bsz, D_MODEL), lambda j: (0, 0)),
            pl.BlockSpec((D_MODEL, bn), lambda j: (0, j)),
            pl.BlockSpec((1, bn), lambda j: (0, j)),
        ],
        out_specs=pl.BlockSpec((bsz, bn), lambda j: (0, j)),
        out_shape=jax.ShapeDtypeStruct((bsz, n), F32),
        compiler_params=pltpu.CompilerParams(dimension_semantics=("arbitrary",),
                                             vmem_limit_bytes=VMEM_LIMIT),
        name="ada_mod",
    )(c, w_ada, b_ada)


def _causal_conv(ext, w, c0):
    width = w.shape[0]
    r = ext.shape[0] - HALO
    n = ext.shape[1]
    e3 = ext.reshape(r // HALO + 1, HALO, n)
    sub = lax.broadcasted_iota(jnp.int32, (1, HALO, n), 1)

    def delayed(a3, s):
        prev3 = jnp.concatenate([a3[-1:], a3[:-1]], axis=0)
        return pltpu.roll(jnp.where(sub >= HALO - s, prev3, a3), s, axis=1)

    def tap(j):
        return w[j:j + 1, c0:c0 + n]

    assert width in (3, 4), width
    x1 = delayed(e3, 1)
    if width == 4:
        out = tap(3) * e3 + tap(2) * x1 + delayed(tap(1) * e3 + tap(0) * x1, 2)
    else:
        out = tap(2) * e3 + tap(1) * x1 + delayed(tap(0) * e3, 2)
    return out[1:].reshape(r, n)


def _silu_from_half(hx):
    return hx + hx * jnp.tanh(hx)


def _in_block(bi, x_ref, mod_ref, nw_ref, wm_ref, wab_ref, cw_ref, scw_ref, gp_ref,
              proj_ref, gb_ref, e_s, p_s):
    nb = x_ref.shape[0] // IN_BLOCKS
    r0 = bi * nb
    rows = slice(r0, r0 + nb)
    erows = slice(HALO + r0, HALO + r0 + nb)
    wrows = slice(r0, r0 + nb + HALO)

    x = x_ref[rows, :]
    h = x * _rms_scale(x) * (nw_ref[...] * (1.0 + mod_ref[1:2, :])) + mod_ref[0:1, :]
    hb = h.astype(BF16)
    half_cw = 0.5 * cw_ref[...]
    yield

    def proj(c0, n=D_DN):
        return jnp.dot(hb, wm_ref[:, c0:c0 + n], preferred_element_type=F32)

    for part in range(3):
        e_s[part, erows, :] = proj(part * D_DN)
        yield
    e_s[3, erows, :] = proj(COL_SC)
    p_s[0, rows, :] = proj(COL_Z)
    yield
    p_s[1, rows, :] = proj(COL_SH)
    p_s[2, rows, :] = proj(COL_SB)
    ab = jnp.dot(hb, wab_ref[...], preferred_element_type=F32)
    yield

    for part in range(3):
        c0 = part * D_DN
        y = _silu_from_half(_causal_conv(e_s[part, wrows, :], half_cw, c0))
        if part < 2:
            parts = []
            for hd in range(N_HEADS):
                yh = y[:, hd * HEAD_DIM:(hd + 1) * HEAD_DIM]
                nrm = lax.rsqrt(jnp.sum(yh * yh, axis=-1, keepdims=True) + EPS)
                if part == 0:
                    nrm = nrm * (HEAD_DIM ** -0.5)
                parts.append(yh * nrm)
            y = jnp.concatenate(parts, axis=1)
        proj_ref[rows, c0:c0 + D_DN] = y.astype(BF16)
        yield

    proj_ref[rows, COL_ZG:COL_ZG + D_DN] = _silu_from_half(0.5 * p_s[0, rows, :]).astype(BF16)
    e_s[3, erows, :] = e_s[3, erows, :] * p_s[1, rows, :]
    conv = _causal_conv(e_s[3, wrows, :], scw_ref[...], 0)
    proj_ref[rows, COL_YSC:COL_YSC + D_SC] = (p_s[2, rows, :] * conv).astype(BF16)

    xg = ab + gp_ref[1:2, :]
    softplus = jnp.maximum(xg, 0.0) + jnp.log1p(jnp.exp(-jnp.abs(xg)))
    lane = lax.broadcasted_iota(jnp.int32, ab.shape, 1)
    gb_ref[rows, :] = jnp.where(lane < N_HEADS, -jnp.exp(gp_ref[0:1, :]) * softplus, _sigmoid(ab))


def _in_kernel(x_ref, mod_ref, nw_ref, win_ref, cw_ref, scw_ref, gp_ref,
               proj_ref, gb_ref, e_s, p_s, wm_ref, wab_ref, *, per_b):
    tm = x_ref.shape[0]

    @pl.when(pl.program_id(0) == 0)
    def _():
        n_gate = 2 * N_HEADS
        cb = D_DN // 2
        for r in range(N_MAIN // cb):
            src = r * cb + (n_gate if r * cb >= COL_SB else 0)
            wm_ref[:, r * cb:(r + 1) * cb] = win_ref[src:src + cb, :].T.astype(BF16)
        wab_ref[...] = win_ref[COL_SB:COL_SB + LANES, :].T.astype(BF16)

    @pl.when(pl.program_id(0) % per_b == 0)
    def _():
        e_s[:, 0:HALO, :] = jnp.zeros((e_s.shape[0], HALO, D_DN), F32)

    _interleave([_in_block(bi, x_ref, mod_ref, nw_ref, wm_ref, wab_ref, cw_ref, scw_ref, gp_ref,
                           proj_ref, gb_ref, e_s, p_s) for bi in range(IN_BLOCKS)], lead=3)
    e_s[:, 0:HALO, :] = e_s[:, tm:tm + HALO, :]


def _in_proj(x2, mod3, nw, w_in, cw, scw, gp, seq):
    t = x2.shape[0]
    tm = TM_IN
    per_b = seq // tm
    const = lambda i: (0, 0)
    return pl.pallas_call(
        functools.partial(_in_kernel, per_b=per_b),
        grid=(t // tm,),
        in_specs=[
            pl.BlockSpec((tm, D_MODEL), lambda i: (i, 0)),
            pl.BlockSpec((None, 6, D_MODEL), lambda i: (i // per_b, 0, 0)),
            pl.BlockSpec((1, D_MODEL), const),
            pl.BlockSpec(w_in.shape, const, pipeline_mode=pl.Buffered(1)),
            pl.BlockSpec((DN_CONV, N_QKV), const),
            pl.BlockSpec((SC_CONV, D_SC), const),
            pl.BlockSpec((8, LANES), const),
        ],
        out_specs=[
            pl.BlockSpec((tm, N_P2), lambda i: (i, 0)),
            pl.BlockSpec((tm, LANES), lambda i: (i, 0)),
        ],
        out_shape=[
            jax.ShapeDtypeStruct((t, N_P2), BF16),
            jax.ShapeDtypeStruct((t, LANES), F32),
        ],
        scratch_shapes=[
            pltpu.VMEM((4, HALO + tm, D_DN), F32),
            pltpu.VMEM((3, tm, D_DN), F32),
            pltpu.VMEM((D_MODEL, N_MAIN), BF16),
            pltpu.VMEM((D_MODEL, LANES), BF16),
        ],
        compiler_params=pltpu.CompilerParams(dimension_semantics=("arbitrary",),
                                             vmem_limit_bytes=VMEM_LIMIT),
        name="in_proj",
    )(x2, mod3, nw, w_in, cw, scw, gp)


def _mix_kernel(proj_ref, gb_ref, x_ref, mod_ref, dnw_ref, wout_ref, pw_ref, out_ref,
                o_s, state_s, aq_s, bb_s, op_s, gam_s, bdm_s, lvl_s, xs_s, mt_s, at_s, gw_s, rv_s):
    tq = x_ref.shape[0]
    t = pl.program_id(1)

    @pl.when(t == 0)
    def _():
        state_s[...] = jnp.zeros_like(state_s)

    q_ref = proj_ref.at[:, 0:D_DN]
    k_ref = proj_ref.at[:, D_DN:2 * D_DN]
    v_ref = proj_ref.at[:, 2 * D_DN:3 * D_DN]

    n_chunks = tq // CHUNK
    quad = N_HEADS * CHUNK
    row_q = lax.broadcasted_iota(jnp.int32, (CHUNK, quad), 0)
    lane_q = lax.broadcasted_iota(jnp.int32, (CHUNK, quad), 1)
    col_q = lane_q & (CHUNK - 1)
    head_q = lane_q >> 6
    tri_incl = row_q >= col_q
    tri_strict = row_q > col_q
    eye_q = row_q == col_q
    eye_f = eye_q.astype(F32)
    row_l = lax.broadcasted_iota(jnp.int32, (CHUNK, LANES), 0)

    def level_mask(r, c, l):
        return (((r >> l) & 1) == 1) & (((c >> l) & 1) == 0) & ((r >> (l + 1)) == (c >> (l + 1)))

    @pl.when((pl.program_id(0) == 0) & (t == 0))
    def _():
        r2 = lax.broadcasted_iota(jnp.int32, (quad, quad), 0)
        c2 = lax.broadcasted_iota(jnp.int32, (quad, quad), 1)
        same_head = (r2 >> 6) == (c2 >> 6)
        bdm_s[...] = same_head.astype(F32).astype(BF16)
        for l in range(1, 6):
            lm = same_head & level_mask(r2 & (CHUNK - 1), c2 & (CHUNK - 1), l)
            lvl_s[l - 1] = lm.astype(F32).astype(BF16)

    lm0_f = level_mask(row_q, col_q, 0).astype(F32)
    zero_blk = jnp.zeros((CHUNK, HEAD_DIM), BF16)
    zero_sq = jnp.zeros((HEAD_DIM, HEAD_DIM), BF16)

    def head_bcast(cols, width):
        if width == HEAD_DIM:
            return jnp.concatenate([jnp.broadcast_to(c_, (CHUNK, HEAD_DIM)) for c_ in cols], axis=1)
        full = [jnp.broadcast_to(c_, (CHUNK, quad)) for c_ in cols]
        return jnp.where(head_q == 0, full[0],
                         jnp.where(head_q == 1, full[1], jnp.where(head_q == 2, full[2], full[3])))

    def bd_wide(a):
        rows = []
        for h in range(N_HEADS):
            rows.append(jnp.concatenate(
                [a[:, j * HEAD_DIM:(j + 1) * HEAD_DIM] if j == h else zero_blk for j in range(N_HEADS)],
                axis=1))
        return jnp.concatenate(rows, axis=0)

    def tile4(a):
        return jnp.concatenate([a] * N_HEADS, axis=0)

    carried = {"state": None}

    def group(gi):
        chunks = range(gi * GROUP_CHUNKS, (gi + 1) * GROUP_CHUNKS)
        rows = {c: slice(c * CHUNK, (c + 1) * CHUNK) for c in chunks}

        for n, c in enumerate(chunks):
            bt = gb_ref[rows[c], :]
            gcum = bt
            for s in (1, 2, 4, 8, 16, 32):
                gcum = gcum + jnp.where(row_l >= s, pltpu.roll(gcum, s, axis=0), 0.0)
            g_cols = [gcum[:, h:h + 1] for h in range(N_HEADS)]
            b_cols = [bt[:, N_HEADS + h:N_HEADS + h + 1] for h in range(N_HEADS)]
            g_i = head_bcast(g_cols, CHUNK)
            b_i = head_bcast(b_cols, CHUNK)
            gw_s[c] = head_bcast(g_cols, HEAD_DIM)
            g_j = jnp.sum(jnp.where(eye_q, g_i, 0.0), axis=0, keepdims=True)
            b_j = jnp.sum(jnp.where(eye_q, b_i, 0.0), axis=0, keepdims=True)
            rv_s[c, 0:1, :] = b_j
            rv_s[c, 1:2, :] = b_j * jnp.exp(g_j)
            decay = jnp.exp(jnp.where(tri_incl, g_i - g_j, -1e30))

            k16 = k_ref[rows[c], :]
            kq = jnp.concatenate([k16, q_ref[rows[c], :]], axis=0)
            kq_k = lax.dot_general(kq, bd_wide(k16), (((1,), (1,)), ((), ())),
                                   preferred_element_type=F32)
            m_st = jnp.where(tri_strict, kq_k[:CHUNK] * b_i * decay, 0.0)
            at_s[c] = (kq_k[CHUNK:] * decay).astype(BF16)
            xs_s[c] = eye_f - m_st * lm0_f
            mt_s[c] = m_st.astype(BF16)
            if n % 2 == 1:
                yield

        for l in range(1, 6):
            for c in chunks:
                x = xs_s[c]
                x16 = x.astype(BF16)
                y = jnp.dot(x16, tile4(mt_s[c]) * lvl_s[l - 1], preferred_element_type=F32)
                z = jnp.dot(y.astype(BF16), tile4(x16) * bdm_s[...], preferred_element_type=F32)
                xs_s[c] = x - z
            yield

        for n, c in enumerate(chunks):
            k16 = k_ref[rows[c], :]
            q = q_ref[rows[c], :].astype(F32)
            k = k16.astype(F32)
            x = xs_s[c]
            u = jnp.dot((x * rv_s[c, 0:1, :]).astype(BF16), bd_wide(v_ref[rows[c], :]),
                        preferred_element_type=F32)
            w = jnp.dot((x * rv_s[c, 1:2, :]).astype(BF16), bd_wide(k16), preferred_element_type=F32)
            u16 = u.astype(BF16)
            w16 = w.astype(BF16)
            a_wu = jnp.dot(at_s[c], jnp.concatenate([bd_wide(w16), bd_wide(u16)], axis=1),
                           preferred_element_type=F32)
            g_wide = gw_s[c]
            g_last = g_wide[CHUNK - 1:CHUNK, :]
            aq_s[c, 2 * CHUNK:3 * CHUNK, :] = (q * jnp.exp(g_wide) - a_wu[:, :D_DN]).astype(BF16)
            op_s[c] = a_wu[:, D_DN:]
            gam_s[c] = jnp.broadcast_to(jnp.exp(g_last), (8, D_DN))
            k_dec = (k * jnp.exp(g_last - g_wide)).astype(BF16)
            for h in range(N_HEADS):
                hs = slice(h * HEAD_DIM, (h + 1) * HEAD_DIM)
                ab_h = _mm_tn(k_dec[:, hs], jnp.concatenate([w16[:, hs], u16[:, hs]], axis=1))
                aq_s[c, 0:2 * CHUNK, hs] = ab_h[:, :HEAD_DIM].astype(BF16)
                bb_s[c, :, hs] = ab_h[:, HEAD_DIM:]
            if n % 2 == 1:
                yield

        state = state_s[...] if gi == 0 else carried["state"]
        for c in chunks:
            s16 = state.astype(BF16)
            new_parts = []
            o_parts_c = []
            for p in range(N_HEADS // 2):
                ps = slice(p * 2 * HEAD_DIM, (p + 1) * 2 * HEAD_DIM)
                s_a = s16[:, (2 * p) * HEAD_DIM:(2 * p + 1) * HEAD_DIM]
                s_b = s16[:, (2 * p + 1) * HEAD_DIM:(2 * p + 2) * HEAD_DIM]
                bd = jnp.concatenate([jnp.concatenate([s_a, zero_sq], axis=1),
                                      jnp.concatenate([zero_sq, s_b], axis=1)], axis=0)
                r = jnp.dot(aq_s[c, :, ps], bd, preferred_element_type=F32)
                new_parts.append(gam_s[c, 0:1, ps] * state[:, ps] + bb_s[c, :, ps] - r[:2 * CHUNK])
                o_parts_c.append(op_s[c, :, ps] + r[2 * CHUNK:])
            state = jnp.concatenate(new_parts, axis=1)
            o_s[rows[c], :] = jnp.concatenate(o_parts_c, axis=1)
        carried["state"] = state
        if gi == n_chunks // GROUP_CHUNKS - 1:
            state_s[...] = state
        yield

        rg = slice(gi * GROUP_CHUNKS * CHUNK, (gi + 1) * GROUP_CHUNKS * CHUNK)
        zg = proj_ref[rg, COL_ZG:COL_ZG + D_DN].astype(F32)
        o_parts = []
        for h in range(N_HEADS):
            hs = slice(h * HEAD_DIM, (h + 1) * HEAD_DIM)
            oh = o_s[rg, hs]
            o_parts.append(oh * _rms_scale(oh) * dnw_ref[...] * zg[:, hs])
        o_dn = jnp.concatenate(o_parts, axis=1)
        y = (jnp.dot(o_dn.astype(BF16), wout_ref[0:D_DN, :], preferred_element_type=F32)
             + jnp.dot(proj_ref[rg, COL_YSC:COL_YSC + D_SC], wout_ref[D_DN:D_MODEL, :],
                       preferred_element_type=F32))
        out_ref[rg, :] = x_ref[rg, :] + mod_ref[2:3, :] * (y * _rms_scale(y) * pw_ref[...])

    _interleave([group(gi) for gi in range(n_chunks // GROUP_CHUNKS)], lead=2)


def _mixer(proj, gb, x2, mod3, dnw, w_out, pw, bsz, seq):
    tq = TQ_MIX
    per_b = seq // tq
    t = x2.shape[0]

    def tile(b, s):
        return (b * per_b + s, 0)

    const = lambda b, s: (0, 0)
    return pl.pallas_call(
        _mix_kernel,
        grid=(bsz, per_b),
        in_specs=[
            pl.BlockSpec((tq, N_P2), tile),
            pl.BlockSpec((tq, LANES), tile),
            pl.BlockSpec((tq, D_MODEL), tile),
            pl.BlockSpec((None, 6, D_MODEL), lambda b, s: (b, 0, 0)),
            pl.BlockSpec((1, HEAD_DIM), const),
            pl.BlockSpec((D_MODEL, D_MODEL), const),
            pl.BlockSpec((1, D_MODEL), const),
        ],
        out_specs=pl.BlockSpec((tq, D_MODEL), tile),
        out_shape=jax.ShapeDtypeStruct((t, D_MODEL), F32),
        scratch_shapes=[
            pltpu.VMEM((tq, D_DN), F32),
            pltpu.VMEM((HEAD_DIM, D_DN), F32),
            pltpu.VMEM((tq // CHUNK, 3 * CHUNK, D_DN), BF16),
            pltpu.VMEM((tq // CHUNK, HEAD_DIM, D_DN), F32),
            pltpu.VMEM((tq // CHUNK, CHUNK, D_DN), F32),
            pltpu.VMEM((tq // CHUNK, 8, D_DN), F32),
            pltpu.VMEM((N_HEADS * CHUNK, N_HEADS * CHUNK), BF16),
            pltpu.VMEM((5, N_HEADS * CHUNK, N_HEADS * CHUNK), BF16),
            pltpu.VMEM((tq // CHUNK, CHUNK, N_HEADS * CHUNK), F32),
            pltpu.VMEM((tq // CHUNK, CHUNK, N_HEADS * CHUNK), BF16),
            pltpu.VMEM((tq // CHUNK, CHUNK, N_HEADS * CHUNK), BF16),
            pltpu.VMEM((tq // CHUNK, CHUNK, D_DN), F32),
            pltpu.VMEM((tq // CHUNK, 8, N_HEADS * CHUNK), F32),
        ],
        compiler_params=pltpu.CompilerParams(dimension_semantics=("arbitrary", "arbitrary"),
                                             vmem_limit_bytes=VMEM_LIMIT),
        name="token_mixer",
    )(proj, gb, x2, mod3, dnw, w_out, pw)


def _ffn_kernel(x_ref, mod_ref, nw_ref, w1_hbm, w2_hbm, pw_ref, out_ref, w1_ref, w2_ref, stage_s, sem):
    @pl.when(pl.program_id(0) == 0)
    def _():
        pieces = []
        for j in range(D_FF // FF_CHUNK):
            cs = slice(j * FF_CHUNK, (j + 1) * FF_CHUNK)
            pieces.append((w1_hbm.at[:, cs], w1_ref.at[:, cs]))
        for j in range(D_FF // FF_CHUNK):
            cs = slice(j * FF_CHUNK, (j + 1) * FF_CHUNK)
            pieces.append((w2_hbm.at[cs, :], w2_ref.at[cs, :]))

        def copy(n):
            return pltpu.make_async_copy(pieces[n][0], stage_s.at[n % 2], sem.at[n % 2])

        copy(0).start()
        for n in range(len(pieces)):
            if n + 1 < len(pieces):
                copy(n + 1).start()
            copy(n).wait()
            pieces[n][1][...] = stage_s[n % 2].astype(BF16)

    x = x_ref[...]
    h = x * _rms_scale(x) * nw_ref[...]
    h = h * (1.0 + mod_ref[4:5, :]) + mod_ref[3:4, :]
    hb = h.astype(BF16)
    hidden = []
    for j in range(D_FF // FF_CHUNK):
        a = jnp.dot(hb, w1_ref[:, j * FF_CHUNK:(j + 1) * FF_CHUNK], preferred_element_type=F32)
        hidden.append(jnp.square(jnp.maximum(a, 0.0)).astype(BF16))
    y = jnp.dot(jnp.concatenate(hidden, axis=1), w2_ref[...], preferred_element_type=F32)
    out_ref[...] = x + mod_ref[5:6, :] * (y * _rms_scale(y) * pw_ref[...])


def _ffn(x2, mod3, nw, w1, w2, pw, seq):
    t = x2.shape[0]
    tm = TM_FFN
    per_b = seq // tm
    return pl.pallas_call(
        _ffn_kernel,
        grid=(t // tm,),
        in_specs=[
            pl.BlockSpec((tm, D_MODEL), lambda i: (i, 0)),
            pl.BlockSpec((None, 6, D_MODEL), lambda i: (i // per_b, 0, 0)),
            pl.BlockSpec((1, D_MODEL), lambda i: (0, 0)),
            pl.BlockSpec(memory_space=pl.ANY),
            pl.BlockSpec(memory_space=pl.ANY),
            pl.BlockSpec((1, D_MODEL), lambda i: (0, 0)),
        ],
        out_specs=pl.BlockSpec((tm, D_MODEL), lambda i: (i, 0)),
        out_shape=jax.ShapeDtypeStruct((t, D_MODEL), F32),
        scratch_shapes=[
            pltpu.VMEM((D_MODEL, D_FF), BF16),
            pltpu.VMEM((D_FF, D_MODEL), BF16),
            pltpu.VMEM((2, FF_CHUNK, FF_CHUNK), F32),
            pltpu.SemaphoreType.DMA((2,)),
        ],
        compiler_params=pltpu.CompilerParams(dimension_semantics=("arbitrary",),
                                             vmem_limit_bytes=VMEM_LIMIT),
        name="ffn",
    )(x2, mod3, nw, w1, w2, pw)


def _layer(x2, c, w_ada, b_ada, pre_mix_w, post_mix_w, w_in, dn_conv_w, dn_a_log, dn_dt_bias,
           dn_norm_w, sc_conv_w, w_out, pre_ffn_w, post_ffn_w, w_ff1, w_ff2, bsz, seq):
    mod3 = _ada(c, w_ada, b_ada[None, :]).reshape(bsz, 6, D_MODEL)

    gp = jnp.zeros((8, LANES), F32)
    gp = gp.at[0, :N_HEADS].set(dn_a_log).at[1, :N_HEADS].set(dn_dt_bias)
    proj, gb = _in_proj(x2, mod3, pre_mix_w[None, :], w_in.T, dn_conv_w, sc_conv_w, gp, seq)

    x2 = _mixer(proj, gb, x2, mod3, dn_norm_w[None, :], w_out.astype(BF16), post_mix_w[None, :],
                bsz, seq)

    return _ffn(x2, mod3, pre_ffn_w[None, :], w_ff1, w_ff2, post_ffn_w[None, :], seq)


def kernel(x, c, w_ada, b_ada, pre_mix_norm_w, post_mix_norm_w, w_in, dn_conv_w, dn_a_log, dn_dt_bias, dn_norm_w, sc_conv_w, w_out, pre_ffn_norm_w, post_ffn_norm_w, w_ff1, w_ff2):
    bsz, seq, d = x.shape
    x2 = x.reshape(bsz * seq, d)
    for l in range(w_ada.shape[0]):
        x2 = _layer(x2, c, w_ada[l], b_ada[l], pre_mix_norm_w[l], post_mix_norm_w[l], w_in[l],
                    dn_conv_w[l], dn_a_log[l], dn_dt_bias[l], dn_norm_w[l], sc_conv_w[l], w_out[l],
                    pre_ffn_norm_w[l], post_ffn_norm_w[l], w_ff1[l], w_ff2[l], bsz, seq)
    return x2.reshape(bsz, seq, d)
```

```python
import functools

import jax
import jax.numpy as jnp
from jax import lax
from jax.experimental import pallas as pl
from jax.experimental.pallas import tpu as pltpu

F32 = jnp.float32
BF16 = jnp.bfloat16

D_MODEL = 1024
N_HEADS = 4
HEAD_DIM = 128
D_DN = N_HEADS * HEAD_DIM
D_SC = D_MODEL - D_DN
DN_CONV = 4
SC_CONV = 3
CHUNK = 64
D_FF = 4 * D_MODEL
EPS = 1e-6
N_QKV = 3 * D_DN
N_MAIN = 4 * D_DN + 3 * D_SC
COL_Z = N_QKV
COL_SB = 4 * D_DN
COL_SC = COL_SB + D_SC
COL_SH = COL_SC + D_SC
N_P2 = N_QKV + D_DN + D_SC
COL_ZG = N_QKV
COL_YSC = N_QKV + D_DN
LANES = 128
HALO = 8

TM_IN = 512
IN_BLOCKS = 4
TQ_MIX = 1024
GROUP_CHUNKS = 4
TM_FFN = 512
FF_CHUNK = 1024
VMEM_LIMIT = 56 * 1024 * 1024


def _sigmoid(x):
    return 0.5 + 0.5 * jnp.tanh(0.5 * x)


def _mm_tn(a, b):
    return lax.dot_general(a, b, (((0,), (0,)), ((), ())), preferred_element_type=F32)


def _rms_scale(x):
    return lax.rsqrt(jnp.mean(x * x, axis=-1, keepdims=True) + EPS)


def _interleave(gens, lead):
    live = []
    pending = list(gens)
    rnd = 0
    while live or pending:
        if pending and rnd % lead == 0:
            live.append(pending.pop(0))
        for g in list(live):
            try:
                next(g)
            except StopIteration:
                live.remove(g)
        rnd += 1


def _ada_kernel(c_ref, w_ref, b_ref, o_ref):
    c = c_ref[...]
    ca = (c * _sigmoid(c)).astype(BF16)
    o_ref[...] = jnp.dot(ca, w_ref[...].astype(BF16), preferred_element_type=F32) + b_ref[...]


def _ada(c, w_ada, b_ada):
    bsz = c.shape[0]
    n = w_ada.shape[1]
    bn = 1024
    return pl.pallas_call(
        _ada_kernel,
        grid=(n // bn,),
        in_specs=[
            pl.BlockSpec((bsz, D_MODEL), lambda j: (0, 0)),
            pl.BlockSpec((D_MODEL, bn), lambda j: (0, j)),
            pl.BlockSpec((1, bn), lambda j: (0, j)),
        ],
        out_specs=pl.BlockSpec((bsz, bn), lambda j: (0, j)),
        out_shape=jax.ShapeDtypeStruct((bsz, n), F32),
        compiler_params=pltpu.CompilerParams(dimension_semantics=("arbitrary",),
                                             vmem_limit_bytes=VMEM_LIMIT),
        name="ada_mod",
    )(c, w_ada, b_ada)


def _causal_conv(ext, w, c0):
    width = w.shape[0]
    r = ext.shape[0] - HALO
    n = ext.shape[1]
    e3 = ext.reshape(r // HALO + 1, HALO, n)
    sub = lax.broadcasted_iota(jnp.int32, (1, HALO, n), 1)

    def delayed(a3, s):
        prev3 = jnp.concatenate([a3[-1:], a3[:-1]], axis=0)
        return pltpu.roll(jnp.where(sub >= HALO - s, prev3, a3), s, axis=1)

    def tap(j):
        return w[j:j + 1, c0:c0 + n]

    assert width in (3, 4), width
    x1 = delayed(e3, 1)
    if width == 4:
        out = tap(3) * e3 + tap(2) * x1 + delayed(tap(1) * e3 + tap(0) * x1, 2)
    else:
        out = tap(2) * e3 + tap(1) * x1 + delayed(tap(0) * e3, 2)
    return out[1:].reshape(r, n)


def _silu_from_half(hx):
    return hx + hx * jnp.tanh(hx)


def _in_block(bi, x_ref, mod_ref, nw_ref, wm_ref, wab_ref, cw_ref, scw_ref, gp_ref,
              proj_ref, gb_ref, e_s, p_s):
    nb = x_ref.shape[0] // IN_BLOCKS
    r0 = bi * nb
    rows = slice(r0, r0 + nb)
    erows = slice(HALO + r0, HALO + r0 + nb)
    wrows = slice(r0, r0 + nb + HALO)

    x = x_ref[rows, :]
    h = x * _rms_scale(x) * (nw_ref[...] * (1.0 + mod_ref[1:2, :])) + mod_ref[0:1, :]
    hb = h.astype(BF16)
    half_cw = 0.5 * cw_ref[...]
    yield

    def proj(c0, n=D_DN):
        return jnp.dot(hb, wm_ref[:, c0:c0 + n], preferred_element_type=F32)

    for part in range(3):
        e_s[part, erows, :] = proj(part * D_DN)
        yield
    e_s[3, erows, :] = proj(COL_SC)
    p_s[0, rows, :] = proj(COL_Z)
    yield
    p_s[1, rows, :] = proj(COL_SH)
    p_s[2, rows, :] = proj(COL_SB)
    ab = jnp.dot(hb, wab_ref[...], preferred_element_type=F32)
    yield

    for part in range(3):
        c0 = part * D_DN
        y = _silu_from_half(_causal_conv(e_s[part, wrows, :], half_cw, c0))
        if part < 2:
            parts = []
            for hd in range(N_HEADS):
                yh = y[:, hd * HEAD_DIM:(hd + 1) * HEAD_DIM]
                nrm = lax.rsqrt(jnp.sum(yh * yh, axis=-1, keepdims=True) + EPS)
                if part == 0:
                    nrm = nrm * (HEAD_DIM ** -0.5)
                parts.append(yh * nrm)
            y = jnp.concatenate(parts, axis=1)
        proj_ref[rows, c0:c0 + D_DN] = y.astype(BF16)
        yield

    proj_ref[rows, COL_ZG:COL_ZG + D_DN] = _silu_from_half(0.5 * p_s[0, rows, :]).astype(BF16)
    e_s[3, erows, :] = e_s[3, erows, :] * p_s[1, rows, :]
    conv = _causal_conv(e_s[3, wrows, :], scw_ref[...], 0)
    proj_ref[rows, COL_YSC:COL_YSC + D_SC] = (p_s[2, rows, :] * conv).astype(BF16)

    xg = ab + gp_ref[1:2, :]
    softplus = jnp.maximum(xg, 0.0) + jnp.log1p(jnp.exp(-jnp.abs(xg)))
    lane = lax.broadcasted_iota(jnp.int32, ab.shape, 1)
    gb_ref[rows, :] = jnp.where(lane < N_HEADS, -jnp.exp(gp_ref[0:1, :]) * softplus, _sigmoid(ab))


def _in_kernel(x_ref, mod_ref, nw_ref, win_ref, cw_ref, scw_ref, gp_ref,
               proj_ref, gb_ref, e_s, p_s, wm_ref, wab_ref, *, per_b):
    tm = x_ref.shape[0]

    @pl.when(pl.program_id(0) == 0)
    def _():
        n_gate = 2 * N_HEADS
        cb = D_DN // 2
        for r in range(N_MAIN // cb):
            src = r * cb + (n_gate if r * cb >= COL_SB else 0)
            wm_ref[:, r * cb:(r + 1) * cb] = win_ref[src:src + cb, :].T.astype(BF16)
        wab_ref[...] = win_ref[COL_SB:COL_SB + LANES, :].T.astype(BF16)

    @pl.when(pl.program_id(0) % per_b == 0)
    def _():
        e_s[:, 0:HALO, :] = jnp.zeros((e_s.shape[0], HALO, D_DN), F32)

    _interleave([_in_block(bi, x_ref, mod_ref, nw_ref, wm_ref, wab_ref, cw_ref, scw_ref, gp_ref,
                           proj_ref, gb_ref, e_s, p_s) for bi in range(IN_BLOCKS)], lead=3)
    e_s[:, 0:HALO, :] = e_s[:, tm:tm + HALO, :]


def _in_proj(x2, mod3, nw, w_in, cw, scw, gp, seq):
    t = x2.shape[0]
    tm = TM_IN
    per_b = seq // tm
    const = lambda i: (0, 0)
    return pl.pallas_call(
        functools.partial(_in_kernel, per_b=per_b),
        grid=(t // tm,),
        in_specs=[
            pl.BlockSpec((tm, D_MODEL), lambda i: (i, 0)),
            pl.BlockSpec((None, 6, D_MODEL), lambda i: (i // per_b, 0, 0)),
            pl.BlockSpec((1, D_MODEL), const),
            pl.BlockSpec(w_in.shape, const, pipeline_mode=pl.Buffered(1)),
            pl.BlockSpec((DN_CONV, N_QKV), const),
            pl.BlockSpec((SC_CONV, D_SC), const),
            pl.BlockSpec((8, LANES), const),
        ],
        out_specs=[
            pl.BlockSpec((tm, N_P2), lambda i: (i, 0)),
            pl.BlockSpec((tm, LANES), lambda i: (i, 0)),
        ],
        out_shape=[
            jax.ShapeDtypeStruct((t, N_P2), BF16),
            jax.ShapeDtypeStruct((t, LANES), F32),
        ],
        scratch_shapes=[
            pltpu.VMEM((4, HALO + tm, D_DN), F32),
            pltpu.VMEM((3, tm, D_DN), F32),
            pltpu.VMEM((D_MODEL, N_MAIN), BF16),
            pltpu.VMEM((D_MODEL, LANES), BF16),
        ],
        compiler_params=pltpu.CompilerParams(dimension_semantics=("arbitrary",),
                                             vmem_limit_bytes=VMEM_LIMIT),
        name="in_proj",
    )(x2, mod3, nw, w_in, cw, scw, gp)


def _mix_kernel(proj_ref, gb_ref, x_ref, mod_ref, dnw_ref, wout_ref, pw_ref, out_ref,
                o_s, state_s, aq_s, bb_s, op_s, gam_s, bdm_s, lvl_s, xs_s, mt_s, at_s, gw_s, rv_s):
    tq = x_ref.shape[0]
    t = pl.program_id(1)

    @pl.when(t == 0)
    def _():
        state_s[...] = jnp.zeros_like(state_s)

    q_ref = proj_ref.at[:, 0:D_DN]
    k_ref = proj_ref.at[:, D_DN:2 * D_DN]
    v_ref = proj_ref.at[:, 2 * D_DN:3 * D_DN]

    n_chunks = tq // CHUNK
    quad = N_HEADS * CHUNK
    row_q = lax.broadcasted_iota(jnp.int32, (CHUNK, quad), 0)
    lane_q = lax.broadcasted_iota(jnp.int32, (CHUNK, quad), 1)
    col_q = lane_q & (CHUNK - 1)
    head_q = lane_q >> 6
    tri_incl = row_q >= col_q
    tri_strict = row_q > col_q
    eye_q = row_q == col_q
    eye_f = eye_q.astype(F32)
    row_l = lax.broadcasted_iota(jnp.int32, (CHUNK, LANES), 0)

    def level_mask(r, c, l):
        return (((r >> l) & 1) == 1) & (((c >> l) & 1) == 0) & ((r >> (l + 1)) == (c >> (l + 1)))

    @pl.when((pl.program_id(0) == 0) & (t == 0))
    def _():
        r2 = lax.broadcasted_iota(jnp.int32, (quad, quad), 0)
        c2 = lax.broadcasted_iota(jnp.int32, (quad, quad), 1)
        same_head = (r2 >> 6) == (c2 >> 6)
        bdm_s[...] = same_head.astype(F32).astype(BF16)
        for l in range(1, 6):
            lm = same_head & level_mask(r2 & (CHUNK - 1), c2 & (CHUNK - 1), l)
            lvl_s[l - 1] = lm.astype(F32).astype(BF16)

    lm0_f = level_mask(row_q, col_q, 0).astype(F32)
    zero_blk = jnp.zeros((CHUNK, HEAD_DIM), BF16)
    zero_sq = jnp.zeros((HEAD_DIM, HEAD_DIM), BF16)

    def head_bcast(cols, width):
        if width == HEAD_DIM:
            return jnp.concatenate([jnp.broadcast_to(c_, (CHUNK, HEAD_DIM)) for c_ in cols], axis=1)
        full = [jnp.broadcast_to(c_, (CHUNK, quad)) for c_ in cols]
        return jnp.where(head_q == 0, full[0],
                         jnp.where(head_q == 1, full[1], jnp.where(head_q == 2, full[2], full[3])))

    def bd_wide(a):
        rows = []
        for h in range(N_HEADS):
            rows.append(jnp.concatenate(
                [a[:, j * HEAD_DIM:(j + 1) * HEAD_DIM] if j == h else zero_blk for j in range(N_HEADS)],
                axis=1))
        return jnp.concatenate(rows, axis=0)

    def tile4(a):
        return jnp.concatenate([a] * N_HEADS, axis=0)

    carried = {"state": None}

    def group(gi):
        chunks = range(gi * GROUP_CHUNKS, (gi + 1) * GROUP_CHUNKS)
        rows = {c: slice(c * CHUNK, (c + 1) * CHUNK) for c in chunks}

        for n, c in enumerate(chunks):
            bt = gb_ref[rows[c], :]
            gcum = bt
            for s in (1, 2, 4, 8, 16, 32):
                gcum = gcum + jnp.where(row_l >= s, pltpu.roll(gcum, s, axis=0), 0.0)
            g_cols = [gcum[:, h:h + 1] for h in range(N_HEADS)]
            b_cols = [bt[:, N_HEADS + h:N_HEADS + h + 1] for h in range(N_HEADS)]
            g_i = head_bcast(g_cols, CHUNK)
            b_i = head_bcast(b_cols, CHUNK)
            gw_s[c] = head_bcast(g_cols, HEAD_DIM)
            g_j = jnp.sum(jnp.where(eye_q, g_i, 0.0), axis=0, keepdims=True)
            b_j = jnp.sum(jnp.where(eye_q, b_i, 0.0), axis=0, keepdims=True)
            rv_s[c, 0:1, :] = b_j
            rv_s[c, 1:2, :] = b_j * jnp.exp(g_j)
            decay = jnp.exp(jnp.where(tri_incl, g_i - g_j, -1e30))

            k16 = k_ref[rows[c], :]
            kq = jnp.concatenate([k16, q_ref[rows[c], :]], axis=0)
            kq_k = lax.dot_general(kq, bd_wide(k16), (((1,), (1,)), ((), ())),
                                   preferred_element_type=F32)
            m_st = jnp.where(tri_strict, kq_k[:CHUNK] * b_i * decay, 0.0)
            at_s[c] = (kq_k[CHUNK:] * decay).astype(BF16)
            xs_s[c] = eye_f - m_st * lm0_f
            mt_s[c] = m_st.astype(BF16)
            if n % 2 == 1:
                yield

        for l in range(1, 6):
            for c in chunks:
                x = xs_s[c]
                x16 = x.astype(BF16)
                y = jnp.dot(x16, tile4(mt_s[c]) * lvl_s[l - 1], preferred_element_type=F32)
                z = jnp.dot(y.astype(BF16), tile4(x16) * bdm_s[...], preferred_element_type=F32)
                xs_s[c] = x - z
            yield

        for n, c in enumerate(chunks):
            k16 = k_ref[rows[c], :]
            q = q_ref[rows[c], :].astype(F32)
            k = k16.astype(F32)
            x = xs_s[c]
            u = jnp.dot((x * rv_s[c, 0:1, :]).astype(BF16), bd_wide(v_ref[rows[c], :]),
                        preferred_element_type=F32)
            w = jnp.dot((x * rv_s[c, 1:2, :]).astype(BF16), bd_wide(k16), preferred_element_type=F32)
            u16 = u.astype(BF16)
            w16 = w.astype(BF16)
            a_wu = jnp.dot(at_s[c], jnp.concatenate([bd_wide(w16), bd_wide(u16)], axis=1),
                           preferred_element_type=F32)
            g_wide = gw_s[c]
            g_last = g_wide[CHUNK - 1:CHUNK, :]
            aq_s[c, 2 * CHUNK:3 * CHUNK, :] = (q * jnp.exp(g_wide) - a_wu[:, :D_DN]).astype(BF16)
            op_s[c] = a_wu[:, D_DN:]
            gam_s[c] = jnp.broadcast_to(jnp.exp(g_last), (8, D_DN))
            k_dec = (k * jnp.exp(g_last - g_wide)).astype(BF16)
            for h in range(N_HEADS):
                hs = slice(h * HEAD_DIM, (h + 1) * HEAD_DIM)
                ab_h = _mm_tn(k_dec[:, hs], jnp.concatenate([w16[:, hs], u16[:, hs]], axis=1))
                aq_s[c, 0:2 * CHUNK, hs] = ab_h[:, :HEAD_DIM].astype(BF16)
                bb_s[c, :, hs] = ab_h[:, HEAD_DIM:]
            if n % 2 == 1:
                yield

        state = state_s[...] if gi == 0 else carried["state"]
        for c in chunks:
            s16 = state.astype(BF16)
            new_parts = []
            o_parts_c = []
            for p in range(N_HEADS // 2):
                ps = slice(p * 2 * HEAD_DIM, (p + 1) * 2 * HEAD_DIM)
                s_a = s16[:, (2 * p) * HEAD_DIM:(2 * p + 1) * HEAD_DIM]
                s_b = s16[:, (2 * p + 1) * HEAD_DIM:(2 * p + 2) * HEAD_DIM]
                bd = jnp.concatenate([jnp.concatenate([s_a, zero_sq], axis=1),
                                      jnp.concatenate([zero_sq, s_b], axis=1)], axis=0)
                r = jnp.dot(aq_s[c, :, ps], bd, preferred_element_type=F32)
                new_parts.append(gam_s[c, 0:1, ps] * state[:, ps] + bb_s[c, :, ps] - r[:2 * CHUNK])
                o_parts_c.append(op_s[c, :, ps] + r[2 * CHUNK:])
            state = jnp.concatenate(new_parts, axis=1)
            o_s[rows[c], :] = jnp.concatenate(o_parts_c, axis=1)
        carried["state"] = state
        if gi == n_chunks // GROUP_CHUNKS - 1:
            state_s[...] = state
        yield

        rg = slice(gi * GROUP_CHUNKS * CHUNK, (gi + 1) * GROUP_CHUNKS * CHUNK)
        zg = proj_ref[rg, COL_ZG:COL_ZG + D_DN].astype(F32)
        o_parts = []
        for h in range(N_HEADS):
            hs = slice(h * HEAD_DIM, (h + 1) * HEAD_DIM)
            oh = o_s[rg, hs]
            o_parts.append(oh * _rms_scale(oh) * dnw_ref[...] * zg[:, hs])
        mixed = jnp.concatenate([p.astype(BF16) for p in o_parts]
                                + [proj_ref[rg, COL_YSC:COL_YSC + D_SC]], axis=1)
        y = jnp.dot(mixed, wout_ref[...], preferred_element_type=F32)
        out_ref[rg, :] = x_ref[rg, :] + mod_ref[2:3, :] * (y * _rms_scale(y) * pw_ref[...])

    _interleave([group(gi) for gi in range(n_chunks // GROUP_CHUNKS)], lead=2)


def _mixer(proj, gb, x2, mod3, dnw, w_out, pw, bsz, seq):
    tq = TQ_MIX
    per_b = seq // tq
    t = x2.shape[0]

    def tile(b, s):
        return (b * per_b + s, 0)

    const = lambda b, s: (0, 0)
    return pl.pallas_call(
        _mix_kernel,
        grid=(bsz, per_b),
        in_specs=[
            pl.BlockSpec((tq, N_P2), tile),
            pl.BlockSpec((tq, LANES), tile),
            pl.BlockSpec((tq, D_MODEL), tile),
            pl.BlockSpec((None, 6, D_MODEL), lambda b, s: (b, 0, 0)),
            pl.BlockSpec((1, HEAD_DIM), const),
            pl.BlockSpec((D_MODEL, D_MODEL), const),
            pl.BlockSpec((1, D_MODEL), const),
        ],
        out_specs=pl.BlockSpec((tq, D_MODEL), tile),
        out_shape=jax.ShapeDtypeStruct((t, D_MODEL), F32),
        scratch_shapes=[
            pltpu.VMEM((tq, D_DN), F32),
            pltpu.VMEM((HEAD_DIM, D_DN), F32),
            pltpu.VMEM((tq // CHUNK, 3 * CHUNK, D_DN), BF16),
            pltpu.VMEM((tq // CHUNK, HEAD_DIM, D_DN), F32),
            pltpu.VMEM((tq // CHUNK, CHUNK, D_DN), F32),
            pltpu.VMEM((tq // CHUNK, 8, D_DN), F32),
            pltpu.VMEM((N_HEADS * CHUNK, N_HEADS * CHUNK), BF16),
            pltpu.VMEM((5, N_HEADS * CHUNK, N_HEADS * CHUNK), BF16),
            pltpu.VMEM((tq // CHUNK, CHUNK, N_HEADS * CHUNK), F32),
            pltpu.VMEM((tq // CHUNK, CHUNK, N_HEADS * CHUNK), BF16),
            pltpu.VMEM((tq // CHUNK, CHUNK, N_HEADS * CHUNK), BF16),
            pltpu.VMEM((tq // CHUNK, CHUNK, D_DN), F32),
            pltpu.VMEM((tq // CHUNK, 8, N_HEADS * CHUNK), F32),
        ],
        compiler_params=pltpu.CompilerParams(dimension_semantics=("arbitrary", "arbitrary"),
                                             vmem_limit_bytes=VMEM_LIMIT),
        name="token_mixer",
    )(proj, gb, x2, mod3, dnw, w_out, pw)


def _ffn_kernel(x_ref, mod_ref, nw_ref, w1_hbm, w2_hbm, pw_ref, out_ref, w1_ref, w2_ref, stage_s, sem):
    @pl.when(pl.program_id(0) == 0)
    def _():
        pieces = []
        for j in range(D_FF // FF_CHUNK):
            cs = slice(j * FF_CHUNK, (j + 1) * FF_CHUNK)
            pieces.append((w1_hbm.at[:, cs], w1_ref.at[:, cs]))
        for j in range(D_FF // FF_CHUNK):
            cs = slice(j * FF_CHUNK, (j + 1) * FF_CHUNK)
            pieces.append((w2_hbm.at[cs, :], w2_ref.at[cs, :]))

        def copy(n):
            return pltpu.make_async_copy(pieces[n][0], stage_s.at[n % 2], sem.at[n % 2])

        copy(0).start()
        for n in range(len(pieces)):
            if n + 1 < len(pieces):
                copy(n + 1).start()
            copy(n).wait()
            pieces[n][1][...] = stage_s[n % 2].astype(BF16)

    x = x_ref[...]
    h = x * _rms_scale(x) * nw_ref[...]
    h = h * (1.0 + mod_ref[4:5, :]) + mod_ref[3:4, :]
    hb = h.astype(BF16)
    hidden = []
    for j in range(D_FF // FF_CHUNK):
        a = jnp.dot(hb, w1_ref[:, j * FF_CHUNK:(j + 1) * FF_CHUNK], preferred_element_type=F32)
        hidden.append(jnp.square(jnp.maximum(a, 0.0)).astype(BF16))
    y = jnp.dot(jnp.concatenate(hidden, axis=1), w2_ref[...], preferred_element_type=F32)
    out_ref[...] = x + mod_ref[5:6, :] * (y * _rms_scale(y) * pw_ref[...])


def _ffn(x2, mod3, nw, w1, w2, pw, seq):
    t = x2.shape[0]
    tm = TM_FFN
    per_b = seq // tm
    return pl.pallas_call(
        _ffn_kernel,
        grid=(t // tm,),
        in_specs=[
            pl.BlockSpec((tm, D_MODEL), lambda i: (i, 0)),
            pl.BlockSpec((None, 6, D_MODEL), lambda i: (i // per_b, 0, 0)),
            pl.BlockSpec((1, D_MODEL), lambda i: (0, 0)),
            pl.BlockSpec(memory_space=pl.ANY),
            pl.BlockSpec(memory_space=pl.ANY),
            pl.BlockSpec((1, D_MODEL), lambda i: (0, 0)),
        ],
        out_specs=pl.BlockSpec((tm, D_MODEL), lambda i: (i, 0)),
        out_shape=jax.ShapeDtypeStruct((t, D_MODEL), F32),
        scratch_shapes=[
            pltpu.VMEM((D_MODEL, D_FF), BF16),
            pltpu.VMEM((D_FF, D_MODEL), BF16),
            pltpu.VMEM((2, FF_CHUNK, FF_CHUNK), F32),
            pltpu.SemaphoreType.DMA((2,)),
        ],
        compiler_params=pltpu.CompilerParams(dimension_semantics=("arbitrary",),
                                             vmem_limit_bytes=VMEM_LIMIT),
        name="ffn",
    )(x2, mod3, nw, w1, w2, pw)


def _layer(x2, c, w_ada, b_ada, pre_mix_w, post_mix_w, w_in, dn_conv_w, dn_a_log, dn_dt_bias,
           dn_norm_w, sc_conv_w, w_out, pre_ffn_w, post_ffn_w, w_ff1, w_ff2, bsz, seq):
    mod3 = _ada(c, w_ada, b_ada[None, :]).reshape(bsz, 6, D_MODEL)

    gp = jnp.zeros((8, LANES), F32)
    gp = gp.at[0, :N_HEADS].set(dn_a_log).at[1, :N_HEADS].set(dn_dt_bias)
    proj, gb = _in_proj(x2, mod3, pre_mix_w[None, :], w_in.T, dn_conv_w, sc_conv_w, gp, seq)

    x2 = _mixer(proj, gb, x2, mod3, dn_norm_w[None, :], w_out.astype(BF16), post_mix_w[None, :],
                bsz, seq)

    return _ffn(x2, mod3, pre_ffn_w[None, :], w_ff1, w_ff2, post_ffn_w[None, :], seq)


def kernel(x, c, w_ada, b_ada, pre_mix_norm_w, post_mix_norm_w, w_in, dn_conv_w, dn_a_log, dn_dt_bias, dn_norm_w, sc_conv_w, w_out, pre_ffn_norm_w, post_ffn_norm_w, w_ff1, w_ff2):
    bsz, seq, d = x.shape
    x2 = x.reshape(bsz * seq, d)
    for l in range(w_ada.shape[0]):
        x2 = _layer(x2, c, w_ada[l], b_ada[l], pre_mix_norm_w[l], post_mix_norm_w[l], w_in[l],
                    dn_conv_w[l], dn_a_log[l], dn_dt_bias[l], dn_norm_w[l], sc_conv_w[l], w_out[l],
                    pre_ffn_norm_w[l], post_ffn_norm_w[l], w_ff1[l], w_ff2[l], bsz, seq)
    return x2.reshape(bsz, seq, d)
```
